```python
import math
import jax, jax.numpy as jnp
from jax import lax
import numpy as np

D_MODEL = 1024
BATCH = 4
SEQ = 4096
DEPTH = 2

CHUNK = 64
Q_BLOCK = 128
D_MIX = D_MODEL
SSD_WIDTH = D_MIX // 2
SSD_HEAD_DIM = 64
SSD_HEADS = SSD_WIDTH // SSD_HEAD_DIM
SSD_GROUPS = 2
SSD_HPG = SSD_HEADS // SSD_GROUPS
SSD_STATE = 128
SSD_CONV = 4
SSD_CONV_DIM = SSD_WIDTH + 2 * SSD_GROUPS * SSD_STATE
SSD_IN = SSD_WIDTH + SSD_CONV_DIM + SSD_HEADS
FOX_WIDTH = D_MIX // 4
FOX_HEAD_DIM = 64
FOX_HEADS = FOX_WIDTH // FOX_HEAD_DIM
FOX_IN = 3 * FOX_WIDTH + FOX_HEADS
SCONV_WIDTH = D_MIX - SSD_WIDTH - FOX_WIDTH
SCONV_K = 3
SCONV_IN = 3 * SCONV_WIDTH
D_IN_PROJ = SSD_IN + FOX_IN + SCONV_IN
D_FF = 2816
ALPHA = (2 * DEPTH) ** 0.25
BETA = (8 * DEPTH) ** -0.25
LN_EPS = 1e-5
RMS_EPS = 1e-5
N_SUB = 3

kernel_name = "hybrid_ssd_fox_shortconv_macaron_deepnorm_adaln"


def layer_norm(x, g, b):
    xf = x.astype(jnp.float32)
    mu = jnp.mean(xf, axis=-1, keepdims=True)
    var = jnp.mean(jnp.square(xf - mu), axis=-1, keepdims=True)
    return ((xf - mu) * lax.rsqrt(var + LN_EPS) * g + b).astype(x.dtype)


def causal_depthwise_conv(x, w, b=None):
    k_w, ch = w.shape
    y = lax.conv_general_dilated(
        x, w[:, None, :], window_strides=(1,), padding=[(k_w - 1, 0)],
        dimension_numbers=("NWC", "WIO", "NWC"), feature_group_count=ch)
    return y if b is None else y + b


def swiglu_ffn(h, w_in, w_out):
    gate, up = jnp.split(h @ w_in, 2, axis=-1)
    return (jax.nn.silu(gate) * up) @ w_out


def segsum(a):
    t = a.shape[-1]
    cs = jnp.cumsum(a, axis=-1)
    diff = cs[..., :, None] - cs[..., None, :]
    mask = jnp.tril(jnp.ones((t, t), dtype=bool))
    return jnp.where(mask, diff, -jnp.inf)


def ssd_chunked_scan(xdt, a, bm, cm):
    b, seq, g, e, p = xdt.shape
    n = bm.shape[-1]
    nc = seq // CHUNK
    xdt = xdt.reshape(b, nc, CHUNK, g, e, p)
    bm = bm.reshape(b, nc, CHUNK, g, n)
    cm = cm.reshape(b, nc, CHUNK, g, n)
    a = jnp.transpose(a.reshape(b, nc, CHUNK, g, e), (0, 3, 4, 1, 2))
    a_cs = jnp.cumsum(a, axis=-1)
    l_mat = jnp.exp(segsum(a))
    cb = jnp.einsum("bclgn,bcsgn->bgcls", cm, bm)
    y_diag = jnp.einsum("bgcls,bgecls,bcsgep->bclgep", cb, l_mat, xdt)
    decay_states = jnp.exp(a_cs[..., -1:] - a_cs)
    states = jnp.einsum("bclgn,bgecl,bclgep->bcgepn", bm, decay_states, xdt)
    chunk_a = jnp.pad(a_cs[..., -1], ((0, 0), (0, 0), (0, 0), (1, 0)))
    decay_chunk = jnp.exp(segsum(chunk_a))
    states = jnp.concatenate([jnp.zeros_like(states[:, :1]), states], axis=1)
    states_in = jnp.einsum("bgezc,bcgepn->bzgepn", decay_chunk, states)[:, :-1]
    y_off = jnp.einsum("bclgn,bcgepn,bgecl->bclgep", cm, states_in, jnp.exp(a_cs))
    return (y_diag + y_off).reshape(b, seq, g, e, p)


def mamba2_group(proj, conv_w, conv_b, dt_bias, a_log, d_skip, norm_g):
    b, seq, _ = proj.shape
    z, xbc, dt_raw = jnp.split(proj, [SSD_WIDTH, SSD_WIDTH + SSD_CONV_DIM], axis=-1)
    xbc = jax.nn.silu(causal_depthwise_conv(xbc, conv_w, conv_b))
    xs, bm, cm = jnp.split(xbc, [SSD_WIDTH, SSD_WIDTH + SSD_GROUPS * SSD_STATE], axis=-1)
    xs = xs.astype(jnp.float32).reshape(b, seq, SSD_GROUPS, SSD_HPG, SSD_HEAD_DIM)
    bm = bm.astype(jnp.float32).reshape(b, seq, SSD_GROUPS, SSD_STATE)
    cm = cm.astype(jnp.float32).reshape(b, seq, SSD_GROUPS, SSD_STATE)
    dt = jax.nn.softplus(dt_raw.astype(jnp.float32) + dt_bias.astype(jnp.float32))
    dt = dt.reshape(b, seq, SSD_GROUPS, SSD_HPG)
    a_head = -jnp.exp(a_log.astype(jnp.float32)).reshape(SSD_GROUPS, SSD_HPG)
    y = ssd_chunked_scan(xs * dt[..., None], dt * a_head, bm, cm)
    y = y + d_skip.astype(jnp.float32).reshape(SSD_GROUPS, SSD_HPG)[:, :, None] * xs
    y = y.reshape(b, seq, SSD_WIDTH) * jax.nn.silu(z.astype(jnp.float32))
    yg = y.reshape(b, seq, SSD_GROUPS, SSD_WIDTH // SSD_GROUPS)
    yg = yg * lax.rsqrt(jnp.mean(jnp.square(yg), axis=-1, keepdims=True) + RMS_EPS)
    return (yg.reshape(b, seq, SSD_WIDTH) * norm_g).astype(proj.dtype)


def fox_group(proj, f_bias):
    b, seq, _ = proj.shape
    q, k, v, f = jnp.split(proj, [FOX_WIDTH, 2 * FOX_WIDTH, 3 * FOX_WIDTH], axis=-1)
    q = q.reshape(b, seq, FOX_HEADS, FOX_HEAD_DIM)
    k = k.reshape(b, seq, FOX_HEADS, FOX_HEAD_DIM)
    v = v.reshape(b, seq, FOX_HEADS, FOX_HEAD_DIM)
    log_f = jax.nn.log_sigmoid(f.astype(jnp.float32) + f_bias.astype(jnp.float32))
    cum_f = jnp.transpose(jnp.cumsum(log_f, axis=1), (0, 2, 1))
    scale = FOX_HEAD_DIM ** -0.5
    outs = []
    for i in range(seq // Q_BLOCK):
        q0, end = i * Q_BLOCK, (i + 1) * Q_BLOCK
        s = jnp.einsum("bqhd,bkhd->bhqk", q[:, q0:end], k[:, :end]).astype(jnp.float32) * scale
        s = s + cum_f[:, :, q0:end, None] - cum_f[:, :, None, :end]
        mask = jnp.arange(q0, end)[:, None] >= jnp.arange(end)[None, :]
        prob = jax.nn.softmax(jnp.where(mask, s, -jnp.inf), axis=-1)
        outs.append(jnp.einsum("bhqk,bkhd->bqhd", prob.astype(v.dtype), v[:, :end]))
    return jnp.concatenate(outs, axis=1).reshape(b, seq, FOX_WIDTH)


def shortconv_group(proj, conv_w):
    bg, cg, xin = jnp.split(proj, 3, axis=-1)
    return bg * causal_depthwise_conv(cg * xin, conv_w)


def modulate(x, shift, scale):
    return x * (1.0 + scale) + shift


def setup_inputs(seed: int = 0) -> dict:
    key = jax.random.key(seed)
    ks = jax.random.split(key, 24)

    def nrm(k, shape, s):
        return jax.random.normal(k, shape, jnp.float32) * s

    dt0 = jnp.exp(jax.random.uniform(ks[12], (DEPTH, SSD_HEADS), jnp.float32,
                                     minval=math.log(1e-3), maxval=math.log(1e-1)))
    return {
        "x": nrm(ks[0], (BATCH, SEQ, D_MODEL), 1.0),
        "c": nrm(ks[1], (BATCH, D_MODEL), 1.0),
        "ln_in_g": 1.0 + nrm(ks[2], (D_MODEL,), 0.01),
        "ln_in_b": nrm(ks[3], (D_MODEL,), 0.01),
        "ada_w": nrm(ks[4], (DEPTH, D_MODEL, N_SUB * 3 * D_MODEL), 0.5 * D_MODEL ** -0.5),
        "ada_b": nrm(ks[5], (DEPTH, N_SUB * 3 * D_MODEL), 0.01),
        "ffn1_w_in": nrm(ks[6], (DEPTH, D_MODEL, 2 * D_FF), D_MODEL ** -0.5),
        "ffn1_w_out": nrm(ks[7], (DEPTH, D_FF, D_MODEL), BETA * D_FF ** -0.5),
        "mix_w_in": nrm(ks[8], (DEPTH, D_MODEL, D_IN_PROJ), D_MODEL ** -0.5),
        "mix_w_out": nrm(ks[9], (DEPTH, D_MIX, D_MODEL), BETA * D_MIX ** -0.5),
        "ssd_conv_w": nrm(ks[10], (DEPTH, SSD_CONV, SSD_CONV_DIM), SSD_CONV ** -0.5),
        "ssd_conv_b": nrm(ks[11], (DEPTH, SSD_CONV_DIM), 0.01),
        "ssd_dt_bias": dt0 + jnp.log(-jnp.expm1(-dt0)),
        "ssd_a_log": jnp.log(jax.random.uniform(ks[13], (DEPTH, SSD_HEADS), jnp.float32,
                                                 minval=1.0, maxval=16.0)),
        "ssd_d": 1.0 + nrm(ks[14], (DEPTH, SSD_HEADS), 0.01),
        "ssd_norm_g": 1.0 + nrm(ks[15], (DEPTH, SSD_WIDTH), 0.01),
        "fox_f_bias": jax.random.uniform(ks[16], (DEPTH, FOX_HEADS), jnp.float32,
                                         minval=1.0, maxval=5.0),
        "sconv_w": nrm(ks[17], (DEPTH, SCONV_K, SCONV_WIDTH), SCONV_K ** -0.5),
        "ffn2_w_in": nrm(ks[18], (DEPTH, D_MODEL, 2 * D_FF), D_MODEL ** -0.5),
        "ffn2_w_out": nrm(ks[19], (DEPTH, D_FF, D_MODEL), BETA * D_FF ** -0.5),
        "ln_g": 1.0 + nrm(ks[20], (DEPTH, N_SUB, D_MODEL), 0.01),
        "ln_b": nrm(ks[21], (DEPTH, N_SUB, D_MODEL), 0.01),
    }


def reference(x, c, ln_in_g, ln_in_b, ada_w, ada_b, ffn1_w_in, ffn1_w_out,
              mix_w_in, mix_w_out, ssd_conv_w, ssd_conv_b, ssd_dt_bias, ssd_a_log,
              ssd_d, ssd_norm_g, fox_f_bias, sconv_w, ffn2_w_in, ffn2_w_out,
              ln_g, ln_b):
    b = x.shape[0]
    x = layer_norm(x, ln_in_g, ln_in_b)
    c_act = jax.nn.silu(c)
    for l in range(DEPTH):
        mod = (c_act @ ada_w[l] + ada_b[l]).reshape(b, N_SUB, 3, 1, D_MODEL)

        h = modulate(x, mod[:, 0, 0], mod[:, 0, 1])
        y = swiglu_ffn(h, ffn1_w_in[l], ffn1_w_out[l])
        x = layer_norm(ALPHA * x + 0.5 * mod[:, 0, 2] * y, ln_g[l, 0], ln_b[l, 0])

        h = modulate(x, mod[:, 1, 0], mod[:, 1, 1])
        proj = h @ mix_w_in[l]
        p_ssd, p_fox, p_sc = jnp.split(proj, [SSD_IN, SSD_IN + FOX_IN], axis=-1)
        y_ssd = mamba2_group(p_ssd, ssd_conv_w[l], ssd_conv_b[l], ssd_dt_bias[l],
                             ssd_a_log[l], ssd_d[l], ssd_norm_g[l])
        y_fox = fox_group(p_fox, fox_f_bias[l])
        y_sc = shortconv_group(p_sc, sconv_w[l])
        y = jnp.concatenate([y_ssd, y_fox, y_sc], axis=-1) @ mix_w_out[l]
        x = layer_norm(ALPHA * x + mod[:, 1, 2] * y, ln_g[l, 1], ln_b[l, 1])

        h = modulate(x, mod[:, 2, 0], mod[:, 2, 1])
        y = swiglu_ffn(h, ffn2_w_in[l], ffn2_w_out[l])
        x = layer_norm(ALPHA * x + 0.5 * mod[:, 2, 2] * y, ln_g[l, 2], ln_b[l, 2])
    return x
```

```python
import functools
import math

import jax
import jax.numpy as jnp
from jax import lax
from jax.experimental import pallas as pl
from jax.experimental.pallas import tpu as pltpu

F32 = jnp.float32
BF16 = jnp.bfloat16
HIGHEST = lax.Precision.HIGHEST

D_MODEL = 1024
DEPTH = 2
N_SUB = 3
D_FF = 2816
SSD_WIDTH = 512
SSD_HEAD_DIM = 64
SSD_HEADS = 8
SSD_GROUPS = 2
SSD_HPG = SSD_HEADS // SSD_GROUPS
SSD_STATE = 128
SSD_CONV = 4
SSD_BC = SSD_GROUPS * SSD_STATE
SSD_CONV_DIM = SSD_WIDTH + 2 * SSD_BC
GROUP_WIDTH = SSD_WIDTH // SSD_GROUPS
FOX_WIDTH = 256
FOX_HEAD_DIM = 64
FOX_HEADS = 4
SCONV_WIDTH = 256
SCONV_K = 3
ALPHA = (2 * DEPTH) ** 0.25
LN_EPS = 1e-5
RMS_EPS = 1e-5

LANES = 128
SUBLANES = 8
VMEM_LIMIT = 56 * 1024 * 1024

COL_Z = 0
COL_XBC = COL_Z + SSD_WIDTH
COL_QKV = COL_XBC + SSD_CONV_DIM
COL_SC = COL_QKV + 3 * FOX_WIDTH
COL_SMALL = COL_SC + 3 * SCONV_WIDTH
D_PROJ = COL_SMALL + LANES
LANE_DT = 0
LANE_A = 8
LANE_CF = 16

FFN_TM = 1024
FFN_TF = 256
PROJ_TM = 512
SSD_T = 256
FOX_T = 512
OUT_TM = 1024
ADA_TN = 1024


def _layer_norm(x, g, b):
    mu = jnp.mean(x, axis=-1, keepdims=True)
    xc = x - mu
    var = jnp.mean(xc * xc, axis=-1, keepdims=True)
    return xc * lax.rsqrt(var + LN_EPS) * g + b


def _silu(x):
    return x / (1.0 + jnp.exp(-x))


def _softplus(x):
    return jnp.maximum(x, 0.0) + jnp.log1p(jnp.exp(-jnp.abs(x)))


def _params(*semantics):
    return pltpu.CompilerParams(dimension_semantics=semantics, vmem_limit_bytes=VMEM_LIMIT)


def _ada_kernel(c_ref, w_ref, b_ref, o_ref):
    o_ref[...] = jnp.dot(_silu(c_ref[...]), w_ref[...], preferred_element_type=F32,
                         precision=HIGHEST) + b_ref[...]


def _ada(c_pad, ada_w, ada_b):
    depth, d, n = ada_w.shape
    rows = c_pad.shape[0]
    return pl.pallas_call(
        _ada_kernel,
        grid=(depth, n // ADA_TN),
        in_specs=[
            pl.BlockSpec((rows, d), lambda l, j: (0, 0)),
            pl.BlockSpec((None, d, ADA_TN), lambda l, j: (l, 0, j)),
            pl.BlockSpec((None, 1, ADA_TN), lambda l, j: (l, 0, j)),
        ],
        out_specs=pl.BlockSpec((None, rows, ADA_TN), lambda l, j: (l, 0, j)),
        out_shape=jax.ShapeDtypeStruct((depth, rows, n), F32),
        compiler_params=_params("arbitrary", "arbitrary"),
    )(c_pad, ada_w, ada_b.reshape(depth, 1, n))


def _ffn_kernel(*refs, sub, pre_ln, n_ff):
    if pre_ln:
        (x_ref, mod_ref, wg_ref, wu_ref, wo_ref, lng_ref, lnb_ref, ing_ref, inb_ref,
         o_ref, h_scr, acc_scr, xin_scr) = refs
    else:
        (x_ref, mod_ref, wg_ref, wu_ref, wo_ref, lng_ref, lnb_ref,
         o_ref, h_scr, acc_scr) = refs
    j = pl.program_id(1)

    @pl.when(j == 0)
    def _():
        x = x_ref[...]
        if pre_ln:
            x = _layer_norm(x, ing_ref[...], inb_ref[...])
            xin_scr[...] = x
        shift = mod_ref[3 * sub:3 * sub + 1, :]
        scale = mod_ref[3 * sub + 1:3 * sub + 2, :]
        h_scr[...] = (x * (1.0 + scale) + shift).astype(BF16)
        acc_scr[...] = jnp.zeros_like(acc_scr)

    h = h_scr[...]
    gate = jnp.dot(h, wg_ref[...], preferred_element_type=F32)
    up = jnp.dot(h, wu_ref[...], preferred_element_type=F32)
    act = (_silu(gate) * up).astype(BF16)
    acc_scr[...] += jnp.dot(act, wo_ref[...], preferred_element_type=F32)

    @pl.when(j == n_ff - 1)
    def _():
        x = xin_scr[...] if pre_ln else x_ref[...]
        g = mod_ref[3 * sub + 2:3 * sub + 3, :]
        r = ALPHA * x + (0.5 * g) * acc_scr[...]
        o_ref[...] = _layer_norm(r, lng_ref[...], lnb_ref[...])


def _ffn(x2d, mod_l, w_in, w_out, ln_g, ln_b, *, sub, seq, pre=None):
    n_tok, d = x2d.shape
    tm = min(FFN_TM, seq)
    assert seq % tm == 0 and n_tok % seq == 0 and D_FF % FFN_TF == 0
    tiles_per_seq = seq // tm
    n_ff = D_FF // FFN_TF
    row = lambda v: v.reshape(1, d)
    in_specs = [
        pl.BlockSpec((tm, d), lambda i, j: (i, 0)),
        pl.BlockSpec((None, 3 * N_SUB, d), lambda i, j: (i // tiles_per_seq, 0, 0)),
        pl.BlockSpec((d, FFN_TF), lambda i, j: (0, j)),
        pl.BlockSpec((d, FFN_TF), lambda i, j: (0, j + n_ff)),
        pl.BlockSpec((FFN_TF, d), lambda i, j: (j, 0)),
        pl.BlockSpec((1, d), lambda i, j: (0, 0)),
        pl.BlockSpec((1, d), lambda i, j: (0, 0)),
    ]
    args = [x2d, mod_l, w_in, w_in, w_out, row(ln_g), row(ln_b)]
    scratch = [pltpu.VMEM((tm, d), BF16), pltpu.VMEM((tm, d), F32)]
    if pre is not None:
        in_specs += [pl.BlockSpec((1, d), lambda i, j: (0, 0))] * 2
        args += [row(pre[0]), row(pre[1])]
        scratch.append(pltpu.VMEM((tm, d), F32))
    return pl.pallas_call(
        functools.partial(_ffn_kernel, sub=sub, pre_ln=pre is not None, n_ff=n_ff),
        grid=(n_tok // tm, n_ff),
        in_specs=in_specs,
        out_specs=pl.BlockSpec((tm, d), lambda i, j: (i, 0)),
        out_shape=jax.ShapeDtypeStruct((n_tok, d), F32),
        scratch_shapes=scratch,
        compiler_params=_params("arbitrary", "arbitrary"),
    )(*args)


def _proj_kernel(x_ref, mod_ref, w_ref, cw_ref, cb_ref, sbias_ref, alog_ref, scw_ref,
                 z_ref, xs_ref, bm_ref, cm_ref, q_ref, k_ref, v_ref, ysc_ref, small_ref, cft_ref,
                 xbc_scr, u_scr, carry_scr, *, tm):
    t = pl.program_id(1)
    halo = SUBLANES

    @pl.when(t == 0)
    def _():
        xbc_scr[0:halo, :] = jnp.zeros((halo, SSD_CONV_DIM), F32)
        u_scr[0:halo, :] = jnp.zeros((halo, SCONV_WIDTH), F32)
        carry_scr[...] = jnp.zeros_like(carry_scr)

    @pl.when(t > 0)
    def _():
        xbc_scr[0:halo, :] = xbc_scr[tm:tm + halo, :]
        u_scr[0:halo, :] = u_scr[tm:tm + halo, :]

    shift = mod_ref[3:4, :]
    scale = mod_ref[4:5, :]
    h = (x_ref[0] * (1.0 + scale) + shift).astype(BF16)

    def seg(start, width):
        return jnp.dot(h, w_ref[:, start:start + width], preferred_element_type=F32)

    z_ref[0] = seg(COL_Z, SSD_WIDTH)

    xbc_scr[halo:halo + tm, :] = seg(COL_XBC, SSD_CONV_DIM)
    conv = cb_ref[...] + cw_ref[SSD_CONV - 1:SSD_CONV, :] * xbc_scr[halo:halo + tm, :]
    for k in range(SSD_CONV - 1):
        back = SSD_CONV - 1 - k
        conv = conv + cw_ref[k:k + 1, :] * xbc_scr[halo - back:halo - back + tm, :]
    xbc = _silu(conv)
    xs_ref[0] = xbc[:, :SSD_WIDTH]
    bm_ref[0] = xbc[:, SSD_WIDTH:SSD_WIDTH + SSD_BC].astype(BF16)
    cm_ref[0] = xbc[:, SSD_WIDTH + SSD_BC:].astype(BF16)

    qkv = seg(COL_QKV, 3 * FOX_WIDTH)
    q_ref[0] = (qkv[:, :FOX_WIDTH] * (FOX_HEAD_DIM ** -0.5)).astype(BF16)
    k_ref[0] = qkv[:, FOX_WIDTH:2 * FOX_WIDTH].astype(BF16)
    v_ref[0] = qkv[:, 2 * FOX_WIDTH:].astype(BF16)

    sc = seg(COL_SC, 3 * SCONV_WIDTH)
    u_scr[halo:halo + tm, :] = sc[:, SCONV_WIDTH:2 * SCONV_WIDTH] * sc[:, 2 * SCONV_WIDTH:]
    cu = scw_ref[SCONV_K - 1:SCONV_K, :] * u_scr[halo:halo + tm, :]
    for k in range(SCONV_K - 1):
        back = SCONV_K - 1 - k
        cu = cu + scw_ref[k:k + 1, :] * u_scr[halo - back:halo - back + tm, :]
    ysc_ref[0] = (sc[:, :SCONV_WIDTH] * cu).astype(BF16)

    raw = seg(COL_SMALL, LANES) + sbias_ref[...]
    lane = lax.broadcasted_iota(jnp.int32, (tm, LANES), 1)
    dt = _softplus(raw)
    a_neg = -jnp.exp(alog_ref[...])
    log_f = jnp.where((lane >= LANE_CF) & (lane < LANE_CF + FOX_HEADS), -_softplus(-raw), 0.0)
    row_i = lax.broadcasted_iota(jnp.int32, (tm, tm), 0)
    col_i = lax.broadcasted_iota(jnp.int32, (tm, tm), 1)
    tri = jnp.where(row_i >= col_i, 1.0, 0.0).astype(F32)
    cum_f = jnp.dot(tri, log_f, preferred_element_type=F32, precision=HIGHEST) + carry_scr[0:1, :]
    carry_scr[...] = jnp.broadcast_to(cum_f[tm - 1:tm, :], carry_scr.shape)
    small = jnp.where(lane < LANE_A, dt, jnp.where(lane < LANE_CF, dt * a_neg, cum_f))
    small_ref[0] = small
    cft_ref[0] = cum_f.T[LANE_CF:LANE_CF + SUBLANES, :]


def _proj(x, mod_l, w_cat, conv_w, conv_b, small_bias, alog_vec, sconv_w):
    b, seq, d = x.shape
    tm = min(PROJ_TM, seq)
    assert seq % tm == 0
    nt = seq // tm
    tok = lambda width: pl.BlockSpec((1, tm, width), lambda bi, t: (bi, t, 0))
    const2 = lambda shape: pl.BlockSpec(shape, lambda bi, t: (0, 0))
    out_shapes = (
        jax.ShapeDtypeStruct((b, seq, SSD_WIDTH), F32),
        jax.ShapeDtypeStruct((b, seq, SSD_WIDTH), F32),
        jax.ShapeDtypeStruct((b, seq, SSD_BC), BF16),
        jax.ShapeDtypeStruct((b, seq, SSD_BC), BF16),
        jax.ShapeDtypeStruct((b, seq, FOX_WIDTH), BF16),
        jax.ShapeDtypeStruct((b, seq, FOX_WIDTH), BF16),
        jax.ShapeDtypeStruct((b, seq, FOX_WIDTH), BF16),
        jax.ShapeDtypeStruct((b, seq, SCONV_WIDTH), BF16),
        jax.ShapeDtypeStruct((b, seq, LANES), F32),
        jax.ShapeDtypeStruct((b, SUBLANES, seq), F32),
    )
    out_specs = (
        tok(SSD_WIDTH), tok(SSD_WIDTH), tok(SSD_BC), tok(SSD_BC),
        tok(FOX_WIDTH), tok(FOX_WIDTH), tok(FOX_WIDTH), tok(SCONV_WIDTH), tok(LANES),
        pl.BlockSpec((1, SUBLANES, tm), lambda bi, t: (bi, 0, t)),
    )
    return pl.pallas_call(
        functools.partial(_proj_kernel, tm=tm),
        grid=(b, nt),
        in_specs=[
            tok(d),
            pl.BlockSpec((None, 3 * N_SUB, d), lambda bi, t: (bi, 0, 0)),
            const2((d, D_PROJ)),
            const2((SSD_CONV, SSD_CONV_DIM)),
            const2((1, SSD_CONV_DIM)),
            const2((1, LANES)),
            const2((1, LANES)),
            const2((SCONV_K, SCONV_WIDTH)),
        ],
        out_specs=out_specs,
        out_shape=out_shapes,
        scratch_shapes=[
            pltpu.VMEM((tm + 2 * SUBLANES, SSD_CONV_DIM), F32),
            pltpu.VMEM((tm + 2 * SUBLANES, SCONV_WIDTH), F32),
            pltpu.VMEM((SUBLANES, LANES), F32),
        ],
        compiler_params=_params("arbitrary", "arbitrary"),
    )(x, mod_l, w_cat, conv_w, conv_b, small_bias, alog_vec, sconv_w)


def _ssd_kernel(xs_ref, bm_ref, cm_ref, z_ref, small_ref, dskip_ref, ng_ref, o_ref, state_scr, *, t):
    c = pl.program_id(1)

    @pl.when(c == 0)
    def _():
        state_scr[...] = jnp.zeros_like(state_scr)

    small = small_ref[0]
    lane = lax.broadcasted_iota(jnp.int32, (t, LANES), 1)
    a_only = jnp.where((lane >= LANE_A) & (lane < LANE_A + SSD_HEADS), small, 0.0)
    row_i = lax.broadcasted_iota(jnp.int32, (t, t), 0)
    col_i = lax.broadcasted_iota(jnp.int32, (t, t), 1)
    lower = row_i >= col_i
    tri = jnp.where(lower, 1.0, 0.0).astype(F32)
    cs = jnp.dot(tri, a_only, preferred_element_type=F32, precision=HIGHEST)
    cs_t = cs.T

    src = lax.broadcasted_iota(jnp.int32, (LANES, SSD_WIDTH), 0)
    dst_head = lax.broadcasted_iota(jnp.int32, (LANES, SSD_WIDTH), 1) // SSD_HEAD_DIM
    spread_dt = jnp.where(src == dst_head + LANE_DT, 1.0, 0.0).astype(F32)
    spread_a = jnp.where(src == dst_head + LANE_A, 1.0, 0.0).astype(F32)
    dt_x = jnp.dot(small, spread_dt, preferred_element_type=F32, precision=HIGHEST)
    cs_x = jnp.dot(cs, spread_a, preferred_element_type=F32, precision=HIGHEST)
    tot_x = cs_x[t - 1:t, :]

    xs = xs_ref[0]
    xdt = xs * dt_x
    decay_in = jnp.exp(cs_x)
    decay_out = jnp.exp(tot_x - cs_x)
    decay_chunk = jnp.exp(tot_x)
    head_of_lane = lax.broadcasted_iota(jnp.int32, (t, GROUP_WIDTH), 1) // SSD_HEAD_DIM

    ys = []
    for g in range(SSD_GROUPS):
        gs = slice(g * GROUP_WIDTH, (g + 1) * GROUP_WIDTH)
        ns = slice(g * SSD_STATE, (g + 1) * SSD_STATE)
        bg = bm_ref[0, :, ns]
        cg = cm_ref[0, :, ns]
        cb = lax.dot_general(cg, bg, (((1,), (1,)), ((), ())), preferred_element_type=F32)
        xg = xdt[:, gs]
        m_parts, x_parts = [], []
        for e in range(SSD_HPG):
            hd = g * SSD_HPG + e
            col = cs[:, LANE_A + hd:LANE_A + hd + 1]
            row = cs_t[LANE_A + hd:LANE_A + hd + 1, :]
            l_mat = jnp.exp(jnp.where(lower, col - row, -jnp.inf))
            m_parts.append((cb * l_mat).astype(BF16))
            x_parts.append(jnp.where(head_of_lane == e, xg, 0.0).astype(BF16))
        m_cat = jnp.concatenate(m_parts, axis=1)
        x_blk = jnp.concatenate(x_parts, axis=0)
        y_diag = jnp.dot(m_cat, x_blk, preferred_element_type=F32)

        state = state_scr[g]
        y_off = jnp.dot(cg, state.astype(BF16), preferred_element_type=F32) * decay_in[:, gs]
        xd = (xg * decay_out[:, gs]).astype(BF16)
        upd = lax.dot_general(bg, xd, (((0,), (0,)), ((), ())), preferred_element_type=F32)
        state_scr[g] = state * decay_chunk[:, gs] + upd
        ys.append(y_diag + y_off + dskip_ref[:, gs] * xs[:, gs])

    z = z_ref[0]
    outs = []
    for g in range(SSD_GROUPS):
        gs = slice(g * GROUP_WIDTH, (g + 1) * GROUP_WIDTH)
        yg = ys[g] * _silu(z[:, gs])
        yg = yg * lax.rsqrt(jnp.mean(yg * yg, axis=-1, keepdims=True) + RMS_EPS)
        outs.append((yg * ng_ref[:, gs]).astype(BF16))
    o_ref[0] = jnp.concatenate(outs, axis=1)


def _ssd(xs, bm, cm, z, small, d_skip_x, norm_g):
    b, seq, _ = xs.shape
    t = min(SSD_T, seq)
    assert seq % t == 0
    tok = lambda width: pl.BlockSpec((1, t, width), lambda bi, c: (bi, c, 0))
    const2 = lambda shape: pl.BlockSpec(shape, lambda bi, c: (0, 0))
    return pl.pallas_call(
        functools.partial(_ssd_kernel, t=t),
        grid=(b, seq // t),
        in_specs=[tok(SSD_WIDTH), tok(SSD_BC), tok(SSD_BC), tok(SSD_WIDTH), tok(LANES),
                  const2((1, SSD_WIDTH)), const2((1, SSD_WIDTH))],
        out_specs=tok(SSD_WIDTH),
        out_shape=jax.ShapeDtypeStruct((b, seq, SSD_WIDTH), BF16),
        scratch_shapes=[pltpu.VMEM((SSD_GROUPS, SSD_STATE, GROUP_WIDTH), F32)],
        compiler_params=_params("arbitrary", "arbitrary"),
    )(xs, bm, cm, z, small, d_skip_x, norm_g)


def _fox_kernel(q_ref, k_ref, v_ref, small_ref, cft_ref, o_ref, *, t):
    qi = pl.program_id(1)
    lane_head = lax.broadcasted_iota(jnp.int32, (t, LANES), 1) // FOX_HEAD_DIM
    row_i = lax.broadcasted_iota(jnp.int32, (t, t), 0)
    col_i = lax.broadcasted_iota(jnp.int32, (t, t), 1)
    causal = row_i >= col_i
    heads_per_vreg = LANES // FOX_HEAD_DIM

    for pair in range(FOX_HEADS // heads_per_vreg):
        ls = slice(pair * LANES, (pair + 1) * LANES)
        q2 = q_ref[0, :, ls]
        out = jnp.zeros((t, LANES), F32)
        for hh in range(heads_per_vreg):
            hd = pair * heads_per_vreg + hh
            mine = lane_head == hh
            qm = jnp.where(mine, q2, jnp.zeros_like(q2))
            cf_q = small_ref[0, :, LANE_CF + hd:LANE_CF + hd + 1]

            def block(j, carry, masked):
                m_prev, l_prev, acc = carry
                ks = pl.ds(pl.multiple_of(j * t, t), t)
                k2 = k_ref[0, ks, ls]
                v2 = v_ref[0, ks, ls]
                s = lax.dot_general(qm, k2, (((1,), (1,)), ((), ())), preferred_element_type=F32)
                s = s - cft_ref[0, hd:hd + 1, ks]
                if masked:
                    s = jnp.where(causal, s, -jnp.inf)
                m_new = jnp.maximum(m_prev, jnp.max(s, axis=-1, keepdims=True) + cf_q)
                alpha = jnp.exp(m_prev - m_new)
                p = jnp.exp(s - (m_new - cf_q))
                l_new = alpha * l_prev + jnp.sum(p, axis=-1, keepdims=True)
                vm = jnp.where(mine, v2, jnp.zeros_like(v2))
                acc = alpha * acc + jnp.dot(p.astype(BF16), vm, preferred_element_type=F32)
                return m_new, l_new, acc

            init = (jnp.full((t, 1), -jnp.inf, F32), jnp.zeros((t, 1), F32), jnp.zeros((t, LANES), F32))
            carry = lax.fori_loop(0, qi, lambda j, cr: block(j, cr, False), init)
            _, l_fin, acc = block(qi, carry, True)
            out = out + acc / l_fin
        o_ref[0, :, ls] = out.astype(BF16)


def _fox(q, k, v, small, cft):
    b, seq, _ = q.shape
    t = min(FOX_T, seq)
    assert seq % t == 0
    return pl.pallas_call(
        functools.partial(_fox_kernel, t=t),
        grid=(b, seq // t),
        in_specs=[
            pl.BlockSpec((1, t, FOX_WIDTH), lambda bi, i: (bi, i, 0)),
            pl.BlockSpec((1, seq, FOX_WIDTH), lambda bi, i: (bi, 0, 0)),
            pl.BlockSpec((1, seq, FOX_WIDTH), lambda bi, i: (bi, 0, 0)),
            pl.BlockSpec((1, t, LANES), lambda bi, i: (bi, i, 0)),
            pl.BlockSpec((1, SUBLANES, seq), lambda bi, i: (bi, 0, 0)),
        ],
        out_specs=pl.BlockSpec((1, t, FOX_WIDTH), lambda bi, i: (bi, i, 0)),
        out_shape=jax.ShapeDtypeStruct((b, seq, FOX_WIDTH), BF16),
        compiler_params=_params("arbitrary", "arbitrary"),
    )(q, k, v, small, cft)


def _outproj_kernel(x_ref, mod_ref, yssd_ref, yfox_ref, ysc_ref, w_ref, lng_ref, lnb_ref, o_ref):
    y = jnp.dot(yssd_ref[...], w_ref[0:SSD_WIDTH, :], preferred_element_type=F32)
    y = y + jnp.dot(yfox_ref[...], w_ref[SSD_WIDTH:SSD_WIDTH + FOX_WIDTH, :], preferred_element_type=F32)
    y = y + jnp.dot(ysc_ref[...], w_ref[SSD_WIDTH + FOX_WIDTH:, :], preferred_element_type=F32)
    g = mod_ref[5:6, :]
    o_ref[...] = _layer_norm(ALPHA * x_ref[...] + g * y, lng_ref[...], lnb_ref[...])


def _outproj(x2d, mod_l, y_ssd, y_fox, y_sc, w_out, ln_g, ln_b, *, seq):
    n_tok, d = x2d.shape
    tm = min(OUT_TM, seq)
    assert seq % tm == 0
    tiles_per_seq = seq // tm
    tok = lambda width: pl.BlockSpec((tm, width), lambda i: (i, 0))
    return pl.pallas_call(
        _outproj_kernel,
        grid=(n_tok // tm,),
        in_specs=[
            tok(d),
            pl.BlockSpec((None, 3 * N_SUB, d), lambda i: (i // tiles_per_seq, 0, 0)),
            tok(SSD_WIDTH), tok(FOX_WIDTH), tok(SCONV_WIDTH),
            pl.BlockSpec((d, d), lambda i: (0, 0)),
            pl.BlockSpec((1, d), lambda i: (0, 0)),
            pl.BlockSpec((1, d), lambda i: (0, 0)),
        ],
        out_specs=tok(d),
        out_shape=jax.ShapeDtypeStruct((n_tok, d), F32),
        compiler_params=_params("arbitrary"),
    )(x2d, mod_l, y_ssd, y_fox, y_sc, w_out, ln_g.reshape(1, d), ln_b.reshape(1, d))


def _pack_mix_w_in(w):
    ssd_in = SSD_WIDTH + SSD_CONV_DIM + SSD_HEADS
    fox_in = 3 * FOX_WIDTH + FOX_HEADS
    dt_cols = w[:, SSD_WIDTH + SSD_CONV_DIM:ssd_in]
    f_cols = w[:, ssd_in + 3 * FOX_WIDTH:ssd_in + fox_in]
    pad = jnp.zeros((w.shape[0], LANES - 2 * SSD_HEADS - FOX_HEADS), w.dtype)
    return jnp.concatenate([
        w[:, :SSD_WIDTH + SSD_CONV_DIM],
        w[:, ssd_in:ssd_in + 3 * FOX_WIDTH],
        w[:, ssd_in + fox_in:],
        dt_cols, dt_cols, f_cols, pad,
    ], axis=1).astype(BF16)


def _lane_vec(pieces):
    row = jnp.zeros((LANES,), F32)
    for off, vec in pieces:
        row = row.at[off:off + vec.shape[0]].set(vec.astype(F32))
    return row.reshape(1, LANES)


def kernel(x, c, ln_in_g, ln_in_b, ada_w, ada_b, ffn1_w_in, ffn1_w_out, mix_w_in, mix_w_out, ssd_conv_w, ssd_conv_b, ssd_dt_bias, ssd_a_log, ssd_d, ssd_norm_g, fox_f_bias, sconv_w, ffn2_w_in, ffn2_w_out, ln_g, ln_b):
    b, seq, d = x.shape
    depth = ada_w.shape[0]
    rows = -(-b // SUBLANES) * SUBLANES
    c_pad = jnp.pad(c, ((0, rows - b), (0, 0)))
    mod = _ada(c_pad, ada_w, ada_b).reshape(depth, rows, 3 * N_SUB, d)

    x2d = x.reshape(b * seq, d)
    for l in range(depth):
        mod_l = mod[l]
        x2d = _ffn(x2d, mod_l, ffn1_w_in[l].astype(BF16), ffn1_w_out[l].astype(BF16),
                   ln_g[l, 0], ln_b[l, 0], sub=0, seq=seq,
                   pre=(ln_in_g, ln_in_b) if l == 0 else None)

        small_bias = _lane_vec([(LANE_DT, ssd_dt_bias[l]), (LANE_A, ssd_dt_bias[l]), (LANE_CF, fox_f_bias[l])])
        alog_vec = _lane_vec([(LANE_A, ssd_a_log[l])])
        z, xs, bm, cm, q, k, v, y_sc, small, cft = _proj(
            x2d.reshape(b, seq, d), mod_l, _pack_mix_w_in(mix_w_in[l]),
            ssd_conv_w[l], ssd_conv_b[l].reshape(1, SSD_CONV_DIM), small_bias, alog_vec, sconv_w[l])
        d_skip_x = jnp.repeat(ssd_d[l], SSD_HEAD_DIM).reshape(1, SSD_WIDTH)
        y_ssd = _ssd(xs, bm, cm, z, small, d_skip_x, ssd_norm_g[l].reshape(1, SSD_WIDTH))
        y_fox = _fox(q, k, v, small, cft)
        x2d = _outproj(x2d, mod_l, y_ssd.reshape(b * seq, SSD_WIDTH), y_fox.reshape(b * seq, FOX_WIDTH),
                       y_sc.reshape(b * seq, SCONV_WIDTH), mix_w_out[l].astype(BF16),
                       ln_g[l, 1], ln_b[l, 1], seq=seq)

        x2d = _ffn(x2d, mod_l, ffn2_w_in[l].astype(BF16), ffn2_w_out[l].astype(BF16),
                   ln_g[l, 2], ln_b[l, 2], sub=2, seq=seq)
    return x2d.reshape(b, seq, d)
```

```python
import functools
import math

import jax
import jax.numpy as jnp
from jax import lax
from jax.experimental import pallas as pl
from jax.experimental.pallas import tpu as pltpu

F32 = jnp.float32
BF16 = jnp.bfloat16
HIGHEST = lax.Precision.HIGHEST

D_MODEL = 1024
DEPTH = 2
N_SUB = 3
D_FF = 2816
SSD_WIDTH = 512
SSD_HEAD_DIM = 64
SSD_HEADS = 8
SSD_GROUPS = 2
SSD_HPG = SSD_HEADS // SSD_GROUPS
SSD_STATE = 128
SSD_CONV = 4
SSD_BC = SSD_GROUPS * SSD_STATE
SSD_CONV_DIM = SSD_WIDTH + 2 * SSD_BC
GROUP_WIDTH = SSD_WIDTH // SSD_GROUPS
FOX_WIDTH = 256
FOX_HEAD_DIM = 64
FOX_HEADS = 4
SCONV_WIDTH = 256
SCONV_K = 3
ALPHA = (2 * DEPTH) ** 0.25
LN_EPS = 1e-5
RMS_EPS = 1e-5

LANES = 128
SUBLANES = 8
VMEM_LIMIT = 56 * 1024 * 1024

COL_Z = 0
COL_XBC = COL_Z + SSD_WIDTH
COL_QKV = COL_XBC + SSD_CONV_DIM
COL_SC = COL_QKV + 3 * FOX_WIDTH
COL_SMALL = COL_SC + 3 * SCONV_WIDTH
COL_F = COL_SMALL + LANES
D_PROJ = COL_F + FOX_WIDTH
LANE_DT = 0
LANE_A = 8
FOX_PIECES = 3
FOX_GROUP = FOX_HEADS * LANES

FFN_TM = 1024
FFN_TF = 256
PROJ_TM = 512
SSD_T = 256
FOX_T = 512
OUT_TM = 1024
ADA_TN = 1024


def _layer_norm(x, g, b):
    mu = jnp.mean(x, axis=-1, keepdims=True)
    xc = x - mu
    var = jnp.mean(xc * xc, axis=-1, keepdims=True)
    return xc * lax.rsqrt(var + LN_EPS) * g + b


def _silu(x):
    return x / (1.0 + jnp.exp(-x))


def _softplus(x):
    return jnp.maximum(x, 0.0) + jnp.log1p(jnp.exp(-jnp.abs(x)))


def _params(*semantics):
    return pltpu.CompilerParams(dimension_semantics=semantics, vmem_limit_bytes=VMEM_LIMIT)


def _ada_kernel(c_ref, w_ref, b_ref, o_ref):
    o_ref[...] = jnp.dot(_silu(c_ref[...]), w_ref[...], preferred_element_type=F32,
                         precision=HIGHEST) + b_ref[...]


def _ada(c_pad, ada_w, ada_b):
    depth, d, n = ada_w.shape
    rows = c_pad.shape[0]
    return pl.pallas_call(
        _ada_kernel,
        grid=(depth, n // ADA_TN),
        in_specs=[
            pl.BlockSpec((rows, d), lambda l, j: (0, 0)),
            pl.BlockSpec((None, d, ADA_TN), lambda l, j: (l, 0, j)),
            pl.BlockSpec((None, 1, ADA_TN), lambda l, j: (l, 0, j)),
        ],
        out_specs=pl.BlockSpec((None, rows, ADA_TN), lambda l, j: (l, 0, j)),
        out_shape=jax.ShapeDtypeStruct((depth, rows, n), F32),
        compiler_params=_params("arbitrary", "arbitrary"),
    )(c_pad, ada_w, ada_b.reshape(depth, 1, n))


def _ffn_kernel(*refs, sub, pre_ln, n_ff):
    if pre_ln:
        (x_ref, mod_ref, wg_ref, wu_ref, wo_ref, lng_ref, lnb_ref, ing_ref, inb_ref,
         o_ref, h_scr, acc_scr, xin_scr) = refs
    else:
        (x_ref, mod_ref, wg_ref, wu_ref, wo_ref, lng_ref, lnb_ref,
         o_ref, h_scr, acc_scr) = refs
    j = pl.program_id(1)

    @pl.when(j == 0)
    def _():
        x = x_ref[...]
        if pre_ln:
            x = _layer_norm(x, ing_ref[...], inb_ref[...])
            xin_scr[...] = x
        shift = mod_ref[3 * sub:3 * sub + 1, :]
        scale = mod_ref[3 * sub + 1:3 * sub + 2, :]
        h_scr[...] = (x * (1.0 + scale) + shift).astype(BF16)
        acc_scr[...] = jnp.zeros_like(acc_scr)

    h = h_scr[...]
    gate = jnp.dot(h, wg_ref[...], preferred_element_type=F32)
    up = jnp.dot(h, wu_ref[...], preferred_element_type=F32)
    act = (_silu(gate) * up).astype(BF16)
    acc_scr[...] += jnp.dot(act, wo_ref[...], preferred_element_type=F32)

    @pl.when(j == n_ff - 1)
    def _():
        x = xin_scr[...] if pre_ln else x_ref[...]
        g = mod_ref[3 * sub + 2:3 * sub + 3, :]
        r = ALPHA * x + (0.5 * g) * acc_scr[...]
        o_ref[...] = _layer_norm(r, lng_ref[...], lnb_ref[...])


def _ffn(x2d, mod_l, w_in, w_out, ln_g, ln_b, *, sub, seq, pre=None):
    n_tok, d = x2d.shape
    tm = min(FFN_TM, seq)
    assert seq % tm == 0 and n_tok % seq == 0 and D_FF % FFN_TF == 0
    tiles_per_seq = seq // tm
    n_ff = D_FF // FFN_TF
    row = lambda v: v.reshape(1, d)
    in_specs = [
        pl.BlockSpec((tm, d), lambda i, j: (i, 0)),
        pl.BlockSpec((None, 3 * N_SUB, d), lambda i, j: (i // tiles_per_seq, 0, 0)),
        pl.BlockSpec((d, FFN_TF), lambda i, j: (0, j)),
        pl.BlockSpec((d, FFN_TF), lambda i, j: (0, j + n_ff)),
        pl.BlockSpec((FFN_TF, d), lambda i, j: (j, 0)),
        pl.BlockSpec((1, d), lambda i, j: (0, 0)),
        pl.BlockSpec((1, d), lambda i, j: (0, 0)),
    ]
    args = [x2d, mod_l, w_in, w_in, w_out, row(ln_g), row(ln_b)]
    scratch = [pltpu.VMEM((tm, d), BF16), pltpu.VMEM((tm, d), F32)]
    if pre is not None:
        in_specs += [pl.BlockSpec((1, d), lambda i, j: (0, 0))] * 2
        args += [row(pre[0]), row(pre[1])]
        scratch.append(pltpu.VMEM((tm, d), F32))
    return pl.pallas_call(
        functools.partial(_ffn_kernel, sub=sub, pre_ln=pre is not None, n_ff=n_ff),
        grid=(n_tok // tm, n_ff),
        in_specs=in_specs,
        out_specs=pl.BlockSpec((tm, d), lambda i, j: (i, 0)),
        out_shape=jax.ShapeDtypeStruct((n_tok, d), F32),
        scratch_shapes=scratch,
        compiler_params=_params("arbitrary", "arbitrary"),
    )(*args)


def _proj_kernel(x_ref, mod_ref, w_ref, cw_ref, cb_ref, sbias_ref, alog_ref, scw_ref, fbias_ref,
                 z_ref, xs_ref, bm_ref, cm_ref, q_ref, k_ref, v_ref, ysc_ref, small_ref,
                 xbc_scr, u_scr, carry_scr, *, tm):
    t = pl.program_id(1)
    halo = SUBLANES

    @pl.when(t == 0)
    def _():
        xbc_scr[0:halo, :] = jnp.zeros((halo, SSD_CONV_DIM), F32)
        u_scr[0:halo, :] = jnp.zeros((halo, SCONV_WIDTH), F32)
        carry_scr[...] = jnp.zeros_like(carry_scr)

    @pl.when(t > 0)
    def _():
        xbc_scr[0:halo, :] = xbc_scr[tm:tm + halo, :]
        u_scr[0:halo, :] = u_scr[tm:tm + halo, :]

    shift = mod_ref[3:4, :]
    scale = mod_ref[4:5, :]
    h = (x_ref[0] * (1.0 + scale) + shift).astype(BF16)

    def seg(start, width):
        return jnp.dot(h, w_ref[:, start:start + width], preferred_element_type=F32)

    z_ref[0] = seg(COL_Z, SSD_WIDTH)

    xbc_scr[halo:halo + tm, :] = seg(COL_XBC, SSD_CONV_DIM)
    conv = cb_ref[...] + cw_ref[SSD_CONV - 1:SSD_CONV, :] * xbc_scr[halo:halo + tm, :]
    for k in range(SSD_CONV - 1):
        back = SSD_CONV - 1 - k
        conv = conv + cw_ref[k:k + 1, :] * xbc_scr[halo - back:halo - back + tm, :]
    xbc = _silu(conv)
    xs_ref[0] = xbc[:, :SSD_WIDTH]
    bm_ref[0] = xbc[:, SSD_WIDTH:SSD_WIDTH + SSD_BC].astype(BF16)
    cm_ref[0] = xbc[:, SSD_WIDTH + SSD_BC:].astype(BF16)

    raw_f = seg(COL_F, FOX_WIDTH) + fbias_ref[...]
    pos = lax.broadcasted_iota(jnp.int32, (tm, FOX_WIDTH), 1) % FOX_HEAD_DIM
    log_f = jnp.where(pos < 2 * FOX_PIECES, -_softplus(-raw_f), 0.0)
    row_i = lax.broadcasted_iota(jnp.int32, (tm, tm), 0)
    col_i = lax.broadcasted_iota(jnp.int32, (tm, tm), 1)
    tri = jnp.where(row_i >= col_i, 1.0, 0.0).astype(F32)
    cum_f = jnp.dot(tri, log_f, preferred_element_type=F32, precision=HIGHEST) + carry_scr[0:1, :]
    carry_scr[...] = jnp.broadcast_to(cum_f[tm - 1:tm, :], carry_scr.shape)
    hi = cum_f.astype(BF16).astype(F32)
    rest = cum_f - hi
    mid = rest.astype(BF16).astype(F32)
    lo = (rest - mid).astype(BF16).astype(F32)
    piece = jnp.where(pos % FOX_PIECES == 0, hi, jnp.where(pos % FOX_PIECES == 1, mid, lo))
    in_a = pos < FOX_PIECES
    in_b = (pos >= FOX_PIECES) & (pos < 2 * FOX_PIECES)
    f_q = jnp.where(in_a, piece, jnp.where(in_b, 1.0, 0.0))
    f_k = jnp.where(in_a, 1.0, jnp.where(in_b, -piece, 0.0))

    qkv = seg(COL_QKV, 3 * FOX_WIDTH)
    q = qkv[:, :FOX_WIDTH] * (FOX_HEAD_DIM ** -0.5)
    k = qkv[:, FOX_WIDTH:2 * FOX_WIDTH]
    v = qkv[:, 2 * FOX_WIDTH:]
    lane = lax.broadcasted_iota(jnp.int32, (tm, LANES), 1)
    for hd in range(FOX_HEADS):
        pair, hh = divmod(hd, LANES // FOX_HEAD_DIM)
        ps = slice(pair * LANES, (pair + 1) * LANES)
        gs = slice(hd * LANES, (hd + 1) * LANES)
        mine = (lane // FOX_HEAD_DIM) == hh
        ones_col = jnp.where(lane == (1 - hh) * FOX_HEAD_DIM, 1.0, 0.0)
        q_ref[0, :, gs] = jnp.where(mine, q[:, ps], f_q[:, ps]).astype(BF16)
        k_ref[0, :, gs] = jnp.where(mine, k[:, ps], f_k[:, ps]).astype(BF16)
        v_ref[0, :, gs] = jnp.where(mine, v[:, ps], ones_col).astype(BF16)

    sc = seg(COL_SC, 3 * SCONV_WIDTH)
    u_scr[halo:halo + tm, :] = sc[:, SCONV_WIDTH:2 * SCONV_WIDTH] * sc[:, 2 * SCONV_WIDTH:]
    cu = scw_ref[SCONV_K - 1:SCONV_K, :] * u_scr[halo:halo + tm, :]
    for k in range(SCONV_K - 1):
        back = SCONV_K - 1 - k
        cu = cu + scw_ref[k:k + 1, :] * u_scr[halo - back:halo - back + tm, :]
    ysc_ref[0] = (sc[:, :SCONV_WIDTH] * cu).astype(BF16)

    dt = _softplus(seg(COL_SMALL, LANES) + sbias_ref[...])
    a_neg = -jnp.exp(alog_ref[...])
    small_ref[0] = jnp.where(lane < LANE_A, dt, dt * a_neg)


def _proj(x, mod_l, w_cat, conv_w, conv_b, small_bias, alog_vec, sconv_w, f_bias):
    b, seq, d = x.shape
    tm = min(PROJ_TM, seq)
    assert seq % tm == 0
    nt = seq // tm
    tok = lambda width: pl.BlockSpec((1, tm, width), lambda bi, t: (bi, t, 0))
    const2 = lambda shape: pl.BlockSpec(shape, lambda bi, t: (0, 0))
    out_shapes = (
        jax.ShapeDtypeStruct((b, seq, SSD_WIDTH), F32),
        jax.ShapeDtypeStruct((b, seq, SSD_WIDTH), F32),
        jax.ShapeDtypeStruct((b, seq, SSD_BC), BF16),
        jax.ShapeDtypeStruct((b, seq, SSD_BC), BF16),
        jax.ShapeDtypeStruct((b, seq, FOX_GROUP), BF16),
        jax.ShapeDtypeStruct((b, seq, FOX_GROUP), BF16),
        jax.ShapeDtypeStruct((b, seq, FOX_GROUP), BF16),
        jax.ShapeDtypeStruct((b, seq, SCONV_WIDTH), BF16),
        jax.ShapeDtypeStruct((b, seq, LANES), F32),
    )
    out_specs = (
        tok(SSD_WIDTH), tok(SSD_WIDTH), tok(SSD_BC), tok(SSD_BC),
        tok(FOX_GROUP), tok(FOX_GROUP), tok(FOX_GROUP), tok(SCONV_WIDTH), tok(LANES),
    )
    return pl.pallas_call(
        functools.partial(_proj_kernel, tm=tm),
        grid=(b, nt),
        in_specs=[
            tok(d),
            pl.BlockSpec((None, 3 * N_SUB, d), lambda bi, t: (bi, 0, 0)),
            const2((d, D_PROJ)),
            const2((SSD_CONV, SSD_CONV_DIM)),
            const2((1, SSD_CONV_DIM)),
            const2((1, LANES)),
            const2((1, LANES)),
            const2((SCONV_K, SCONV_WIDTH)),
            const2((1, FOX_WIDTH)),
        ],
        out_specs=out_specs,
        out_shape=out_shapes,
        scratch_shapes=[
            pltpu.VMEM((tm + SUBLANES, SSD_CONV_DIM), F32),
            pltpu.VMEM((tm + SUBLANES, SCONV_WIDTH), F32),
            pltpu.VMEM((SUBLANES, FOX_WIDTH), F32),
        ],
        compiler_params=_params("arbitrary", "arbitrary"),
    )(x, mod_l, w_cat, conv_w, conv_b, small_bias, alog_vec, sconv_w, f_bias)


def _ssd_kernel(xs_ref, bm_ref, cm_ref, z_ref, small_ref, dskip_ref, ng_ref, o_ref, state_scr, *, t):
    c = pl.program_id(1)

    @pl.when(c == 0)
    def _():
        state_scr[...] = jnp.zeros_like(state_scr)

    small = small_ref[0]
    lane = lax.broadcasted_iota(jnp.int32, (t, LANES), 1)
    a_only = jnp.where((lane >= LANE_A) & (lane < LANE_A + SSD_HEADS), small, 0.0)
    row_i = lax.broadcasted_iota(jnp.int32, (t, t), 0)
    col_i = lax.broadcasted_iota(jnp.int32, (t, t), 1)
    lower = row_i >= col_i
    tri = jnp.where(lower, 1.0, 0.0).astype(F32)
    cs = jnp.dot(tri, a_only, preferred_element_type=F32, precision=HIGHEST)
    cs_t = cs.T

    src = lax.broadcasted_iota(jnp.int32, (LANES, SSD_WIDTH), 0)
    dst_head = lax.broadcasted_iota(jnp.int32, (LANES, SSD_WIDTH), 1) // SSD_HEAD_DIM
    spread_dt = jnp.where(src == dst_head + LANE_DT, 1.0, 0.0).astype(F32)
    spread_a = jnp.where(src == dst_head + LANE_A, 1.0, 0.0).astype(F32)
    dt_x = jnp.dot(small, spread_dt, preferred_element_type=F32, precision=HIGHEST)
    cs_x = jnp.dot(cs, spread_a, preferred_element_type=F32, precision=HIGHEST)
    tot_x = cs_x[t - 1:t, :]

    xs = xs_ref[0]
    xdt = xs * dt_x
    decay_in = jnp.exp(cs_x)
    decay_out = jnp.exp(tot_x - cs_x)
    decay_chunk = jnp.exp(tot_x)
    head_of_lane = lax.broadcasted_iota(jnp.int32, (t, GROUP_WIDTH), 1) // SSD_HEAD_DIM

    ys = []
    for g in range(SSD_GROUPS):
        gs = slice(g * GROUP_WIDTH, (g + 1) * GROUP_WIDTH)
        ns = slice(g * SSD_STATE, (g + 1) * SSD_STATE)
        bg = bm_ref[0, :, ns]
        cg = cm_ref[0, :, ns]
        cb = lax.dot_general(cg, bg, (((1,), (1,)), ((), ())), preferred_element_type=F32)
        xg = xdt[:, gs]
        m_parts, x_parts = [], []
        for e in range(SSD_HPG):
            hd = g * SSD_HPG + e
            col = cs[:, LANE_A + hd:LANE_A + hd + 1]
            row = cs_t[LANE_A + hd:LANE_A + hd + 1, :]
            l_mat = jnp.exp(jnp.where(lower, col - row, -jnp.inf))
            m_parts.append((cb * l_mat).astype(BF16))
            x_parts.append(jnp.where(head_of_lane == e, xg, 0.0).astype(BF16))
        m_cat = jnp.concatenate(m_parts, axis=1)
        x_blk = jnp.concatenate(x_parts, axis=0)
        y_diag = jnp.dot(m_cat, x_blk, preferred_element_type=F32)

        state = state_scr[g]
        y_off = jnp.dot(cg, state.astype(BF16), preferred_element_type=F32) * decay_in[:, gs]
        xd = (xg * decay_out[:, gs]).astype(BF16)
        upd = lax.dot_general(bg, xd, (((0,), (0,)), ((), ())), preferred_element_type=F32)
        state_scr[g] = state * decay_chunk[:, gs] + upd
        ys.append(y_diag + y_off + dskip_ref[:, gs] * xs[:, gs])

    z = z_ref[0]
    outs = []
    for g in range(SSD_GROUPS):
        gs = slice(g * GROUP_WIDTH, (g + 1) * GROUP_WIDTH)
        yg = ys[g] * _silu(z[:, gs])
        yg = yg * lax.rsqrt(jnp.mean(yg * yg, axis=-1, keepdims=True) + RMS_EPS)
        outs.append((yg * ng_ref[:, gs]).astype(BF16))
    o_ref[0] = jnp.concatenate(outs, axis=1)


def _ssd(xs, bm, cm, z, small, d_skip_x, norm_g):
    b, seq, _ = xs.shape
    t = min(SSD_T, seq)
    assert seq % t == 0
    tok = lambda width: pl.BlockSpec((1, t, width), lambda bi, c: (bi, c, 0))
    const2 = lambda shape: pl.BlockSpec(shape, lambda bi, c: (0, 0))
    return pl.pallas_call(
        functools.partial(_ssd_kernel, t=t),
        grid=(b, seq // t),
        in_specs=[tok(SSD_WIDTH), tok(SSD_BC), tok(SSD_BC), tok(SSD_WIDTH), tok(LANES),
                  const2((1, SSD_WIDTH)), const2((1, SSD_WIDTH))],
        out_specs=tok(SSD_WIDTH),
        out_shape=jax.ShapeDtypeStruct((b, seq, SSD_WIDTH), BF16),
        scratch_shapes=[pltpu.VMEM((SSD_GROUPS, SSD_STATE, GROUP_WIDTH), F32)],
        compiler_params=_params("arbitrary", "arbitrary"),
    )(xs, bm, cm, z, small, d_skip_x, norm_g)


def _fox_kernel(q_ref, k_ref, v_ref, o_ref, *, t):
    qi = pl.program_id(1)
    lane_head = lax.broadcasted_iota(jnp.int32, (t, LANES), 1) // FOX_HEAD_DIM
    row_i = lax.broadcasted_iota(jnp.int32, (t, t), 0)
    col_i = lax.broadcasted_iota(jnp.int32, (t, t), 1)
    causal = row_i >= col_i
    heads_per_vreg = LANES // FOX_HEAD_DIM

    def block(j, carry, masked):
        ks = pl.ds(pl.multiple_of(j * t, t), t)
        new = []
        for hd in range(FOX_HEADS):
            m_prev, acc = carry[hd]
            gs = slice(hd * LANES, (hd + 1) * LANES)
            s = lax.dot_general(q_ref[0, :, gs], k_ref[0, ks, gs], (((1,), (1,)), ((), ())),
                                preferred_element_type=F32)
            if masked:
                s = jnp.where(causal, s, -jnp.inf)
            m_new = jnp.maximum(m_prev, jnp.max(s, axis=-1, keepdims=True))
            p = jnp.exp(s - m_new).astype(BF16)
            acc = jnp.exp(m_prev - m_new) * acc + jnp.dot(p, v_ref[0, ks, gs],
                                                           preferred_element_type=F32)
            new.append((m_new, acc))
        return tuple(new)

    init = tuple((jnp.full((t, 1), -jnp.inf, F32), jnp.zeros((t, LANES), F32))
                 for _ in range(FOX_HEADS))
    carry = lax.fori_loop(0, qi, lambda j, cr: block(j, cr, False), init)
    final = block(qi, carry, True)
    for pair in range(FOX_HEADS // heads_per_vreg):
        out = jnp.zeros((t, LANES), F32)
        for hh in range(heads_per_vreg):
            _, acc = final[pair * heads_per_vreg + hh]
            ones_lane = (1 - hh) * FOX_HEAD_DIM
            row_sum = acc[:, ones_lane:ones_lane + 1]
            out = out + jnp.where(lane_head == hh, acc, 0.0) / row_sum
        o_ref[0, :, pair * LANES:(pair + 1) * LANES] = out.astype(BF16)


def _fox(qa, ka, va):
    b, seq, _ = qa.shape
    t = min(FOX_T, seq)
    assert seq % t == 0
    return pl.pallas_call(
        functools.partial(_fox_kernel, t=t),
        grid=(b, seq // t),
        in_specs=[
            pl.BlockSpec((1, t, FOX_GROUP), lambda bi, i: (bi, i, 0)),
            pl.BlockSpec((1, seq, FOX_GROUP), lambda bi, i: (bi, 0, 0)),
            pl.BlockSpec((1, seq, FOX_GROUP), lambda bi, i: (bi, 0, 0)),
        ],
        out_specs=pl.BlockSpec((1, t, FOX_WIDTH), lambda bi, i: (bi, i, 0)),
        out_shape=jax.ShapeDtypeStruct((b, seq, FOX_WIDTH), BF16),
        compiler_params=_params("arbitrary", "arbitrary"),
    )(qa, ka, va)


def _outproj_kernel(x_ref, mod_ref, yssd_ref, yfox_ref, ysc_ref, w_ref, lng_ref, lnb_ref, o_ref):
    y = jnp.dot(yssd_ref[...], w_ref[0:SSD_WIDTH, :], preferred_element_type=F32)
    y = y + jnp.dot(yfox_ref[...], w_ref[SSD_WIDTH:SSD_WIDTH + FOX_WIDTH, :], preferred_element_type=F32)
    y = y + jnp.dot(ysc_ref[...], w_ref[SSD_WIDTH + FOX_WIDTH:, :], preferred_element_type=F32)
    g = mod_ref[5:6, :]
    o_ref[...] = _layer_norm(ALPHA * x_ref[...] + g * y, lng_ref[...], lnb_ref[...])


def _outproj(x2d, mod_l, y_ssd, y_fox, y_sc, w_out, ln_g, ln_b, *, seq):
    n_tok, d = x2d.shape
    tm = min(OUT_TM, seq)
    assert seq % tm == 0
    tiles_per_seq = seq // tm
    tok = lambda width: pl.BlockSpec((tm, width), lambda i: (i, 0))
    return pl.pallas_call(
        _outproj_kernel,
        grid=(n_tok // tm,),
        in_specs=[
            tok(d),
            pl.BlockSpec((None, 3 * N_SUB, d), lambda i: (i // tiles_per_seq, 0, 0)),
            tok(SSD_WIDTH), tok(FOX_WIDTH), tok(SCONV_WIDTH),
            pl.BlockSpec((d, d), lambda i: (0, 0)),
            pl.BlockSpec((1, d), lambda i: (0, 0)),
            pl.BlockSpec((1, d), lambda i: (0, 0)),
        ],
        out_specs=tok(d),
        out_shape=jax.ShapeDtypeStruct((n_tok, d), F32),
        compiler_params=_params("arbitrary"),
    )(x2d, mod_l, y_ssd, y_fox, y_sc, w_out, ln_g.reshape(1, d), ln_b.reshape(1, d))


def _spread_forget(f):
    lead = f.shape[:-1]
    heads_per_vreg = LANES // FOX_HEAD_DIM
    blocks = []
    for pair in range(FOX_HEADS // heads_per_vreg):
        for half in range(heads_per_vreg):
            hd = pair * heads_per_vreg + (heads_per_vreg - 1 - half)
            blocks.append(jnp.broadcast_to(f[..., hd:hd + 1], lead + (2 * FOX_PIECES,)))
            blocks.append(jnp.zeros(lead + (FOX_HEAD_DIM - 2 * FOX_PIECES,), f.dtype))
    return jnp.concatenate(blocks, axis=-1)


def _pack_mix_w_in(w):
    ssd_in = SSD_WIDTH + SSD_CONV_DIM + SSD_HEADS
    fox_in = 3 * FOX_WIDTH + FOX_HEADS
    dt_cols = w[:, SSD_WIDTH + SSD_CONV_DIM:ssd_in]
    f_cols = w[:, ssd_in + 3 * FOX_WIDTH:ssd_in + fox_in]
    pad = jnp.zeros((w.shape[0], LANES - 2 * SSD_HEADS), w.dtype)
    return jnp.concatenate([
        w[:, :SSD_WIDTH + SSD_CONV_DIM],
        w[:, ssd_in:ssd_in + 3 * FOX_WIDTH],
        w[:, ssd_in + fox_in:],
        dt_cols, dt_cols, pad,
        _spread_forget(f_cols),
    ], axis=1).astype(BF16)


def _lane_vec(pieces):
    row = jnp.zeros((LANES,), F32)
    for off, vec in pieces:
        row = row.at[off:off + vec.shape[0]].set(vec.astype(F32))
    return row.reshape(1, LANES)


def kernel(x, c, ln_in_g, ln_in_b, ada_w, ada_b, ffn1_w_in, ffn1_w_out, mix_w_in, mix_w_out, ssd_conv_w, ssd_conv_b, ssd_dt_bias, ssd_a_log, ssd_d, ssd_norm_g, fox_f_bias, sconv_w, ffn2_w_in, ffn2_w_out, ln_g, ln_b):
    b, seq, d = x.shape
    depth = ada_w.shape[0]
    rows = -(-b // SUBLANES) * SUBLANES
    c_pad = jnp.pad(c, ((0, rows - b), (0, 0)))
    mod = _ada(c_pad, ada_w, ada_b).reshape(depth, rows, 3 * N_SUB, d)

    x2d = x.reshape(b * seq, d)
    for l in range(depth):
        mod_l = mod[l]
        x2d = _ffn(x2d, mod_l, ffn1_w_in[l].astype(BF16), ffn1_w_out[l].astype(BF16),
                   ln_g[l, 0], ln_b[l, 0], sub=0, seq=seq,
                   pre=(ln_in_g, ln_in_b) if l == 0 else None)

        small_bias = _lane_vec([(LANE_DT, ssd_dt_bias[l]), (LANE_A, ssd_dt_bias[l])])
        alog_vec = _lane_vec([(LANE_A, ssd_a_log[l])])
        z, xs, bm, cm, qa, ka, va, y_sc, small = _proj(
            x2d.reshape(b, seq, d), mod_l, _pack_mix_w_in(mix_w_in[l]),
            ssd_conv_w[l], ssd_conv_b[l].reshape(1, SSD_CONV_DIM), small_bias, alog_vec, sconv_w[l],
            _spread_forget(fox_f_bias[l]).reshape(1, FOX_WIDTH))
        d_skip_x = jnp.repeat(ssd_d[l], SSD_HEAD_DIM).reshape(1, SSD_WIDTH)
        y_ssd = _ssd(xs, bm, cm, z, small, d_skip_x, ssd_norm_g[l].reshape(1, SSD_WIDTH))
        y_fox = _fox(qa, ka, va)
        x2d = _outproj(x2d, mod_l, y_ssd.reshape(b * seq, SSD_WIDTH), y_fox.reshape(b * seq, FOX_WIDTH),
                       y_sc.reshape(b * seq, SCONV_WIDTH), mix_w_out[l].astype(BF16),
                       ln_g[l, 1], ln_b[l, 1], seq=seq)

        x2d = _ffn(x2d, mod_l, ffn2_w_in[l].astype(BF16), ffn2_w_out[l].astype(BF16),
                   ln_g[l, 2], ln_b[l, 2], sub=2, seq=seq)
    return x2d.reshape(b, seq, d)
```

```python
import functools
import math

import jax
import jax.numpy as jnp
from jax import lax
from jax.experimental import pallas as pl
from jax.experimental.pallas import tpu as pltpu

F32 = jnp.float32
BF16 = jnp.bfloat16
HIGHEST = lax.Precision.HIGHEST

D_MODEL = 1024
DEPTH = 2
N_SUB = 3
D_FF = 2816
SSD_WIDTH = 512
SSD_HEAD_DIM = 64
SSD_HEADS = 8
SSD_GROUPS = 2
SSD_HPG = SSD_HEADS // SSD_GROUPS
SSD_STATE = 128
SSD_CONV = 4
SSD_BC = SSD_GROUPS * SSD_STATE
SSD_CONV_DIM = SSD_WIDTH + 2 * SSD_BC
GROUP_WIDTH = SSD_WIDTH // SSD_GROUPS
FOX_WIDTH = 256
FOX_HEAD_DIM = 64
FOX_HEADS = 4
SCONV_WIDTH = 256
SCONV_K = 3
ALPHA = (2 * DEPTH) ** 0.25
LN_EPS = 1e-5
RMS_EPS = 1e-5
LOG2_E = math.log2(math.e)

LANES = 128
SUBLANES = 8
VMEM_LIMIT = 56 * 1024 * 1024

COL_Z = 0
COL_XBC = COL_Z + SSD_WIDTH
COL_QKV = COL_XBC + SSD_CONV_DIM
COL_SC = COL_QKV + 3 * FOX_WIDTH
COL_SMALL = COL_SC + 3 * SCONV_WIDTH
COL_F = COL_SMALL + LANES
D_PROJ = COL_F + FOX_WIDTH
LANE_DT = 0
LANE_A = 8
FOX_PIECES = 3
FOX_GROUP = FOX_HEADS * LANES

FFN_TM = 1024
FFN_TF = 256
PROJ_TM = 512
SSD_T = 256
FOX_T = 512
OUT_TM = 1024
ADA_TN = 1024


def _layer_norm(x, g, b):
    mu = jnp.mean(x, axis=-1, keepdims=True)
    xc = x - mu
    var = jnp.mean(xc * xc, axis=-1, keepdims=True)
    return xc * lax.rsqrt(var + LN_EPS) * g + b


def _silu(x):
    return x / (1.0 + jnp.exp(-x))


def _softplus(x):
    return jnp.maximum(x, 0.0) + jnp.log1p(jnp.exp(-jnp.abs(x)))


def _params(*semantics):
    return pltpu.CompilerParams(dimension_semantics=semantics, vmem_limit_bytes=VMEM_LIMIT)


def _ada_kernel(c_ref, w_ref, b_ref, o_ref):
    o_ref[...] = jnp.dot(_silu(c_ref[...]), w_ref[...], preferred_element_type=F32,
                         precision=HIGHEST) + b_ref[...]


def _ada(c_pad, ada_w, ada_b):
    depth, d, n = ada_w.shape
    rows = c_pad.shape[0]
    return pl.pallas_call(
        _ada_kernel,
        grid=(depth, n // ADA_TN),
        in_specs=[
            pl.BlockSpec((rows, d), lambda l, j: (0, 0)),
            pl.BlockSpec((None, d, ADA_TN), lambda l, j: (l, 0, j)),
            pl.BlockSpec((None, 1, ADA_TN), lambda l, j: (l, 0, j)),
        ],
        out_specs=pl.BlockSpec((None, rows, ADA_TN), lambda l, j: (l, 0, j)),
        out_shape=jax.ShapeDtypeStruct((depth, rows, n), F32),
        compiler_params=_params("arbitrary", "arbitrary"),
    )(c_pad, ada_w, ada_b.reshape(depth, 1, n))


def _ffn_kernel(*refs, sub, pre_ln):
    if pre_ln:
        x_ref, mod_ref, wi_ref, wo_ref, lng_ref, lnb_ref, ing_ref, inb_ref, o_ref, act_scr = refs
    else:
        x_ref, mod_ref, wi_ref, wo_ref, lng_ref, lnb_ref, o_ref, act_scr = refs
    x = x_ref[...]
    if pre_ln:
        x = _layer_norm(x, ing_ref[...], inb_ref[...])
    shift = mod_ref[3 * sub:3 * sub + 1, :]
    scale = mod_ref[3 * sub + 1:3 * sub + 2, :]
    h = (x * (1.0 + scale) + shift).astype(BF16)
    for c in range(D_FF // FFN_TF):
        cols = slice(c * FFN_TF, (c + 1) * FFN_TF)
        gate = jnp.dot(h, wi_ref[:, cols], preferred_element_type=F32)
        up = jnp.dot(h, wi_ref[:, D_FF + c * FFN_TF:D_FF + (c + 1) * FFN_TF], preferred_element_type=F32)
        act_scr[:, cols] = (_silu(gate) * up).astype(BF16)
    y = jnp.dot(act_scr[...], wo_ref[...], preferred_element_type=F32)
    g = mod_ref[3 * sub + 2:3 * sub + 3, :]
    o_ref[...] = _layer_norm(ALPHA * x + (0.5 * g) * y, lng_ref[...], lnb_ref[...])


def _ffn(x2d, mod_l, w_in, w_out, ln_g, ln_b, *, layer, sub, seq, pre=None):
    n_tok, d = x2d.shape
    tm = min(FFN_TM, seq)
    assert seq % tm == 0 and n_tok % seq == 0 and D_FF % FFN_TF == 0
    tiles_per_seq = seq // tm
    row = lambda v: v.reshape(1, d)
    resident = lambda shape: pl.BlockSpec(shape, lambda i: (0, 0), pipeline_mode=pl.Buffered(1))
    weight = lambda shape: pl.BlockSpec((None,) + shape, lambda i: (layer, 0, 0),
                                        pipeline_mode=pl.Buffered(1))
    in_specs = [
        pl.BlockSpec((tm, d), lambda i: (i, 0)),
        pl.BlockSpec((None, 3 * N_SUB, d), lambda i: (i // tiles_per_seq, 0, 0)),
        weight((d, 2 * D_FF)),
        weight((D_FF, d)),
        resident((1, d)),
        resident((1, d)),
    ]
    args = [x2d, mod_l, w_in, w_out, row(ln_g), row(ln_b)]
    if pre is not None:
        in_specs += [resident((1, d))] * 2
        args += [row(pre[0]), row(pre[1])]
    return pl.pallas_call(
        functools.partial(_ffn_kernel, sub=sub, pre_ln=pre is not None),
        grid=(n_tok // tm,),
        in_specs=in_specs,
        out_specs=pl.BlockSpec((tm, d), lambda i: (i, 0)),
        out_shape=jax.ShapeDtypeStruct((n_tok, d), F32),
        scratch_shapes=[pltpu.VMEM((tm, D_FF), BF16)],
        compiler_params=_params("arbitrary"),
    )(*args)


def _proj_kernel(x_ref, mod_ref, w_ref, cw_ref, cb_ref, sbias_ref, alog_ref, scw_ref, fbias_ref,
                 z_ref, xs_ref, bm_ref, cm_ref, q_ref, k_ref, v_ref, ysc_ref, small_ref,
                 xbc_scr, u_scr, carry_scr, *, tm):
    t = pl.program_id(1)
    halo = SUBLANES

    @pl.when(t == 0)
    def _():
        xbc_scr[0:halo, :] = jnp.zeros((halo, SSD_CONV_DIM), F32)
        u_scr[0:halo, :] = jnp.zeros((halo, SCONV_WIDTH), F32)
        carry_scr[...] = jnp.zeros_like(carry_scr)

    @pl.when(t > 0)
    def _():
        xbc_scr[0:halo, :] = xbc_scr[tm:tm + halo, :]
        u_scr[0:halo, :] = u_scr[tm:tm + halo, :]

    shift = mod_ref[3:4, :]
    scale = mod_ref[4:5, :]
    h = (x_ref[0] * (1.0 + scale) + shift).astype(BF16)

    def seg(start, width):
        return jnp.dot(h, w_ref[:, start:start + width], preferred_element_type=F32)

    z_ref[0] = seg(COL_Z, SSD_WIDTH)

    xbc_scr[halo:halo + tm, :] = seg(COL_XBC, SSD_CONV_DIM)
    conv = cb_ref[...] + cw_ref[SSD_CONV - 1:SSD_CONV, :] * xbc_scr[halo:halo + tm, :]
    for k in range(SSD_CONV - 1):
        back = SSD_CONV - 1 - k
        conv = conv + cw_ref[k:k + 1, :] * xbc_scr[halo - back:halo - back + tm, :]
    xbc = _silu(conv)
    xs_ref[0] = xbc[:, :SSD_WIDTH]
    bm_ref[0] = xbc[:, SSD_WIDTH:SSD_WIDTH + SSD_BC].astype(BF16)
    cm_ref[0] = xbc[:, SSD_WIDTH + SSD_BC:].astype(BF16)

    raw_f = seg(COL_F, FOX_WIDTH) + fbias_ref[...]
    pos = lax.broadcasted_iota(jnp.int32, (tm, FOX_WIDTH), 1) % FOX_HEAD_DIM
    log_f = jnp.where(pos < 2 * FOX_PIECES, -_softplus(-raw_f), 0.0)
    row_i = lax.broadcasted_iota(jnp.int32, (tm, tm), 0)
    col_i = lax.broadcasted_iota(jnp.int32, (tm, tm), 1)
    tri = jnp.where(row_i >= col_i, 1.0, 0.0).astype(F32)
    cum_f = jnp.dot(tri, log_f, preferred_element_type=F32, precision=HIGHEST) + carry_scr[0:1, :]
    carry_scr[...] = jnp.broadcast_to(cum_f[tm - 1:tm, :], carry_scr.shape)
    cum_f2 = cum_f * LOG2_E
    hi = cum_f2.astype(BF16).astype(F32)
    rest = cum_f2 - hi
    mid = rest.astype(BF16).astype(F32)
    lo = (rest - mid).astype(BF16).astype(F32)
    piece = jnp.where(pos % FOX_PIECES == 0, hi, jnp.where(pos % FOX_PIECES == 1, mid, lo))
    in_a = pos < FOX_PIECES
    in_b = (pos >= FOX_PIECES) & (pos < 2 * FOX_PIECES)
    f_q = jnp.where(in_a, piece, jnp.where(in_b, 1.0, 0.0))
    f_k = jnp.where(in_a, 1.0, jnp.where(in_b, -piece, 0.0))

    qkv = seg(COL_QKV, 3 * FOX_WIDTH)
    q = qkv[:, :FOX_WIDTH] * (FOX_HEAD_DIM ** -0.5 * LOG2_E)
    k = qkv[:, FOX_WIDTH:2 * FOX_WIDTH]
    v = qkv[:, 2 * FOX_WIDTH:]
    lane = lax.broadcasted_iota(jnp.int32, (tm, LANES), 1)
    for hd in range(FOX_HEADS):
        pair, hh = divmod(hd, LANES // FOX_HEAD_DIM)
        ps = slice(pair * LANES, (pair + 1) * LANES)
        gs = slice(hd * LANES, (hd + 1) * LANES)
        mine = (lane // FOX_HEAD_DIM) == hh
        ones_col = jnp.where(lane == (1 - hh) * FOX_HEAD_DIM, 1.0, 0.0)
        q_ref[0, :, gs] = jnp.where(mine, q[:, ps], f_q[:, ps]).astype(BF16)
        k_ref[0, :, gs] = jnp.where(mine, k[:, ps], f_k[:, ps]).astype(BF16)
        v_ref[0, :, gs] = jnp.where(mine, v[:, ps], ones_col).astype(BF16)

    sc = seg(COL_SC, 3 * SCONV_WIDTH)
    u_scr[halo:halo + tm, :] = sc[:, SCONV_WIDTH:2 * SCONV_WIDTH] * sc[:, 2 * SCONV_WIDTH:]
    cu = scw_ref[SCONV_K - 1:SCONV_K, :] * u_scr[halo:halo + tm, :]
    for k in range(SCONV_K - 1):
        back = SCONV_K - 1 - k
        cu = cu + scw_ref[k:k + 1, :] * u_scr[halo - back:halo - back + tm, :]
    ysc_ref[0] = (sc[:, :SCONV_WIDTH] * cu).astype(BF16)

    dt = _softplus(seg(COL_SMALL, LANES) + sbias_ref[...])
    a_neg = -jnp.exp(alog_ref[...])
    small_ref[0] = jnp.where(lane < LANE_A, dt, dt * a_neg)


def _proj(x, mod_l, w_cat, conv_w, conv_b, small_bias, alog_vec, sconv_w, f_bias, *, layer):
    b, seq, d = x.shape
    tm = min(PROJ_TM, seq)
    assert seq % tm == 0
    nt = seq // tm
    tok = lambda width: pl.BlockSpec((1, tm, width), lambda bi, t: (bi, t, 0))
    const2 = lambda shape: pl.BlockSpec(shape, lambda bi, t: (0, 0))
    out_shapes = (
        jax.ShapeDtypeStruct((b, seq, SSD_WIDTH), F32),
        jax.ShapeDtypeStruct((b, seq, SSD_WIDTH), F32),
        jax.ShapeDtypeStruct((b, seq, SSD_BC), BF16),
        jax.ShapeDtypeStruct((b, seq, SSD_BC), BF16),
        jax.ShapeDtypeStruct((b, seq, FOX_GROUP), BF16),
        jax.ShapeDtypeStruct((b, seq, FOX_GROUP), BF16),
        jax.ShapeDtypeStruct((b, seq, FOX_GROUP), BF16),
        jax.ShapeDtypeStruct((b, seq, SCONV_WIDTH), BF16),
        jax.ShapeDtypeStruct((b, seq, LANES), F32),
    )
    out_specs = (
        tok(SSD_WIDTH), tok(SSD_WIDTH), tok(SSD_BC), tok(SSD_BC),
        tok(FOX_GROUP), tok(FOX_GROUP), tok(FOX_GROUP), tok(SCONV_WIDTH), tok(LANES),
    )
    return pl.pallas_call(
        functools.partial(_proj_kernel, tm=tm),
        grid=(b, nt),
        in_specs=[
            tok(d),
            pl.BlockSpec((None, 3 * N_SUB, d), lambda bi, t: (bi, 0, 0)),
            pl.BlockSpec((None, d, D_PROJ), lambda bi, t: (layer, 0, 0)),
            const2((SSD_CONV, SSD_CONV_DIM)),
            const2((1, SSD_CONV_DIM)),
            const2((1, LANES)),
            const2((1, LANES)),
            const2((SCONV_K, SCONV_WIDTH)),
            const2((1, FOX_WIDTH)),
        ],
        out_specs=out_specs,
        out_shape=out_shapes,
        scratch_shapes=[
            pltpu.VMEM((tm + SUBLANES, SSD_CONV_DIM), F32),
            pltpu.VMEM((tm + SUBLANES, SCONV_WIDTH), F32),
            pltpu.VMEM((SUBLANES, FOX_WIDTH), F32),
        ],
        compiler_params=_params("arbitrary", "arbitrary"),
    )(x, mod_l, w_cat, conv_w, conv_b, small_bias, alog_vec, sconv_w, f_bias)


def _ssd_kernel(xs_ref, bm_ref, cm_ref, z_ref, small_ref, dskip_ref, ng_ref, o_ref, state_scr, *, t):
    c = pl.program_id(1)

    @pl.when(c == 0)
    def _():
        state_scr[...] = jnp.zeros_like(state_scr)

    small = small_ref[0]
    lane = lax.broadcasted_iota(jnp.int32, (t, LANES), 1)
    a_only = jnp.where((lane >= LANE_A) & (lane < LANE_A + SSD_HEADS), small, 0.0)
    row_i = lax.broadcasted_iota(jnp.int32, (t, t), 0)
    col_i = lax.broadcasted_iota(jnp.int32, (t, t), 1)
    lower = row_i >= col_i
    tri = jnp.where(lower, 1.0, 0.0).astype(F32)
    cs = jnp.dot(tri, a_only, preferred_element_type=F32, precision=HIGHEST)
    cs_t = cs.T

    src = lax.broadcasted_iota(jnp.int32, (LANES, SSD_WIDTH), 0)
    dst_head = lax.broadcasted_iota(jnp.int32, (LANES, SSD_WIDTH), 1) // SSD_HEAD_DIM
    spread_dt = jnp.where(src == dst_head + LANE_DT, 1.0, 0.0).astype(F32)
    spread_a = jnp.where(src == dst_head + LANE_A, 1.0, 0.0).astype(F32)
    dt_x = jnp.dot(small, spread_dt, preferred_element_type=F32, precision=HIGHEST)
    cs_x = jnp.dot(cs, spread_a, preferred_element_type=F32, precision=HIGHEST)
    tot_x = cs_x[t - 1:t, :]

    xs = xs_ref[0]
    xdt = xs * dt_x
    decay_in = jnp.exp(cs_x)
    decay_out = jnp.exp(tot_x - cs_x)
    decay_chunk = jnp.exp(tot_x)
    head_of_lane = lax.broadcasted_iota(jnp.int32, (t, GROUP_WIDTH), 1) // SSD_HEAD_DIM

    ys = []
    for g in range(SSD_GROUPS):
        gs = slice(g * GROUP_WIDTH, (g + 1) * GROUP_WIDTH)
        ns = slice(g * SSD_STATE, (g + 1) * SSD_STATE)
        bg = bm_ref[0, :, ns]
        cg = cm_ref[0, :, ns]
        cb = lax.dot_general(cg, bg, (((1,), (1,)), ((), ())), preferred_element_type=F32)
        xg = xdt[:, gs]
        m_parts, x_parts = [], []
        for e in range(SSD_HPG):
            hd = g * SSD_HPG + e
            col = cs[:, LANE_A + hd:LANE_A + hd + 1]
            row = cs_t[LANE_A + hd:LANE_A + hd + 1, :]
            l_mat = jnp.exp(jnp.where(lower, col - row, -jnp.inf))
            m_parts.append((cb * l_mat).astype(BF16))
            x_parts.append(jnp.where(head_of_lane == e, xg, 0.0).astype(BF16))
        m_cat = jnp.concatenate(m_parts, axis=1)
        x_blk = jnp.concatenate(x_parts, axis=0)
        y_diag = jnp.dot(m_cat, x_blk, preferred_element_type=F32)

        state = state_scr[g]
        y_off = jnp.dot(cg, state.astype(BF16), preferred_element_type=F32) * decay_in[:, gs]
        xd = (xg * decay_out[:, gs]).astype(BF16)
        upd = lax.dot_general(bg, xd, (((0,), (0,)), ((), ())), preferred_element_type=F32)
        state_scr[g] = state * decay_chunk[:, gs] + upd
        ys.append(y_diag + y_off + dskip_ref[:, gs] * xs[:, gs])

    z = z_ref[0]
    outs = []
    for g in range(SSD_GROUPS):
        gs = slice(g * GROUP_WIDTH, (g + 1) * GROUP_WIDTH)
        yg = ys[g] * _silu(z[:, gs])
        yg = yg * lax.rsqrt(jnp.mean(yg * yg, axis=-1, keepdims=True) + RMS_EPS)
        outs.append((yg * ng_ref[:, gs]).astype(BF16))
    o_ref[0] = jnp.concatenate(outs, axis=1)


def _ssd(xs, bm, cm, z, small, d_skip_x, norm_g):
    b, seq, _ = xs.shape
    t = min(SSD_T, seq)
    assert seq % t == 0
    tok = lambda width: pl.BlockSpec((1, t, width), lambda bi, c: (bi, c, 0))
    const2 = lambda shape: pl.BlockSpec(shape, lambda bi, c: (0, 0))
    return pl.pallas_call(
        functools.partial(_ssd_kernel, t=t),
        grid=(b, seq // t),
        in_specs=[tok(SSD_WIDTH), tok(SSD_BC), tok(SSD_BC), tok(SSD_WIDTH), tok(LANES),
                  const2((1, SSD_WIDTH)), const2((1, SSD_WIDTH))],
        out_specs=tok(SSD_WIDTH),
        out_shape=jax.ShapeDtypeStruct((b, seq, SSD_WIDTH), BF16),
        scratch_shapes=[pltpu.VMEM((SSD_GROUPS, SSD_STATE, GROUP_WIDTH), F32)],
        compiler_params=_params("arbitrary", "arbitrary"),
    )(xs, bm, cm, z, small, d_skip_x, norm_g)


def _fox_kernel(q_ref, k_ref, v_ref, o_ref, m_scr, acc_scr, *, t):
    qi = pl.program_id(1)
    lane_head = lax.broadcasted_iota(jnp.int32, (t, LANES), 1) // FOX_HEAD_DIM
    row_i = lax.broadcasted_iota(jnp.int32, (t, t), 0)
    col_i = lax.broadcasted_iota(jnp.int32, (t, t), 1)
    causal = row_i >= col_i
    heads_per_vreg = LANES // FOX_HEAD_DIM

    def block(start, width, masked):
        ks = pl.ds(start, width)
        group = lambda hd: slice(hd * LANES, (hd + 1) * LANES)

        def logits(hd):
            s = lax.dot_general(q_ref[0, :, group(hd)], k_ref[0, ks, group(hd)],
                                (((1,), (1,)), ((), ())), preferred_element_type=F32)
            return jnp.where(causal, s, -jnp.inf) if masked else s

        def update(hd, s):
            m_prev = m_scr[hd]
            m_new = jnp.maximum(m_prev, jnp.max(s, axis=-1, keepdims=True))
            p = jnp.exp2(s - pltpu.repeat(m_new, width // LANES, axis=1)).astype(BF16)
            acc_scr[hd] = jnp.exp2(m_prev - m_new) * acc_scr[hd] + jnp.dot(
                p, v_ref[0, ks, group(hd)], preferred_element_type=F32)
            m_scr[hd] = m_new

        s_next = logits(0)
        for hd in range(FOX_HEADS):
            s_cur = s_next
            if hd + 1 < FOX_HEADS:
                s_next = logits(hd + 1)
            update(hd, s_cur)

    m_scr[...] = jnp.full(m_scr.shape, -jnp.inf, F32)
    acc_scr[...] = jnp.zeros(acc_scr.shape, F32)

    wide = 2 * t
    assert k_ref.shape[1] % wide == 0

    @pl.loop(0, qi // 2)
    def _(pj):
        block(pl.multiple_of(pj * wide, wide), wide, False)

    @pl.when(qi % 2 == 1)
    def _():
        block(pl.multiple_of((qi - 1) * t, t), t, False)

    block(pl.multiple_of(qi * t, t), t, True)
    for pair in range(FOX_HEADS // heads_per_vreg):
        out = jnp.zeros((t, LANES), F32)
        for hh in range(heads_per_vreg):
            acc = acc_scr[pair * heads_per_vreg + hh]
            ones_lane = (1 - hh) * FOX_HEAD_DIM
            row_sum = acc[:, ones_lane:ones_lane + 1]
            out = out + jnp.where(lane_head == hh, acc, 0.0) / row_sum
        o_ref[0, :, pair * LANES:(pair + 1) * LANES] = out.astype(BF16)


def _fox(qa, ka, va):
    b, seq, _ = qa.shape
    t = min(FOX_T, seq)
    assert seq % t == 0
    return pl.pallas_call(
        functools.partial(_fox_kernel, t=t),
        grid=(b, seq // t),
        in_specs=[
            pl.BlockSpec((1, t, FOX_GROUP), lambda bi, i: (bi, i, 0)),
            pl.BlockSpec((1, seq, FOX_GROUP), lambda bi, i: (bi, 0, 0)),
            pl.BlockSpec((1, seq, FOX_GROUP), lambda bi, i: (bi, 0, 0)),
        ],
        out_specs=pl.BlockSpec((1, t, FOX_WIDTH), lambda bi, i: (bi, i, 0)),
        out_shape=jax.ShapeDtypeStruct((b, seq, FOX_WIDTH), BF16),
        scratch_shapes=[pltpu.VMEM((FOX_HEADS, t, LANES), F32), pltpu.VMEM((FOX_HEADS, t, LANES), F32)],
        compiler_params=_params("arbitrary", "arbitrary"),
    )(qa, ka, va)


def _outproj_kernel(x_ref, mod_ref, yssd_ref, yfox_ref, ysc_ref, w_ref, lng_ref, lnb_ref, o_ref):
    y = jnp.dot(yssd_ref[...], w_ref[0:SSD_WIDTH, :], preferred_element_type=F32)
    y = y + jnp.dot(yfox_ref[...], w_ref[SSD_WIDTH:SSD_WIDTH + FOX_WIDTH, :], preferred_element_type=F32)
    y = y + jnp.dot(ysc_ref[...], w_ref[SSD_WIDTH + FOX_WIDTH:, :], preferred_element_type=F32)
    g = mod_ref[5:6, :]
    o_ref[...] = _layer_norm(ALPHA * x_ref[...] + g * y, lng_ref[...], lnb_ref[...])


def _outproj(x2d, mod_l, y_ssd, y_fox, y_sc, w_out, ln_g, ln_b, *, layer, seq):
    n_tok, d = x2d.shape
    tm = min(OUT_TM, seq)
    assert seq % tm == 0
    tiles_per_seq = seq // tm
    tok = lambda width: pl.BlockSpec((tm, width), lambda i: (i, 0))
    return pl.pallas_call(
        _outproj_kernel,
        grid=(n_tok // tm,),
        in_specs=[
            tok(d),
            pl.BlockSpec((None, 3 * N_SUB, d), lambda i: (i // tiles_per_seq, 0, 0)),
            tok(SSD_WIDTH), tok(FOX_WIDTH), tok(SCONV_WIDTH),
            pl.BlockSpec((None, d, d), lambda i: (layer, 0, 0)),
            pl.BlockSpec((1, d), lambda i: (0, 0)),
            pl.BlockSpec((1, d), lambda i: (0, 0)),
        ],
        out_specs=tok(d),
        out_shape=jax.ShapeDtypeStruct((n_tok, d), F32),
        compiler_params=_params("arbitrary"),
    )(x2d, mod_l, y_ssd, y_fox, y_sc, w_out, ln_g.reshape(1, d), ln_b.reshape(1, d))


def _spread_forget(f):
    lead = f.shape[:-1]
    heads_per_vreg = LANES // FOX_HEAD_DIM
    blocks = []
    for pair in range(FOX_HEADS // heads_per_vreg):
        for half in range(heads_per_vreg):
            hd = pair * heads_per_vreg + (heads_per_vreg - 1 - half)
            blocks.append(jnp.broadcast_to(f[..., hd:hd + 1], lead + (2 * FOX_PIECES,)))
            blocks.append(jnp.zeros(lead + (FOX_HEAD_DIM - 2 * FOX_PIECES,), f.dtype))
    return jnp.concatenate(blocks, axis=-1)


def _pack_mix_w_in(w):
    ssd_in = SSD_WIDTH + SSD_CONV_DIM + SSD_HEADS
    fox_in = 3 * FOX_WIDTH + FOX_HEADS
    dt_cols = w[..., SSD_WIDTH + SSD_CONV_DIM:ssd_in]
    f_cols = w[..., ssd_in + 3 * FOX_WIDTH:ssd_in + fox_in]
    pad = jnp.zeros(w.shape[:-1] + (LANES - 2 * SSD_HEADS,), w.dtype)
    return jnp.concatenate([
        w[..., :SSD_WIDTH + SSD_CONV_DIM],
        w[..., ssd_in:ssd_in + 3 * FOX_WIDTH],
        w[..., ssd_in + fox_in:],
        dt_cols, dt_cols, pad,
        _spread_forget(f_cols),
    ], axis=-1).astype(BF16)


def _lane_vec(pieces):
    row = jnp.zeros((LANES,), F32)
    for off, vec in pieces:
        row = row.at[off:off + vec.shape[0]].set(vec.astype(F32))
    return row.reshape(1, LANES)


def kernel(x, c, ln_in_g, ln_in_b, ada_w, ada_b, ffn1_w_in, ffn1_w_out, mix_w_in, mix_w_out, ssd_conv_w, ssd_conv_b, ssd_dt_bias, ssd_a_log, ssd_d, ssd_norm_g, fox_f_bias, sconv_w, ffn2_w_in, ffn2_w_out, ln_g, ln_b):
    b, seq, d = x.shape
    depth = ada_w.shape[0]
    rows = -(-b // SUBLANES) * SUBLANES
    c_pad = jnp.pad(c, ((0, rows - b), (0, 0)))
    mod = _ada(c_pad, ada_w, ada_b).reshape(depth, rows, 3 * N_SUB, d)

    w1_in, w1_out = ffn1_w_in.astype(BF16), ffn1_w_out.astype(BF16)
    w2_in, w2_out = ffn2_w_in.astype(BF16), ffn2_w_out.astype(BF16)
    wm_in, wm_out = _pack_mix_w_in(mix_w_in), mix_w_out.astype(BF16)

    x2d = x.reshape(b * seq, d)
    for l in range(depth):
        mod_l = mod[l]
        x2d = _ffn(x2d, mod_l, w1_in, w1_out, ln_g[l, 0], ln_b[l, 0], layer=l, sub=0, seq=seq,
                   pre=(ln_in_g, ln_in_b) if l == 0 else None)

        small_bias = _lane_vec([(LANE_DT, ssd_dt_bias[l]), (LANE_A, ssd_dt_bias[l])])
        alog_vec = _lane_vec([(LANE_A, ssd_a_log[l])])
        z, xs, bm, cm, qa, ka, va, y_sc, small = _proj(
            x2d.reshape(b, seq, d), mod_l, wm_in,
            ssd_conv_w[l], ssd_conv_b[l].reshape(1, SSD_CONV_DIM), small_bias, alog_vec, sconv_w[l],
            _spread_forget(fox_f_bias[l]).reshape(1, FOX_WIDTH), layer=l)
        d_skip_x = jnp.repeat(ssd_d[l], SSD_HEAD_DIM).reshape(1, SSD_WIDTH)
        y_ssd = _ssd(xs, bm, cm, z, small, d_skip_x, ssd_norm_g[l].reshape(1, SSD_WIDTH))
        y_fox = _fox(qa, ka, va)
        x2d = _outproj(x2d, mod_l, y_ssd.reshape(b * seq, SSD_WIDTH), y_fox.reshape(b * seq, FOX_WIDTH),
                       y_sc.reshape(b * seq, SCONV_WIDTH), wm_out, ln_g[l, 1], ln_b[l, 1],
                       layer=l, seq=seq)

        x2d = _ffn(x2d, mod_l, w2_in, w2_out, ln_g[l, 2], ln_b[l, 2], layer=l, sub=2, seq=seq)
    return x2d.reshape(b, seq, d)
```

```python
import functools
import math

import jax
import jax.numpy as jnp
from jax import lax
from jax.experimental import pallas as pl
from jax.experimental.pallas import tpu as pltpu

F32 = jnp.float32
BF16 = jnp.bfloat16
HIGHEST = lax.Precision.HIGHEST

D_MODEL = 1024
DEPTH = 2
N_SUB = 3
D_FF = 2816
SSD_WIDTH = 512
SSD_HEAD_DIM = 64
SSD_HEADS = 8
SSD_GROUPS = 2
SSD_HPG = SSD_HEADS // SSD_GROUPS
SSD_STATE = 128
SSD_CONV = 4
SSD_BC = SSD_GROUPS * SSD_STATE
SSD_CONV_DIM = SSD_WIDTH + 2 * SSD_BC
GROUP_WIDTH = SSD_WIDTH // SSD_GROUPS
FOX_WIDTH = 256
FOX_HEAD_DIM = 64
FOX_HEADS = 4
SCONV_WIDTH = 256
SCONV_K = 3
ALPHA = (2 * DEPTH) ** 0.25
LN_EPS = 1e-5
RMS_EPS = 1e-5
LOG2_E = math.log2(math.e)

LANES = 128
SUBLANES = 8
VMEM_LIMIT = 56 * 1024 * 1024

COL_Z = 0
COL_XBC = COL_Z + SSD_WIDTH
COL_QKV = COL_XBC + SSD_CONV_DIM
COL_SC = COL_QKV + 3 * FOX_WIDTH
COL_SMALL = COL_SC + 3 * SCONV_WIDTH
COL_F = COL_SMALL + LANES
D_PROJ = COL_F + LANES
LANE_DT = 0
LANE_A = 8
LANE_A2 = 16
FOX_PIECES = 3
FOX_SLOT = 2 * FOX_PIECES
FOX_GROUP = FOX_HEADS * LANES
HEADS_PER_VREG = LANES // FOX_HEAD_DIM


def _forget_lane(hd):
    pair, hh = divmod(hd, HEADS_PER_VREG)
    return (HEADS_PER_VREG - 1 - hh) * FOX_HEAD_DIM + pair * FOX_SLOT

FFN_TM = 1024
FFN_TF = 256
PROJ_TM = 512
SSD_T = 256
SSD_ROWS = 512
FOX_T = 512
OUT_TM = 1024
ADA_TN = 1024


def _layer_norm(x, g, b):
    mu = jnp.mean(x, axis=-1, keepdims=True)
    xc = x - mu
    var = jnp.mean(xc * xc, axis=-1, keepdims=True)
    return xc * lax.rsqrt(var + LN_EPS) * g + b


def _silu(x):
    return x / (1.0 + jnp.exp(-x))


def _softplus(x):
    return jnp.maximum(x, 0.0) + jnp.log1p(jnp.exp(-jnp.abs(x)))


def _split_bf16(x, pieces):
    out = []
    for _ in range(pieces):
        part = x.astype(BF16)
        out.append(part)
        x = x - part.astype(F32)
    return out


def _params(*semantics):
    return pltpu.CompilerParams(dimension_semantics=semantics, vmem_limit_bytes=VMEM_LIMIT)


def _ada_kernel(c_ref, w_ref, b_ref, o_ref):
    o_ref[...] = jnp.dot(_silu(c_ref[...]), w_ref[...], preferred_element_type=F32,
                         precision=HIGHEST) + b_ref[...]


def _ada(c_pad, ada_w, ada_b):
    depth, d, n = ada_w.shape
    rows = c_pad.shape[0]
    return pl.pallas_call(
        _ada_kernel,
        grid=(depth, n // ADA_TN),
        in_specs=[
            pl.BlockSpec((rows, d), lambda l, j: (0, 0)),
            pl.BlockSpec((None, d, ADA_TN), lambda l, j: (l, 0, j)),
            pl.BlockSpec((None, 1, ADA_TN), lambda l, j: (l, 0, j)),
        ],
        out_specs=pl.BlockSpec((None, rows, ADA_TN), lambda l, j: (l, 0, j)),
        out_shape=jax.ShapeDtypeStruct((depth, rows, n), F32),
        compiler_params=_params("arbitrary", "arbitrary"),
    )(c_pad, ada_w, ada_b.reshape(depth, 1, n))


def _ffn_kernel(*refs, sub, pre_ln):
    if pre_ln:
        x_ref, mod_ref, wi_ref, wo_ref, lng_ref, lnb_ref, ing_ref, inb_ref, o_ref, act_scr = refs
    else:
        x_ref, mod_ref, wi_ref, wo_ref, lng_ref, lnb_ref, o_ref, act_scr = refs
    x = x_ref[...]
    if pre_ln:
        x = _layer_norm(x, ing_ref[...], inb_ref[...])
    shift = mod_ref[3 * sub:3 * sub + 1, :]
    scale = mod_ref[3 * sub + 1:3 * sub + 2, :]
    h = (x * (1.0 + scale) + shift).astype(BF16)
    for c in range(D_FF // FFN_TF):
        cols = slice(c * FFN_TF, (c + 1) * FFN_TF)
        gate = jnp.dot(h, wi_ref[:, cols], preferred_element_type=F32)
        up = jnp.dot(h, wi_ref[:, D_FF + c * FFN_TF:D_FF + (c + 1) * FFN_TF], preferred_element_type=F32)
        act_scr[:, cols] = (_silu(gate) * up).astype(BF16)
    y = jnp.dot(act_scr[...], wo_ref[...], preferred_element_type=F32)
    g = mod_ref[3 * sub + 2:3 * sub + 3, :]
    o_ref[...] = _layer_norm(ALPHA * x + (0.5 * g) * y, lng_ref[...], lnb_ref[...])


def _ffn(x2d, mod_l, w_in, w_out, ln_g, ln_b, *, layer, sub, seq, pre=None):
    n_tok, d = x2d.shape
    tm = min(FFN_TM, seq)
    assert seq % tm == 0 and n_tok % seq == 0 and D_FF % FFN_TF == 0
    tiles_per_seq = seq // tm
    row = lambda v: v.reshape(1, d)
    resident = lambda shape: pl.BlockSpec(shape, lambda i: (0, 0), pipeline_mode=pl.Buffered(1))
    weight = lambda shape: pl.BlockSpec((None,) + shape, lambda i: (layer, 0, 0),
                                        pipeline_mode=pl.Buffered(1))
    in_specs = [
        pl.BlockSpec((tm, d), lambda i: (i, 0)),
        pl.BlockSpec((None, 3 * N_SUB, d), lambda i: (i // tiles_per_seq, 0, 0)),
        weight((d, 2 * D_FF)),
        weight((D_FF, d)),
        resident((1, d)),
        resident((1, d)),
    ]
    args = [x2d, mod_l, w_in, w_out, row(ln_g), row(ln_b)]
    if pre is not None:
        in_specs += [resident((1, d))] * 2
        args += [row(pre[0]), row(pre[1])]
    return pl.pallas_call(
        functools.partial(_ffn_kernel, sub=sub, pre_ln=pre is not None),
        grid=(n_tok // tm,),
        in_specs=in_specs,
        out_specs=pl.BlockSpec((tm, d), lambda i: (i, 0)),
        out_shape=jax.ShapeDtypeStruct((n_tok, d), F32),
        scratch_shapes=[pltpu.VMEM((tm, D_FF), BF16)],
        compiler_params=_params("arbitrary"),
    )(*args)


def _proj_kernel(x_ref, mod_ref, w_ref, cw_ref, cb_ref, sbias_ref, alog_ref, scw_ref, fbias_ref, tri_ref,
                 z_ref, xs_ref, bm_ref, cm_ref, q_ref, k_ref, v_ref, ysc_ref, small_ref,
                 xbc_scr, u_scr, carry_scr, *, tm):
    t = pl.program_id(1)
    halo = SUBLANES

    @pl.when(t == 0)
    def _():
        xbc_scr[0:halo, :] = jnp.zeros((halo, SSD_CONV_DIM), F32)
        u_scr[0:halo, :] = jnp.zeros((halo, SCONV_WIDTH), F32)
        carry_scr[...] = jnp.zeros_like(carry_scr)

    @pl.when(t > 0)
    def _():
        xbc_scr[0:halo, :] = xbc_scr[tm:tm + halo, :]
        u_scr[0:halo, :] = u_scr[tm:tm + halo, :]

    shift = mod_ref[3:4, :]
    scale = mod_ref[4:5, :]
    h = (x_ref[0] * (1.0 + scale) + shift).astype(BF16)

    def seg(start, width):
        return jnp.dot(h, w_ref[:, start:start + width], preferred_element_type=F32)

    z_ref[0] = seg(COL_Z, SSD_WIDTH)

    xbc_scr[halo:halo + tm, :] = seg(COL_XBC, SSD_CONV_DIM)
    conv = cb_ref[...] + cw_ref[SSD_CONV - 1:SSD_CONV, :] * xbc_scr[halo:halo + tm, :]
    for k in range(SSD_CONV - 1):
        back = SSD_CONV - 1 - k
        conv = conv + cw_ref[k:k + 1, :] * xbc_scr[halo - back:halo - back + tm, :]
    xbc = _silu(conv)
    xs_ref[0] = xbc[:, :SSD_WIDTH]
    bm_ref[0] = xbc[:, SSD_WIDTH:SSD_WIDTH + SSD_BC].astype(BF16)
    cm_ref[0] = xbc[:, SSD_WIDTH + SSD_BC:].astype(BF16)

    lane = lax.broadcasted_iota(jnp.int32, (tm, LANES), 1)
    pos = lane % FOX_HEAD_DIM
    used = pos < (FOX_HEADS // HEADS_PER_VREG) * FOX_SLOT
    log_f = jnp.where(used, -_softplus(-(seg(COL_F, LANES) + fbias_ref[...])), 0.0)
    cum_f = carry_scr[0:1, :]
    for part in _split_bf16(log_f, 3):
        cum_f = cum_f + jnp.dot(tri_ref[...], part, preferred_element_type=F32)
    carry_scr[...] = jnp.broadcast_to(cum_f[tm - 1:tm, :], carry_scr.shape)
    hi, mid, lo = (part.astype(F32) for part in _split_bf16(cum_f * LOG2_E, FOX_PIECES))
    piece = jnp.where(pos % FOX_PIECES == 0, hi, jnp.where(pos % FOX_PIECES == 1, mid, lo))
    in_a = pos % FOX_SLOT < FOX_PIECES
    f_q = jnp.where(in_a, piece, 1.0)
    f_k = jnp.where(in_a, 1.0, -piece)

    qkv = seg(COL_QKV, 3 * FOX_WIDTH)
    q = qkv[:, :FOX_WIDTH] * (FOX_HEAD_DIM ** -0.5 * LOG2_E)
    k = qkv[:, FOX_WIDTH:2 * FOX_WIDTH]
    v = qkv[:, 2 * FOX_WIDTH:]
    for hd in range(FOX_HEADS):
        pair, hh = divmod(hd, HEADS_PER_VREG)
        ps = slice(pair * LANES, (pair + 1) * LANES)
        gs = slice(hd * LANES, (hd + 1) * LANES)
        mine = (lane // FOX_HEAD_DIM) == hh
        first = _forget_lane(hd)
        forget = (lane >= first) & (lane < first + FOX_SLOT)
        ones_col = jnp.where(lane == (1 - hh) * FOX_HEAD_DIM, 1.0, 0.0)
        q_ref[0, :, gs] = jnp.where(mine, q[:, ps], jnp.where(forget, f_q, 0.0)).astype(BF16)
        k_ref[0, :, gs] = jnp.where(mine, k[:, ps], jnp.where(forget, f_k, 0.0)).astype(BF16)
        v_ref[0, :, gs] = jnp.where(mine, v[:, ps], ones_col).astype(BF16)

    sc = seg(COL_SC, 3 * SCONV_WIDTH)
    u_scr[halo:halo + tm, :] = sc[:, SCONV_WIDTH:2 * SCONV_WIDTH] * sc[:, 2 * SCONV_WIDTH:]
    cu = scw_ref[SCONV_K - 1:SCONV_K, :] * u_scr[halo:halo + tm, :]
    for k in range(SCONV_K - 1):
        back = SCONV_K - 1 - k
        cu = cu + scw_ref[k:k + 1, :] * u_scr[halo - back:halo - back + tm, :]
    ysc_ref[0] = (sc[:, :SCONV_WIDTH] * cu).astype(BF16)

    dt = _softplus(seg(COL_SMALL, LANES) + sbias_ref[...])
    a_neg = -jnp.exp(alog_ref[...])
    small_ref[0] = jnp.where(lane < LANE_A, dt, dt * a_neg)


def _proj(x, mod_l, w_cat, conv_w, conv_b, small_bias, alog_vec, sconv_w, f_bias, *, layer):
    b, seq, d = x.shape
    tm = min(PROJ_TM, seq)
    assert seq % tm == 0
    nt = seq // tm
    tok = lambda width: pl.BlockSpec((1, tm, width), lambda bi, t: (bi, t, 0))
    const2 = lambda shape: pl.BlockSpec(shape, lambda bi, t: (0, 0))
    out_shapes = (
        jax.ShapeDtypeStruct((b, seq, SSD_WIDTH), F32),
        jax.ShapeDtypeStruct((b, seq, SSD_WIDTH), F32),
        jax.ShapeDtypeStruct((b, seq, SSD_BC), BF16),
        jax.ShapeDtypeStruct((b, seq, SSD_BC), BF16),
        jax.ShapeDtypeStruct((b, seq, FOX_GROUP), BF16),
        jax.ShapeDtypeStruct((b, seq, FOX_GROUP), BF16),
        jax.ShapeDtypeStruct((b, seq, FOX_GROUP), BF16),
        jax.ShapeDtypeStruct((b, seq, SCONV_WIDTH), BF16),
        jax.ShapeDtypeStruct((b, seq, LANES), F32),
    )
    out_specs = (
        tok(SSD_WIDTH), tok(SSD_WIDTH), tok(SSD_BC), tok(SSD_BC),
        tok(FOX_GROUP), tok(FOX_GROUP), tok(FOX_GROUP), tok(SCONV_WIDTH), tok(LANES),
    )
    return pl.pallas_call(
        functools.partial(_proj_kernel, tm=tm),
        grid=(b, nt),
        in_specs=[
            tok(d),
            pl.BlockSpec((None, 3 * N_SUB, d), lambda bi, t: (bi, 0, 0)),
            pl.BlockSpec((None, d, D_PROJ), lambda bi, t: (layer, 0, 0)),
            const2((SSD_CONV, SSD_CONV_DIM)),
            const2((1, SSD_CONV_DIM)),
            const2((1, LANES)),
            const2((1, LANES)),
            const2((SCONV_K, SCONV_WIDTH)),
            const2((1, LANES)),
            const2((tm, tm)),
        ],
        out_specs=out_specs,
        out_shape=out_shapes,
        scratch_shapes=[
            pltpu.VMEM((tm + SUBLANES, SSD_CONV_DIM), F32),
            pltpu.VMEM((tm + SUBLANES, SCONV_WIDTH), F32),
            pltpu.VMEM((SUBLANES, LANES), F32),
        ],
        compiler_params=_params("arbitrary", "arbitrary"),
    )(x, mod_l, w_cat, conv_w, conv_b, small_bias, alog_vec, sconv_w, f_bias, _lower_ones(tm))


def _ssd_kernel(xs_ref, bm_ref, cm_ref, z_ref, small_ref, dskip_ref, ng_ref, tri_ref, spread_ref,
                o_ref, state_scr, *, t, chunks):
    @pl.when(pl.program_id(1) == 0)
    def _():
        state_scr[...] = jnp.zeros_like(state_scr)

    lane = lax.broadcasted_iota(jnp.int32, (t, LANES), 1)
    row_i = lax.broadcasted_iota(jnp.int32, (t, t), 0)
    col_i = lax.broadcasted_iota(jnp.int32, (t, t), 1)
    lower = row_i >= col_i
    head_of_lane = lax.broadcasted_iota(jnp.int32, (t, GROUP_WIDTH), 1) // SSD_HEAD_DIM

    for ci in range(chunks):
        rows = slice(ci * t, (ci + 1) * t)
        small = small_ref[0, rows, :]
        a_only = jnp.where((lane >= LANE_A) & (lane < LANE_A2 + SSD_HEADS), small, 0.0)
        cs = jnp.zeros((t, LANES), F32)
        for part in _split_bf16(a_only, 3):
            cs = cs + jnp.dot(tri_ref[...], part, preferred_element_type=F32)
        cs_t = cs.T
        total = cs[t - 1:t, :]
        per_head = jnp.where(lane < LANE_A, small,
                             jnp.where(lane < LANE_A2, jnp.exp(cs), jnp.exp(total - cs)))
        wide = jnp.zeros((t, 3 * SSD_WIDTH), F32)
        for part in _split_bf16(per_head, 2):
            wide = wide + jnp.dot(part, spread_ref[...], preferred_element_type=F32)
        dt_x = wide[:, :SSD_WIDTH]
        decay_in = wide[:, SSD_WIDTH:2 * SSD_WIDTH]
        decay_out = wide[:, 2 * SSD_WIDTH:]
        decay_chunk = decay_in[t - 1:t, :]

        xs = xs_ref[0, rows, :]
        xdt = xs * dt_x
        ys = []
        for g in range(SSD_GROUPS):
            gs = slice(g * GROUP_WIDTH, (g + 1) * GROUP_WIDTH)
            ns = slice(g * SSD_STATE, (g + 1) * SSD_STATE)
            bg = bm_ref[0, rows, ns]
            cg = cm_ref[0, rows, ns]
            cb = lax.dot_general(cg, bg, (((1,), (1,)), ((), ())), preferred_element_type=F32)
            xg = xdt[:, gs]
            m_parts, x_parts = [], []
            for e in range(SSD_HPG):
                hd = g * SSD_HPG + e
                col = cs[:, LANE_A + hd:LANE_A + hd + 1]
                row = cs_t[LANE_A + hd:LANE_A + hd + 1, :]
                l_mat = jnp.exp(jnp.where(lower, col - row, -jnp.inf))
                m_parts.append((cb * l_mat).astype(BF16))
                x_parts.append(jnp.where(head_of_lane == e, xg, 0.0).astype(BF16))
            m_cat = jnp.concatenate(m_parts, axis=1)
            x_blk = jnp.concatenate(x_parts, axis=0)
            y_diag = jnp.dot(m_cat, x_blk, preferred_element_type=F32)

            state = state_scr[g]
            y_off = jnp.dot(cg, state.astype(BF16), preferred_element_type=F32) * decay_in[:, gs]
            xd = (xg * decay_out[:, gs]).astype(BF16)
            upd = lax.dot_general(bg, xd, (((0,), (0,)), ((), ())), preferred_element_type=F32)
            state_scr[g] = state * decay_chunk[:, gs] + upd
            ys.append(y_diag + y_off + dskip_ref[:, gs] * xs[:, gs])

        z = z_ref[0, rows, :]
        for g in range(SSD_GROUPS):
            gs = slice(g * GROUP_WIDTH, (g + 1) * GROUP_WIDTH)
            yg = ys[g] * _silu(z[:, gs])
            yg = yg * lax.rsqrt(jnp.mean(yg * yg, axis=-1, keepdims=True) + RMS_EPS)
            o_ref[0, rows, gs] = (yg * ng_ref[:, gs]).astype(BF16)


def _head_spread():
    src = jnp.arange(LANES)[:, None]
    col = jnp.arange(3 * SSD_WIDTH)[None, :]
    first_lane = jnp.array([LANE_DT, LANE_A, LANE_A2])[col // SSD_WIDTH]
    return (src == first_lane + (col % SSD_WIDTH) // SSD_HEAD_DIM).astype(BF16)


def _ssd(xs, bm, cm, z, small, d_skip_x, norm_g):
    b, seq, _ = xs.shape
    t = min(SSD_T, seq)
    rows = min(SSD_ROWS, seq)
    assert seq % rows == 0 and rows % t == 0
    tok = lambda width: pl.BlockSpec((1, rows, width), lambda bi, c: (bi, c, 0))
    const2 = lambda shape: pl.BlockSpec(shape, lambda bi, c: (0, 0))
    return pl.pallas_call(
        functools.partial(_ssd_kernel, t=t, chunks=rows // t),
        grid=(b, seq // rows),
        in_specs=[tok(SSD_WIDTH), tok(SSD_BC), tok(SSD_BC), tok(SSD_WIDTH), tok(LANES),
                  const2((1, SSD_WIDTH)), const2((1, SSD_WIDTH)),
                  const2((t, t)), const2((LANES, 3 * SSD_WIDTH))],
        out_specs=tok(SSD_WIDTH),
        out_shape=jax.ShapeDtypeStruct((b, seq, SSD_WIDTH), BF16),
        scratch_shapes=[pltpu.VMEM((SSD_GROUPS, SSD_STATE, GROUP_WIDTH), F32)],
        compiler_params=_params("arbitrary", "arbitrary"),
    )(xs, bm, cm, z, small, d_skip_x, norm_g, _lower_ones(t), _head_spread())


def _fox_kernel(q_ref, k_ref, v_ref, o_ref, m_scr, acc_scr, *, t):
    qi = pl.program_id(1)
    lane_head = lax.broadcasted_iota(jnp.int32, (t, LANES), 1) // FOX_HEAD_DIM
    row_i = lax.broadcasted_iota(jnp.int32, (t, t), 0)
    col_i = lax.broadcasted_iota(jnp.int32, (t, t), 1)
    causal = row_i >= col_i
    heads_per_vreg = LANES // FOX_HEAD_DIM

    def block(start, width, masked):
        ks = pl.ds(start, width)
        group = lambda hd: slice(hd * LANES, (hd + 1) * LANES)

        def logits(hd):
            s = lax.dot_general(q_ref[0, :, group(hd)], k_ref[0, ks, group(hd)],
                                (((1,), (1,)), ((), ())), preferred_element_type=F32)
            return jnp.where(causal, s, -jnp.inf) if masked else s

        def update(hd, s):
            m_prev = m_scr[hd]
            m_new = jnp.maximum(m_prev, jnp.max(s, axis=-1, keepdims=True))
            p = jnp.exp2(s - pltpu.repeat(m_new, width // LANES, axis=1)).astype(BF16)
            acc_scr[hd] = jnp.exp2(m_prev - m_new) * acc_scr[hd] + jnp.dot(
                p, v_ref[0, ks, group(hd)], preferred_element_type=F32)
            m_scr[hd] = m_new

        s_next = logits(0)
        for hd in range(FOX_HEADS):
            s_cur = s_next
            if hd + 1 < FOX_HEADS:
                s_next = logits(hd + 1)
            update(hd, s_cur)

    m_scr[...] = jnp.full(m_scr.shape, -jnp.inf, F32)
    acc_scr[...] = jnp.zeros(acc_scr.shape, F32)

    wide = 2 * t
    assert k_ref.shape[1] % wide == 0

    @pl.loop(0, qi // 2)
    def _(pj):
        block(pl.multiple_of(pj * wide, wide), wide, False)

    @pl.when(qi % 2 == 1)
    def _():
        block(pl.multiple_of((qi - 1) * t, t), t, False)

    block(pl.multiple_of(qi * t, t), t, True)
    for pair in range(FOX_HEADS // heads_per_vreg):
        out = jnp.zeros((t, LANES), F32)
        for hh in range(heads_per_vreg):
            acc = acc_scr[pair * heads_per_vreg + hh]
            ones_lane = (1 - hh) * FOX_HEAD_DIM
            row_sum = acc[:, ones_lane:ones_lane + 1]
            out = out + jnp.where(lane_head == hh, acc, 0.0) / row_sum
        o_ref[0, :, pair * LANES:(pair + 1) * LANES] = out.astype(BF16)


def _fox(qa, ka, va):
    b, seq, _ = qa.shape
    t = min(FOX_T, seq)
    assert seq % t == 0
    return pl.pallas_call(
        functools.partial(_fox_kernel, t=t),
        grid=(b, seq // t),
        in_specs=[
            pl.BlockSpec((1, t, FOX_GROUP), lambda bi, i: (bi, i, 0)),
            pl.BlockSpec((1, seq, FOX_GROUP), lambda bi, i: (bi, 0, 0)),
            pl.BlockSpec((1, seq, FOX_GROUP), lambda bi, i: (bi, 0, 0)),
        ],
        out_specs=pl.BlockSpec((1, t, FOX_WIDTH), lambda bi, i: (bi, i, 0)),
        out_shape=jax.ShapeDtypeStruct((b, seq, FOX_WIDTH), BF16),
        scratch_shapes=[pltpu.VMEM((FOX_HEADS, t, LANES), F32), pltpu.VMEM((FOX_HEADS, t, LANES), F32)],
        compiler_params=_params("arbitrary", "arbitrary"),
    )(qa, ka, va)


def _outproj_kernel(x_ref, mod_ref, yssd_ref, yfox_ref, ysc_ref, w_ref, lng_ref, lnb_ref, o_ref):
    y = jnp.dot(yssd_ref[...], w_ref[0:SSD_WIDTH, :], preferred_element_type=F32)
    y = y + jnp.dot(yfox_ref[...], w_ref[SSD_WIDTH:SSD_WIDTH + FOX_WIDTH, :], preferred_element_type=F32)
    y = y + jnp.dot(ysc_ref[...], w_ref[SSD_WIDTH + FOX_WIDTH:, :], preferred_element_type=F32)
    g = mod_ref[5:6, :]
    o_ref[...] = _layer_norm(ALPHA * x_ref[...] + g * y, lng_ref[...], lnb_ref[...])


def _outproj(x2d, mod_l, y_ssd, y_fox, y_sc, w_out, ln_g, ln_b, *, layer, seq):
    n_tok, d = x2d.shape
    tm = min(OUT_TM, seq)
    assert seq % tm == 0
    tiles_per_seq = seq // tm
    tok = lambda width: pl.BlockSpec((tm, width), lambda i: (i, 0))
    return pl.pallas_call(
        _outproj_kernel,
        grid=(n_tok // tm,),
        in_specs=[
            tok(d),
            pl.BlockSpec((None, 3 * N_SUB, d), lambda i: (i // tiles_per_seq, 0, 0)),
            tok(SSD_WIDTH), tok(FOX_WIDTH), tok(SCONV_WIDTH),
            pl.BlockSpec((None, d, d), lambda i: (layer, 0, 0)),
            pl.BlockSpec((1, d), lambda i: (0, 0)),
            pl.BlockSpec((1, d), lambda i: (0, 0)),
        ],
        out_specs=tok(d),
        out_shape=jax.ShapeDtypeStruct((n_tok, d), F32),
        compiler_params=_params("arbitrary"),
    )(x2d, mod_l, y_ssd, y_fox, y_sc, w_out, ln_g.reshape(1, d), ln_b.reshape(1, d))


def _lower_ones(n):
    return jnp.tril(jnp.ones((n, n), BF16))


def _spread_forget(f):
    lead = f.shape[:-1]
    order = sorted(range(FOX_HEADS), key=_forget_lane)
    blocks, lane = [], 0
    for hd in order:
        blocks.append(jnp.zeros(lead + (_forget_lane(hd) - lane,), f.dtype))
        blocks.append(jnp.broadcast_to(f[..., hd:hd + 1], lead + (FOX_SLOT,)))
        lane = _forget_lane(hd) + FOX_SLOT
    blocks.append(jnp.zeros(lead + (LANES - lane,), f.dtype))
    return jnp.concatenate(blocks, axis=-1)


def _pack_mix_w_in(w):
    ssd_in = SSD_WIDTH + SSD_CONV_DIM + SSD_HEADS
    fox_in = 3 * FOX_WIDTH + FOX_HEADS
    dt_cols = w[..., SSD_WIDTH + SSD_CONV_DIM:ssd_in]
    f_cols = w[..., ssd_in + 3 * FOX_WIDTH:ssd_in + fox_in]
    pad = jnp.zeros(w.shape[:-1] + (LANES - 3 * SSD_HEADS,), w.dtype)
    return jnp.concatenate([
        w[..., :SSD_WIDTH + SSD_CONV_DIM],
        w[..., ssd_in:ssd_in + 3 * FOX_WIDTH],
        w[..., ssd_in + fox_in:],
        dt_cols, dt_cols, dt_cols, pad,
        _spread_forget(f_cols),
    ], axis=-1).astype(BF16)


def _lane_vec(pieces):
    row = jnp.zeros((LANES,), F32)
    for off, vec in pieces:
        row = row.at[off:off + vec.shape[0]].set(vec.astype(F32))
    return row.reshape(1, LANES)


def kernel(x, c, ln_in_g, ln_in_b, ada_w, ada_b, ffn1_w_in, ffn1_w_out, mix_w_in, mix_w_out, ssd_conv_w, ssd_conv_b, ssd_dt_bias, ssd_a_log, ssd_d, ssd_norm_g, fox_f_bias, sconv_w, ffn2_w_in, ffn2_w_out, ln_g, ln_b):
    b, seq, d = x.shape
    depth = ada_w.shape[0]
    rows = -(-b // SUBLANES) * SUBLANES
    c_pad = jnp.pad(c, ((0, rows - b), (0, 0)))
    mod = _ada(c_pad, ada_w, ada_b).reshape(depth, rows, 3 * N_SUB, d)

    w1_in, w1_out = ffn1_w_in.astype(BF16), ffn1_w_out.astype(BF16)
    w2_in, w2_out = ffn2_w_in.astype(BF16), ffn2_w_out.astype(BF16)
    wm_in, wm_out = _pack_mix_w_in(mix_w_in), mix_w_out.astype(BF16)

    x2d = x.reshape(b * seq, d)
    for l in range(depth):
        mod_l = mod[l]
        x2d = _ffn(x2d, mod_l, w1_in, w1_out, ln_g[l, 0], ln_b[l, 0], layer=l, sub=0, seq=seq,
                   pre=(ln_in_g, ln_in_b) if l == 0 else None)

        small_bias = _lane_vec([(lane, ssd_dt_bias[l]) for lane in (LANE_DT, LANE_A, LANE_A2)])
        alog_vec = _lane_vec([(LANE_A, ssd_a_log[l]), (LANE_A2, ssd_a_log[l])])
        z, xs, bm, cm, qa, ka, va, y_sc, small = _proj(
            x2d.reshape(b, seq, d), mod_l, wm_in,
            ssd_conv_w[l], ssd_conv_b[l].reshape(1, SSD_CONV_DIM), small_bias, alog_vec, sconv_w[l],
            _spread_forget(fox_f_bias[l]).reshape(1, LANES), layer=l)
        d_skip_x = jnp.repeat(ssd_d[l], SSD_HEAD_DIM).reshape(1, SSD_WIDTH)
        y_ssd = _ssd(xs, bm, cm, z, small, d_skip_x, ssd_norm_g[l].reshape(1, SSD_WIDTH))
        y_fox = _fox(qa, ka, va)
        x2d = _outproj(x2d, mod_l, y_ssd.reshape(b * seq, SSD_WIDTH), y_fox.reshape(b * seq, FOX_WIDTH),
                       y_sc.reshape(b * seq, SCONV_WIDTH), wm_out, ln_g[l, 1], ln_b[l, 1],
                       layer=l, seq=seq)

        x2d = _ffn(x2d, mod_l, w2_in, w2_out, ln_g[l, 2], ln_b[l, 2], layer=l, sub=2, seq=seq)
    return x2d.reshape(b, seq, d)
```

```python
import functools
import math

import jax
import jax.numpy as jnp
from jax import lax
from jax.experimental import pallas as pl
from jax.experimental.pallas import tpu as pltpu

F32 = jnp.float32
BF16 = jnp.bfloat16
HIGHEST = lax.Precision.HIGHEST

D_MODEL = 1024
DEPTH = 2
N_SUB = 3
D_FF = 2816
SSD_WIDTH = 512
SSD_HEAD_DIM = 64
SSD_HEADS = 8
SSD_GROUPS = 2
SSD_HPG = SSD_HEADS // SSD_GROUPS
SSD_STATE = 128
SSD_CONV = 4
SSD_BC = SSD_GROUPS * SSD_STATE
SSD_CONV_DIM = SSD_WIDTH + 2 * SSD_BC
GROUP_WIDTH = SSD_WIDTH // SSD_GROUPS
FOX_WIDTH = 256
FOX_HEAD_DIM = 64
FOX_HEADS = 4
SCONV_WIDTH = 256
SCONV_K = 3
ALPHA = (2 * DEPTH) ** 0.25
LN_EPS = 1e-5
RMS_EPS = 1e-5
LOG2_E = math.log2(math.e)

LANES = 128
SUBLANES = 8
VMEM_LIMIT = 56 * 1024 * 1024

COL_Z = 0
COL_XBC = COL_Z + SSD_WIDTH
COL_QKV = COL_XBC + SSD_CONV_DIM
COL_SC = COL_QKV + 3 * FOX_WIDTH
COL_SMALL = COL_SC + 3 * SCONV_WIDTH
COL_F = COL_SMALL + LANES
D_PROJ = COL_F + LANES
LANE_DT = 0
LANE_A = 8
LANE_A2 = 16
FOX_PIECES = 3
FOX_SLOT = 2 * FOX_PIECES
FOX_GROUP = FOX_HEADS * LANES
HEADS_PER_VREG = LANES // FOX_HEAD_DIM


def _forget_lane(hd):
    pair, hh = divmod(hd, HEADS_PER_VREG)
    return (HEADS_PER_VREG - 1 - hh) * FOX_HEAD_DIM + pair * FOX_SLOT

FFN_TM = 1024
FFN_TF = 256
PROJ_TM = 512
SSD_T = 256
SSD_ROWS = 512
FOX_T = 512
ADA_TN = 1024


def _layer_norm(x, g, b):
    mu = jnp.mean(x, axis=-1, keepdims=True)
    xc = x - mu
    var = jnp.mean(xc * xc, axis=-1, keepdims=True)
    return xc * lax.rsqrt(var + LN_EPS) * g + b


def _silu(x):
    return x / (1.0 + jnp.exp2(x * (-LOG2_E)))


def _softplus(x):
    return jnp.maximum(x, 0.0) + jnp.log1p(jnp.exp(-jnp.abs(x)))


def _split_bf16(x, pieces):
    out = []
    for _ in range(pieces):
        part = x.astype(BF16)
        out.append(part)
        x = x - part.astype(F32)
    return out


def _params(*semantics):
    return pltpu.CompilerParams(dimension_semantics=semantics, vmem_limit_bytes=VMEM_LIMIT)


def _ada_kernel(c_ref, w_ref, b_ref, o_ref):
    o_ref[...] = jnp.dot(_silu(c_ref[...]), w_ref[...], preferred_element_type=F32,
                         precision=HIGHEST) + b_ref[...]


def _ada(c_pad, ada_w, ada_b):
    depth, d, n = ada_w.shape
    rows = c_pad.shape[0]
    return pl.pallas_call(
        _ada_kernel,
        grid=(depth, n // ADA_TN),
        in_specs=[
            pl.BlockSpec((rows, d), lambda l, j: (0, 0)),
            pl.BlockSpec((None, d, ADA_TN), lambda l, j: (l, 0, j)),
            pl.BlockSpec((None, 1, ADA_TN), lambda l, j: (l, 0, j)),
        ],
        out_specs=pl.BlockSpec((None, rows, ADA_TN), lambda l, j: (l, 0, j)),
        out_shape=jax.ShapeDtypeStruct((depth, rows, n), F32),
        compiler_params=_params("arbitrary", "arbitrary"),
    )(c_pad, ada_w, ada_b.reshape(depth, 1, n))


def _ffn_kernel(*refs, sub, pre_ln, mix):
    x_ref, mod_ref, wi_ref, wo_ref, lng_ref, lnb_ref = refs[:6]
    extra = list(refs[6:-2])
    o_ref, act_scr = refs[-2:]
    x = x_ref[...]
    if pre_ln:
        ing_ref, inb_ref = extra[:2]
        extra = extra[2:]
        x = _layer_norm(x, ing_ref[...], inb_ref[...])
    if mix:
        yssd_ref, yfox_ref, ysc_ref, wm_ref, mlng_ref, mlnb_ref = extra
        y = jnp.dot(yssd_ref[...], wm_ref[0:SSD_WIDTH, :], preferred_element_type=F32)
        y = y + jnp.dot(yfox_ref[...], wm_ref[SSD_WIDTH:SSD_WIDTH + FOX_WIDTH, :],
                        preferred_element_type=F32)
        y = y + jnp.dot(ysc_ref[...], wm_ref[SSD_WIDTH + FOX_WIDTH:, :], preferred_element_type=F32)
        x = _layer_norm(ALPHA * x + mod_ref[5:6, :] * y, mlng_ref[...], mlnb_ref[...])
    shift = mod_ref[3 * sub:3 * sub + 1, :]
    scale = mod_ref[3 * sub + 1:3 * sub + 2, :]
    h = (x * (1.0 + scale) + shift).astype(BF16)
    for c in range(D_FF // FFN_TF):
        cols = slice(c * FFN_TF, (c + 1) * FFN_TF)
        gate = jnp.dot(h, wi_ref[:, cols], preferred_element_type=F32)
        up = jnp.dot(h, wi_ref[:, D_FF + c * FFN_TF:D_FF + (c + 1) * FFN_TF], preferred_element_type=F32)
        act_scr[:, cols] = (_silu(gate) * up).astype(BF16)
    y = jnp.dot(act_scr[...], wo_ref[...], preferred_element_type=F32)
    g = mod_ref[3 * sub + 2:3 * sub + 3, :]
    o_ref[...] = _layer_norm(ALPHA * x + (0.5 * g) * y, lng_ref[...], lnb_ref[...])


def _ffn(x2d, mod_l, w_in, w_out, ln_g, ln_b, *, layer, sub, seq, pre=None, mix=None):
    n_tok, d = x2d.shape
    tm = min(FFN_TM, seq)
    assert seq % tm == 0 and n_tok % seq == 0 and D_FF % FFN_TF == 0
    tiles_per_seq = seq // tm
    row = lambda v: v.reshape(1, d)
    resident = lambda shape: pl.BlockSpec(shape, lambda i: (0, 0), pipeline_mode=pl.Buffered(1))
    weight = lambda shape: pl.BlockSpec((None,) + shape, lambda i: (layer, 0, 0),
                                        pipeline_mode=pl.Buffered(1))
    in_specs = [
        pl.BlockSpec((tm, d), lambda i: (i, 0)),
        pl.BlockSpec((None, 3 * N_SUB, d), lambda i: (i // tiles_per_seq, 0, 0)),
        weight((d, 2 * D_FF)),
        weight((D_FF, d)),
        resident((1, d)),
        resident((1, d)),
    ]
    args = [x2d, mod_l, w_in, w_out, row(ln_g), row(ln_b)]
    if pre is not None:
        in_specs += [resident((1, d))] * 2
        args += [row(pre[0]), row(pre[1])]
    if mix is not None:
        y_ssd, y_fox, y_sc, w_mix, mix_g, mix_b = mix
        tok = lambda width: pl.BlockSpec((tm, width), lambda i: (i, 0))
        in_specs += [tok(SSD_WIDTH), tok(FOX_WIDTH), tok(SCONV_WIDTH), weight((d, d)),
                     resident((1, d)), resident((1, d))]
        args += [y_ssd, y_fox, y_sc, w_mix, row(mix_g), row(mix_b)]
    return pl.pallas_call(
        functools.partial(_ffn_kernel, sub=sub, pre_ln=pre is not None, mix=mix is not None),
        grid=(n_tok // tm,),
        in_specs=in_specs,
        out_specs=pl.BlockSpec((tm, d), lambda i: (i, 0)),
        out_shape=jax.ShapeDtypeStruct((n_tok, d), F32),
        scratch_shapes=[pltpu.VMEM((tm, D_FF), BF16)],
        compiler_params=_params("arbitrary"),
    )(*args)


def _proj_kernel(x_ref, mod_ref, w_ref, cw_ref, cb_ref, sbias_ref, alog_ref, scw_ref, fbias_ref, tri_ref,
                 z_ref, xs_ref, bm_ref, cm_ref, q_ref, k_ref, v_ref, ysc_ref, small_ref,
                 xbc_scr, u_scr, carry_scr, *, tm):
    t = pl.program_id(1)
    halo = SUBLANES

    @pl.when(t == 0)
    def _():
        xbc_scr[0:halo, :] = jnp.zeros((halo, SSD_CONV_DIM), F32)
        u_scr[0:halo, :] = jnp.zeros((halo, SCONV_WIDTH), F32)
        carry_scr[...] = jnp.zeros_like(carry_scr)

    @pl.when(t > 0)
    def _():
        xbc_scr[0:halo, :] = xbc_scr[tm:tm + halo, :]
        u_scr[0:halo, :] = u_scr[tm:tm + halo, :]

    shift = mod_ref[3:4, :]
    scale = mod_ref[4:5, :]
    h = (x_ref[0] * (1.0 + scale) + shift).astype(BF16)

    def seg(start, width):
        return jnp.dot(h, w_ref[:, start:start + width], preferred_element_type=F32)

    z_ref[0] = seg(COL_Z, SSD_WIDTH)

    xbc_scr[halo:halo + tm, :] = seg(COL_XBC, SSD_CONV_DIM)
    conv = cb_ref[...] + cw_ref[SSD_CONV - 1:SSD_CONV, :] * xbc_scr[halo:halo + tm, :]
    for k in range(SSD_CONV - 1):
        back = SSD_CONV - 1 - k
        conv = conv + cw_ref[k:k + 1, :] * xbc_scr[halo - back:halo - back + tm, :]
    xbc = _silu(conv)
    xs_ref[0] = xbc[:, :SSD_WIDTH]
    bm_ref[0] = xbc[:, SSD_WIDTH:SSD_WIDTH + SSD_BC].astype(BF16)
    cm_ref[0] = xbc[:, SSD_WIDTH + SSD_BC:].astype(BF16)

    lane = lax.broadcasted_iota(jnp.int32, (tm, LANES), 1)
    pos = lane % FOX_HEAD_DIM
    used = pos < (FOX_HEADS // HEADS_PER_VREG) * FOX_SLOT
    log_f = jnp.where(used, -_softplus(-(seg(COL_F, LANES) + fbias_ref[...])), 0.0)
    cum_f = carry_scr[0:1, :]
    for part in _split_bf16(log_f, 3):
        cum_f = cum_f + jnp.dot(tri_ref[...], part, preferred_element_type=F32)
    carry_scr[...] = jnp.broadcast_to(cum_f[tm - 1:tm, :], carry_scr.shape)
    hi, mid, lo = (part.astype(F32) for part in _split_bf16(cum_f * LOG2_E, FOX_PIECES))
    piece = jnp.where(pos % FOX_PIECES == 0, hi, jnp.where(pos % FOX_PIECES == 1, mid, lo))
    in_a = pos % FOX_SLOT < FOX_PIECES
    f_q = jnp.where(in_a, piece, 1.0)
    f_k = jnp.where(in_a, 1.0, -piece)

    qkv = seg(COL_QKV, 3 * FOX_WIDTH)
    q = qkv[:, :FOX_WIDTH] * (FOX_HEAD_DIM ** -0.5 * LOG2_E)
    k = qkv[:, FOX_WIDTH:2 * FOX_WIDTH]
    v = qkv[:, 2 * FOX_WIDTH:]
    for hd in range(FOX_HEADS):
        pair, hh = divmod(hd, HEADS_PER_VREG)
        ps = slice(pair * LANES, (pair + 1) * LANES)
        gs = slice(hd * LANES, (hd + 1) * LANES)
        mine = (lane // FOX_HEAD_DIM) == hh
        first = _forget_lane(hd)
        forget = (lane >= first) & (lane < first + FOX_SLOT)
        ones_col = jnp.where(lane == (1 - hh) * FOX_HEAD_DIM, 1.0, 0.0)
        q_ref[0, :, gs] = jnp.where(mine, q[:, ps], jnp.where(forget, f_q, 0.0)).astype(BF16)
        k_ref[0, :, gs] = jnp.where(mine, k[:, ps], jnp.where(forget, f_k, 0.0)).astype(BF16)
        v_ref[0, :, gs] = jnp.where(mine, v[:, ps], ones_col).astype(BF16)

    sc = seg(COL_SC, 3 * SCONV_WIDTH)
    u_scr[halo:halo + tm, :] = sc[:, SCONV_WIDTH:2 * SCONV_WIDTH] * sc[:, 2 * SCONV_WIDTH:]
    cu = scw_ref[SCONV_K - 1:SCONV_K, :] * u_scr[halo:halo + tm, :]
    for k in range(SCONV_K - 1):
        back = SCONV_K - 1 - k
        cu = cu + scw_ref[k:k + 1, :] * u_scr[halo - back:halo - back + tm, :]
    ysc_ref[0] = (sc[:, :SCONV_WIDTH] * cu).astype(BF16)

    dt = _softplus(seg(COL_SMALL, LANES) + sbias_ref[...])
    a_neg = -jnp.exp(alog_ref[...])
    small_ref[0] = jnp.where(lane < LANE_A, dt, dt * a_neg)


def _proj(x, mod_l, w_cat, conv_w, conv_b, small_bias, alog_vec, sconv_w, f_bias, *, layer):
    b, seq, d = x.shape
    tm = min(PROJ_TM, seq)
    assert seq % tm == 0
    nt = seq // tm
    tok = lambda width: pl.BlockSpec((1, tm, width), lambda bi, t: (bi, t, 0))
    const2 = lambda shape: pl.BlockSpec(shape, lambda bi, t: (0, 0))
    out_shapes = (
        jax.ShapeDtypeStruct((b, seq, SSD_WIDTH), F32),
        jax.ShapeDtypeStruct((b, seq, SSD_WIDTH), F32),
        jax.ShapeDtypeStruct((b, seq, SSD_BC), BF16),
        jax.ShapeDtypeStruct((b, seq, SSD_BC), BF16),
        jax.ShapeDtypeStruct((b, seq, FOX_GROUP), BF16),
        jax.ShapeDtypeStruct((b, seq, FOX_GROUP), BF16),
        jax.ShapeDtypeStruct((b, seq, FOX_GROUP), BF16),
        jax.ShapeDtypeStruct((b, seq, SCONV_WIDTH), BF16),
        jax.ShapeDtypeStruct((b, seq, LANES), F32),
    )
    out_specs = (
        tok(SSD_WIDTH), tok(SSD_WIDTH), tok(SSD_BC), tok(SSD_BC),
        tok(FOX_GROUP), tok(FOX_GROUP), tok(FOX_GROUP), tok(SCONV_WIDTH), tok(LANES),
    )
    return pl.pallas_call(
        functools.partial(_proj_kernel, tm=tm),
        grid=(b, nt),
        in_specs=[
            tok(d),
            pl.BlockSpec((None, 3 * N_SUB, d), lambda bi, t: (bi, 0, 0)),
            pl.BlockSpec((None, d, D_PROJ), lambda bi, t: (layer, 0, 0)),
            const2((SSD_CONV, SSD_CONV_DIM)),
            const2((1, SSD_CONV_DIM)),
            const2((1, LANES)),
            const2((1, LANES)),
            const2((SCONV_K, SCONV_WIDTH)),
            const2((1, LANES)),
            const2((tm, tm)),
        ],
        out_specs=out_specs,
        out_shape=out_shapes,
        scratch_shapes=[
            pltpu.VMEM((tm + SUBLANES, SSD_CONV_DIM), F32),
            pltpu.VMEM((tm + SUBLANES, SCONV_WIDTH), F32),
            pltpu.VMEM((SUBLANES, LANES), F32),
        ],
        compiler_params=_params("arbitrary", "arbitrary"),
    )(x, mod_l, w_cat, conv_w, conv_b, small_bias, alog_vec, sconv_w, f_bias, _lower_ones(tm))


def _ssd_kernel(xs_ref, bm_ref, cm_ref, z_ref, small_ref, dskip_ref, ng_ref, tri_ref, spread_ref,
                o_ref, state_scr, *, t, chunks):
    @pl.when(pl.program_id(1) == 0)
    def _():
        state_scr[...] = jnp.zeros_like(state_scr)

    lane = lax.broadcasted_iota(jnp.int32, (t, LANES), 1)
    row_i = lax.broadcasted_iota(jnp.int32, (t, t), 0)
    col_i = lax.broadcasted_iota(jnp.int32, (t, t), 1)
    lower = row_i >= col_i
    head_of_lane = lax.broadcasted_iota(jnp.int32, (t, GROUP_WIDTH), 1) // SSD_HEAD_DIM

    for ci in range(chunks):
        rows = slice(ci * t, (ci + 1) * t)
        small = small_ref[0, rows, :]
        a_only = jnp.where((lane >= LANE_A) & (lane < LANE_A2 + SSD_HEADS), small, 0.0)
        cs = jnp.zeros((t, LANES), F32)
        for part in _split_bf16(a_only, 3):
            cs = cs + jnp.dot(tri_ref[...], part, preferred_element_type=F32)
        cs_t = cs.T
        total = cs[t - 1:t, :]
        per_head = jnp.where(lane < LANE_A, small,
                             jnp.where(lane < LANE_A2, jnp.exp(cs), jnp.exp(total - cs)))
        wide = jnp.zeros((t, 3 * SSD_WIDTH), F32)
        for part in _split_bf16(per_head, 2):
            wide = wide + jnp.dot(part, spread_ref[...], preferred_element_type=F32)
        dt_x = wide[:, :SSD_WIDTH]
        decay_in = wide[:, SSD_WIDTH:2 * SSD_WIDTH]
        decay_out = wide[:, 2 * SSD_WIDTH:]
        decay_chunk = decay_in[t - 1:t, :]

        xs = xs_ref[0, rows, :]
        xdt = xs * dt_x
        ys = []
        for g in range(SSD_GROUPS):
            gs = slice(g * GROUP_WIDTH, (g + 1) * GROUP_WIDTH)
            ns = slice(g * SSD_STATE, (g + 1) * SSD_STATE)
            bg = bm_ref[0, rows, ns]
            cg = cm_ref[0, rows, ns]
            cb = lax.dot_general(cg, bg, (((1,), (1,)), ((), ())), preferred_element_type=F32)
            xg = xdt[:, gs]
            m_parts, x_parts = [], []
            for e in range(SSD_HPG):
                hd = g * SSD_HPG + e
                col = cs[:, LANE_A + hd:LANE_A + hd + 1]
                row = cs_t[LANE_A + hd:LANE_A + hd + 1, :]
                l_mat = jnp.exp(jnp.where(lower, col - row, -jnp.inf))
                m_parts.append((cb * l_mat).astype(BF16))
                x_parts.append(jnp.where(head_of_lane == e, xg, 0.0).astype(BF16))
            m_cat = jnp.concatenate(m_parts, axis=1)
            x_blk = jnp.concatenate(x_parts, axis=0)
            y_diag = jnp.dot(m_cat, x_blk, preferred_element_type=F32)

            state = state_scr[g]
            y_off = jnp.dot(cg, state.astype(BF16), preferred_element_type=F32) * decay_in[:, gs]
            xd = (xg * decay_out[:, gs]).astype(BF16)
            upd = lax.dot_general(bg, xd, (((0,), (0,)), ((), ())), preferred_element_type=F32)
            state_scr[g] = state * decay_chunk[:, gs] + upd
            ys.append(y_diag + y_off + dskip_ref[:, gs] * xs[:, gs])

        z = z_ref[0, rows, :]
        for g in range(SSD_GROUPS):
            gs = slice(g * GROUP_WIDTH, (g + 1) * GROUP_WIDTH)
            yg = ys[g] * _silu(z[:, gs])
            yg = yg * lax.rsqrt(jnp.mean(yg * yg, axis=-1, keepdims=True) + RMS_EPS)
            o_ref[0, rows, gs] = (yg * ng_ref[:, gs]).astype(BF16)


def _head_spread():
    src = jnp.arange(LANES)[:, None]
    col = jnp.arange(3 * SSD_WIDTH)[None, :]
    first_lane = jnp.array([LANE_DT, LANE_A, LANE_A2])[col // SSD_WIDTH]
    return (src == first_lane + (col % SSD_WIDTH) // SSD_HEAD_DIM).astype(BF16)


def _ssd(xs, bm, cm, z, small, d_skip_x, norm_g):
    b, seq, _ = xs.shape
    t = min(SSD_T, seq)
    rows = min(SSD_ROWS, seq)
    assert seq % rows == 0 and rows % t == 0
    tok = lambda width: pl.BlockSpec((1, rows, width), lambda bi, c: (bi, c, 0))
    const2 = lambda shape: pl.BlockSpec(shape, lambda bi, c: (0, 0))
    return pl.pallas_call(
        functools.partial(_ssd_kernel, t=t, chunks=rows // t),
        grid=(b, seq // rows),
        in_specs=[tok(SSD_WIDTH), tok(SSD_BC), tok(SSD_BC), tok(SSD_WIDTH), tok(LANES),
                  const2((1, SSD_WIDTH)), const2((1, SSD_WIDTH)),
                  const2((t, t)), const2((LANES, 3 * SSD_WIDTH))],
        out_specs=tok(SSD_WIDTH),
        out_shape=jax.ShapeDtypeStruct((b, seq, SSD_WIDTH), BF16),
        scratch_shapes=[pltpu.VMEM((SSD_GROUPS, SSD_STATE, GROUP_WIDTH), F32)],
        compiler_params=_params("arbitrary", "arbitrary"),
    )(xs, bm, cm, z, small, d_skip_x, norm_g, _lower_ones(t), _head_spread())


def _fox_kernel(q_ref, k_ref, v_ref, o_ref, m_scr, acc_scr, *, t):
    qi = pl.program_id(1)
    lane_head = lax.broadcasted_iota(jnp.int32, (t, LANES), 1) // FOX_HEAD_DIM
    row_i = lax.broadcasted_iota(jnp.int32, (t, t), 0)
    col_i = lax.broadcasted_iota(jnp.int32, (t, t), 1)
    causal = row_i >= col_i
    heads_per_vreg = LANES // FOX_HEAD_DIM

    def block(start, width, masked):
        ks = pl.ds(start, width)
        group = lambda hd: slice(hd * LANES, (hd + 1) * LANES)

        def logits(hd):
            s = lax.dot_general(q_ref[0, :, group(hd)], k_ref[0, ks, group(hd)],
                                (((1,), (1,)), ((), ())), preferred_element_type=F32)
            return jnp.where(causal, s, -jnp.inf) if masked else s

        def update(hd, s):
            m_prev = m_scr[hd]
            m_new = jnp.maximum(m_prev, jnp.max(s, axis=-1, keepdims=True))
            p = jnp.exp2(s - jnp.concatenate([m_new] * (width // LANES), axis=1)).astype(BF16)
            acc_scr[hd] = jnp.exp2(m_prev - m_new) * acc_scr[hd] + jnp.dot(
                p, v_ref[0, ks, group(hd)], preferred_element_type=F32)
            m_scr[hd] = m_new

        s_next = logits(0)
        for hd in range(FOX_HEADS):
            s_cur = s_next
            if hd + 1 < FOX_HEADS:
                s_next = logits(hd + 1)
            update(hd, s_cur)

    m_scr[...] = jnp.full(m_scr.shape, -jnp.inf, F32)
    acc_scr[...] = jnp.zeros(acc_scr.shape, F32)

    wide = 2 * t
    assert k_ref.shape[1] % wide == 0

    @pl.loop(0, qi // 2)
    def _(pj):
        block(pl.multiple_of(pj * wide, wide), wide, False)

    @pl.when(qi % 2 == 1)
    def _():
        block(pl.multiple_of((qi - 1) * t, t), t, False)

    block(pl.multiple_of(qi * t, t), t, True)
    for pair in range(FOX_HEADS // heads_per_vreg):
        out = jnp.zeros((t, LANES), F32)
        for hh in range(heads_per_vreg):
            acc = acc_scr[pair * heads_per_vreg + hh]
            ones_lane = (1 - hh) * FOX_HEAD_DIM
            row_sum = acc[:, ones_lane:ones_lane + 1]
            out = out + jnp.where(lane_head == hh, acc, 0.0) / row_sum
        o_ref[0, :, pair * LANES:(pair + 1) * LANES] = out.astype(BF16)


def _fox(qa, ka, va):
    b, seq, _ = qa.shape
    t = min(FOX_T, seq)
    assert seq % t == 0
    return pl.pallas_call(
        functools.partial(_fox_kernel, t=t),
        grid=(b, seq // t),
        in_specs=[
            pl.BlockSpec((1, t, FOX_GROUP), lambda bi, i: (bi, i, 0)),
            pl.BlockSpec((1, seq, FOX_GROUP), lambda bi, i: (bi, 0, 0)),
            pl.BlockSpec((1, seq, FOX_GROUP), lambda bi, i: (bi, 0, 0)),
        ],
        out_specs=pl.BlockSpec((1, t, FOX_WIDTH), lambda bi, i: (bi, i, 0)),
        out_shape=jax.ShapeDtypeStruct((b, seq, FOX_WIDTH), BF16),
        scratch_shapes=[pltpu.VMEM((FOX_HEADS, t, LANES), F32), pltpu.VMEM((FOX_HEADS, t, LANES), F32)],
        compiler_params=_params("arbitrary", "arbitrary"),
    )(qa, ka, va)


def _lower_ones(n):
    return jnp.tril(jnp.ones((n, n), BF16))


def _spread_forget(f):
    lead = f.shape[:-1]
    order = sorted(range(FOX_HEADS), key=_forget_lane)
    blocks, lane = [], 0
    for hd in order:
        blocks.append(jnp.zeros(lead + (_forget_lane(hd) - lane,), f.dtype))
        blocks.append(jnp.broadcast_to(f[..., hd:hd + 1], lead + (FOX_SLOT,)))
        lane = _forget_lane(hd) + FOX_SLOT
    blocks.append(jnp.zeros(lead + (LANES - lane,), f.dtype))
    return jnp.concatenate(blocks, axis=-1)


def _pack_mix_w_in(w):
    ssd_in = SSD_WIDTH + SSD_CONV_DIM + SSD_HEADS
    fox_in = 3 * FOX_WIDTH + FOX_HEADS
    dt_cols = w[..., SSD_WIDTH + SSD_CONV_DIM:ssd_in]
    f_cols = w[..., ssd_in + 3 * FOX_WIDTH:ssd_in + fox_in]
    pad = jnp.zeros(w.shape[:-1] + (LANES - 3 * SSD_HEADS,), w.dtype)
    return jnp.concatenate([
        w[..., :SSD_WIDTH + SSD_CONV_DIM],
        w[..., ssd_in:ssd_in + 3 * FOX_WIDTH],
        w[..., ssd_in + fox_in:],
        dt_cols, dt_cols, dt_cols, pad,
        _spread_forget(f_cols),
    ], axis=-1).astype(BF16)


def _lane_vec(pieces):
    row = jnp.zeros((LANES,), F32)
    for off, vec in pieces:
        row = row.at[off:off + vec.shape[0]].set(vec.astype(F32))
    return row.reshape(1, LANES)


def kernel(x, c, ln_in_g, ln_in_b, ada_w, ada_b, ffn1_w_in, ffn1_w_out, mix_w_in, mix_w_out, ssd_conv_w, ssd_conv_b, ssd_dt_bias, ssd_a_log, ssd_d, ssd_norm_g, fox_f_bias, sconv_w, ffn2_w_in, ffn2_w_out, ln_g, ln_b):
    b, seq, d = x.shape
    depth = ada_w.shape[0]
    rows = -(-b // SUBLANES) * SUBLANES
    c_pad = jnp.pad(c, ((0, rows - b), (0, 0)))
    mod = _ada(c_pad, ada_w, ada_b).reshape(depth, rows, 3 * N_SUB, d)

    w1_in, w1_out = ffn1_w_in.astype(BF16), ffn1_w_out.astype(BF16)
    w2_in, w2_out = ffn2_w_in.astype(BF16), ffn2_w_out.astype(BF16)
    wm_in, wm_out = _pack_mix_w_in(mix_w_in), mix_w_out.astype(BF16)

    x2d = x.reshape(b * seq, d)
    for l in range(depth):
        mod_l = mod[l]
        x2d = _ffn(x2d, mod_l, w1_in, w1_out, ln_g[l, 0], ln_b[l, 0], layer=l, sub=0, seq=seq,
                   pre=(ln_in_g, ln_in_b) if l == 0 else None)

        small_bias = _lane_vec([(lane, ssd_dt_bias[l]) for lane in (LANE_DT, LANE_A, LANE_A2)])
        alog_vec = _lane_vec([(LANE_A, ssd_a_log[l]), (LANE_A2, ssd_a_log[l])])
        z, xs, bm, cm, qa, ka, va, y_sc, small = _proj(
            x2d.reshape(b, seq, d), mod_l, wm_in,
            ssd_conv_w[l], ssd_conv_b[l].reshape(1, SSD_CONV_DIM), small_bias, alog_vec, sconv_w[l],
            _spread_forget(fox_f_bias[l]).reshape(1, LANES), layer=l)
        d_skip_x = jnp.repeat(ssd_d[l], SSD_HEAD_DIM).reshape(1, SSD_WIDTH)
        y_ssd = _ssd(xs, bm, cm, z, small, d_skip_x, ssd_norm_g[l].reshape(1, SSD_WIDTH))
        y_fox = _fox(qa, ka, va)
        mixer_tail = (y_ssd.reshape(b * seq, SSD_WIDTH), y_fox.reshape(b * seq, FOX_WIDTH),
                      y_sc.reshape(b * seq, SCONV_WIDTH), wm_out, ln_g[l, 1], ln_b[l, 1])
        x2d = _ffn(x2d, mod_l, w2_in, w2_out, ln_g[l, 2], ln_b[l, 2], layer=l, sub=2, seq=seq,
                   mix=mixer_tail)
    return x2d.reshape(b, seq, d)
```

```python
import functools
import math

import jax
import jax.numpy as jnp
from jax import lax
from jax.experimental import pallas as pl
from jax.experimental.pallas import tpu as pltpu

F32 = jnp.float32
BF16 = jnp.bfloat16
HIGHEST = lax.Precision.HIGHEST

D_MODEL = 1024
DEPTH = 2
N_SUB = 3
D_FF = 2816
SSD_WIDTH = 512
SSD_HEAD_DIM = 64
SSD_HEADS = 8
SSD_GROUPS = 2
SSD_HPG = SSD_HEADS // SSD_GROUPS
SSD_STATE = 128
SSD_CONV = 4
SSD_BC = SSD_GROUPS * SSD_STATE
SSD_CONV_DIM = SSD_WIDTH + 2 * SSD_BC
GROUP_WIDTH = SSD_WIDTH // SSD_GROUPS
FOX_WIDTH = 256
FOX_HEAD_DIM = 64
FOX_HEADS = 4
SCONV_WIDTH = 256
SCONV_K = 3
ALPHA = (2 * DEPTH) ** 0.25
LN_EPS = 1e-5
RMS_EPS = 1e-5
LOG2_E = math.log2(math.e)

LANES = 128
SUBLANES = 8
VMEM_LIMIT = 56 * 1024 * 1024

COL_Z = 0
COL_XBC = COL_Z + SSD_WIDTH
COL_QKV = COL_XBC + SSD_CONV_DIM
COL_SC = COL_QKV + 3 * FOX_WIDTH
COL_SMALL = COL_SC + 3 * SCONV_WIDTH
COL_F = COL_SMALL + LANES
D_PROJ = COL_F + LANES
LANE_DT = 0
LANE_A = 8
LANE_A2 = 16
FOX_PIECES = 3
FOX_SLOT = 2 * FOX_PIECES
FOX_GROUP = FOX_HEADS * LANES
HEADS_PER_VREG = LANES // FOX_HEAD_DIM


def _forget_lane(hd):
    pair, hh = divmod(hd, HEADS_PER_VREG)
    return (HEADS_PER_VREG - 1 - hh) * FOX_HEAD_DIM + pair * FOX_SLOT

FFN_TM = 1024
FFN_TM_MIX = 512
FFN_TF = 256
FFN_HALF = 512
PROJ_TM = 512
PROJ_CUM = 128
SSD_T = 256
SSD_ROWS = 512
FOX_T = 512
ADA_TN = 1024


def _layer_norm(x, g, b):
    mu = jnp.mean(x, axis=-1, keepdims=True)
    xc = x - mu
    var = jnp.mean(xc * xc, axis=-1, keepdims=True)
    return xc * lax.rsqrt(var + LN_EPS) * g + b


def _silu(x):
    return x / (1.0 + jnp.exp2(x * (-LOG2_E)))


def _softplus(x):
    return jnp.maximum(x, 0.0) + jnp.log1p(jnp.exp(-jnp.abs(x)))


def _split_bf16(x, pieces):
    out = []
    for _ in range(pieces):
        part = x.astype(BF16)
        out.append(part)
        x = x - part.astype(F32)
    return out


def _params(*semantics):
    return pltpu.CompilerParams(dimension_semantics=semantics, vmem_limit_bytes=VMEM_LIMIT)


def _ada_kernel(c_ref, w_ref, b_ref, o_ref):
    o_ref[...] = jnp.dot(_silu(c_ref[...]), w_ref[...], preferred_element_type=F32,
                         precision=HIGHEST) + b_ref[...]


def _ada(c_pad, ada_w, ada_b):
    depth, d, n = ada_w.shape
    rows = c_pad.shape[0]
    return pl.pallas_call(
        _ada_kernel,
        grid=(depth, n // ADA_TN),
        in_specs=[
            pl.BlockSpec((rows, d), lambda l, j: (0, 0)),
            pl.BlockSpec((None, d, ADA_TN), lambda l, j: (l, 0, j)),
            pl.BlockSpec((None, 1, ADA_TN), lambda l, j: (l, 0, j)),
        ],
        out_specs=pl.BlockSpec((None, rows, ADA_TN), lambda l, j: (l, 0, j)),
        out_shape=jax.ShapeDtypeStruct((depth, rows, n), F32),
        compiler_params=_params("arbitrary", "arbitrary"),
    )(c_pad, ada_w, ada_b.reshape(depth, 1, n))


def _ffn_kernel(*refs, layer, sub, pre_ln, mix):
    x_ref, mod_ref, wi_hbm, wo_hbm, lng_ref, lnb_ref = refs[:6]
    n_scratch = 7
    extra = list(refs[6:-1 - n_scratch])
    o_ref = refs[-1 - n_scratch]
    act_scr, wi_ref, wo_ref, stage_g, stage_u, stage_o, sem = refs[-n_scratch:]
    if pre_ln:
        ing_ref, inb_ref = extra[:2]
        extra = extra[2:]
    if mix:
        yssd_ref, yfox_ref, ysc_ref, wm_ref, mlng_ref, mlnb_ref = extra
    shift = mod_ref[3 * sub:3 * sub + 1, :]
    scale = mod_ref[3 * sub + 1:3 * sub + 2, :]
    gain = mod_ref[3 * sub + 2:3 * sub + 3, :]
    halves = [slice(r, r + FFN_HALF) for r in range(0, x_ref.shape[0], FFN_HALF)]

    def entry(rows):
        x = x_ref[rows, :]
        if pre_ln:
            x = _layer_norm(x, ing_ref[...], inb_ref[...])
        if mix:
            y = jnp.dot(yssd_ref[rows, :], wm_ref[0:SSD_WIDTH, :], preferred_element_type=F32)
            y = y + jnp.dot(yfox_ref[rows, :], wm_ref[SSD_WIDTH:SSD_WIDTH + FOX_WIDTH, :],
                            preferred_element_type=F32)
            y = y + jnp.dot(ysc_ref[rows, :], wm_ref[SSD_WIDTH + FOX_WIDTH:, :],
                            preferred_element_type=F32)
            x = _layer_norm(ALPHA * x + mod_ref[5:6, :] * y, mlng_ref[...], mlnb_ref[...])
        return x

    n_chunks = D_FF // FFN_TF
    gate_cols = lambda c: slice(c * FFN_TF, (c + 1) * FFN_TF)
    up_cols = lambda c: slice(D_FF + c * FFN_TF, D_FF + (c + 1) * FFN_TF)

    def weight_copies(c):
        slot = c % 2
        return (
            pltpu.make_async_copy(wi_hbm.at[layer, :, gate_cols(c)], stage_g.at[slot], sem.at[slot, 0]),
            pltpu.make_async_copy(wi_hbm.at[layer, :, up_cols(c)], stage_u.at[slot], sem.at[slot, 1]),
            pltpu.make_async_copy(wo_hbm.at[layer, gate_cols(c), :], stage_o.at[slot], sem.at[slot, 2]),
        )

    def tile(load_weights):
        if load_weights:
            for copy in weight_copies(0):
                copy.start()
        xs = [entry(rows) for rows in halves]
        ys = []
        for r, (rows, x) in enumerate(zip(halves, xs)):
            h = (x * (1.0 + scale) + shift).astype(BF16)
            for c in range(n_chunks):
                if load_weights and r == 0:
                    if c + 1 < n_chunks:
                        for copy in weight_copies(c + 1):
                            copy.start()
                    for copy in weight_copies(c):
                        copy.wait()
                    slot = c % 2
                    wi_ref[:, gate_cols(c)] = stage_g[slot].astype(BF16)
                    wi_ref[:, up_cols(c)] = stage_u[slot].astype(BF16)
                    wo_ref[gate_cols(c), :] = stage_o[slot].astype(BF16)
                gate = jnp.dot(h, wi_ref[:, gate_cols(c)], preferred_element_type=F32)
                up = jnp.dot(h, wi_ref[:, up_cols(c)], preferred_element_type=F32)
                act_scr[rows, gate_cols(c)] = (_silu(gate) * up).astype(BF16)
            ys.append(jnp.dot(act_scr[rows, :], wo_ref[...], preferred_element_type=F32))
        for rows, x, y in zip(halves, xs, ys):
            o_ref[rows, :] = _layer_norm(ALPHA * x + (0.5 * gain) * y, lng_ref[...], lnb_ref[...])

    first = pl.program_id(0) == 0
    pl.when(first)(lambda: tile(True))
    pl.when(jnp.logical_not(first))(lambda: tile(False))


def _ffn(x2d, mod_l, w_in, w_out, ln_g, ln_b, *, layer, sub, seq, pre=None, mix=None):
    n_tok, d = x2d.shape
    tm = min(FFN_TM if mix is None else FFN_TM_MIX, seq)
    assert seq % tm == 0 and n_tok % seq == 0 and D_FF % FFN_TF == 0 and tm % FFN_HALF == 0
    tiles_per_seq = seq // tm
    row = lambda v: v.reshape(1, d)
    resident = lambda shape: pl.BlockSpec(shape, lambda i: (0, 0), pipeline_mode=pl.Buffered(1))
    weight = lambda shape: pl.BlockSpec((None,) + shape, lambda i: (layer, 0, 0),
                                        pipeline_mode=pl.Buffered(1))
    in_specs = [
        pl.BlockSpec((tm, d), lambda i: (i, 0)),
        pl.BlockSpec((None, 3 * N_SUB, d), lambda i: (i // tiles_per_seq, 0, 0)),
        pl.BlockSpec(memory_space=pl.ANY),
        pl.BlockSpec(memory_space=pl.ANY),
        resident((1, d)),
        resident((1, d)),
    ]
    args = [x2d, mod_l, w_in, w_out, row(ln_g), row(ln_b)]
    if pre is not None:
        in_specs += [resident((1, d))] * 2
        args += [row(pre[0]), row(pre[1])]
    if mix is not None:
        y_ssd, y_fox, y_sc, w_mix, mix_g, mix_b = mix
        tok = lambda width: pl.BlockSpec((tm, width), lambda i: (i, 0))
        in_specs += [tok(SSD_WIDTH), tok(FOX_WIDTH), tok(SCONV_WIDTH), weight((d, d)),
                     resident((1, d)), resident((1, d))]
        args += [y_ssd, y_fox, y_sc, w_mix, row(mix_g), row(mix_b)]
    return pl.pallas_call(
        functools.partial(_ffn_kernel, layer=layer, sub=sub, pre_ln=pre is not None, mix=mix is not None),
        grid=(n_tok // tm,),
        in_specs=in_specs,
        out_specs=pl.BlockSpec((tm, d), lambda i: (i, 0)),
        out_shape=jax.ShapeDtypeStruct((n_tok, d), F32),
        scratch_shapes=[
            pltpu.VMEM((tm, D_FF), BF16),
            pltpu.VMEM((d, 2 * D_FF), BF16),
            pltpu.VMEM((D_FF, d), BF16),
            pltpu.VMEM((2, d, FFN_TF), F32),
            pltpu.VMEM((2, d, FFN_TF), F32),
            pltpu.VMEM((2, FFN_TF, d), F32),
            pltpu.SemaphoreType.DMA((2, 3)),
        ],
        compiler_params=_params("arbitrary"),
    )(*args)


def _proj_kernel(x_ref, mod_ref, w_ref, cw_ref, cb_ref, sbias_ref, alog_ref, scw_ref, fbias_ref, tri_ref,
                 z_ref, xs_ref, bm_ref, cm_ref, q_ref, k_ref, v_ref, ysc_ref, small_ref,
                 xbc_scr, u_scr, carry_scr, *, tm):
    t = pl.program_id(1)
    halo = SUBLANES

    @pl.when(t == 0)
    def _():
        xbc_scr[0:halo, :] = jnp.zeros((halo, SSD_CONV_DIM), F32)
        u_scr[0:halo, :] = jnp.zeros((halo, SCONV_WIDTH), F32)
        carry_scr[...] = jnp.zeros_like(carry_scr)

    @pl.when(t > 0)
    def _():
        xbc_scr[0:halo, :] = xbc_scr[tm:tm + halo, :]
        u_scr[0:halo, :] = u_scr[tm:tm + halo, :]

    shift = mod_ref[3:4, :]
    scale = mod_ref[4:5, :]
    h = (x_ref[0] * (1.0 + scale) + shift).astype(BF16)

    def seg(start, width):
        return jnp.dot(h, w_ref[:, start:start + width], preferred_element_type=F32)

    xbc_scr[halo:halo + tm, :] = seg(COL_XBC, SSD_CONV_DIM)
    conv = cb_ref[...] + cw_ref[SSD_CONV - 1:SSD_CONV, :] * xbc_scr[halo:halo + tm, :]
    for k in range(SSD_CONV - 1):
        back = SSD_CONV - 1 - k
        conv = conv + cw_ref[k:k + 1, :] * xbc_scr[halo - back:halo - back + tm, :]
    xbc = _silu(conv)
    xs_ref[0] = xbc[:, :SSD_WIDTH]
    bm_ref[0] = xbc[:, SSD_WIDTH:SSD_WIDTH + SSD_BC].astype(BF16)
    cm_ref[0] = xbc[:, SSD_WIDTH + SSD_BC:].astype(BF16)

    sc = seg(COL_SC, 3 * SCONV_WIDTH)
    u_scr[halo:halo + tm, :] = sc[:, SCONV_WIDTH:2 * SCONV_WIDTH] * sc[:, 2 * SCONV_WIDTH:]
    cu = scw_ref[SCONV_K - 1:SCONV_K, :] * u_scr[halo:halo + tm, :]
    for k in range(SCONV_K - 1):
        back = SCONV_K - 1 - k
        cu = cu + scw_ref[k:k + 1, :] * u_scr[halo - back:halo - back + tm, :]
    ysc_ref[0] = (sc[:, :SCONV_WIDTH] * cu).astype(BF16)

    lane = lax.broadcasted_iota(jnp.int32, (tm, LANES), 1)
    pos = lane % FOX_HEAD_DIM
    used = pos < (FOX_HEADS // HEADS_PER_VREG) * FOX_SLOT
    small_f = seg(COL_SMALL, 2 * LANES)
    log_f = jnp.where(used, -_softplus(-(small_f[:, LANES:] + fbias_ref[...])), 0.0)
    parts = _split_bf16(log_f, 3)
    offset = carry_scr[0:1, :]
    blocks = []
    for r in range(0, tm, PROJ_CUM):
        local = jnp.dot(tri_ref[...], parts[0][r:r + PROJ_CUM, :], preferred_element_type=F32)
        for part in parts[1:]:
            local = local + jnp.dot(tri_ref[...], part[r:r + PROJ_CUM, :], preferred_element_type=F32)
        blocks.append(local + offset)
        offset = offset + local[PROJ_CUM - 1:PROJ_CUM, :]
    cum_f = jnp.concatenate(blocks, axis=0)
    carry_scr[...] = jnp.broadcast_to(offset, carry_scr.shape)
    hi, mid, lo = (part.astype(F32) for part in _split_bf16(cum_f * LOG2_E, FOX_PIECES))
    piece = jnp.where(pos % FOX_PIECES == 0, hi, jnp.where(pos % FOX_PIECES == 1, mid, lo))
    in_a = pos % FOX_SLOT < FOX_PIECES
    f_q = jnp.where(in_a, piece, 1.0)
    f_k = jnp.where(in_a, 1.0, -piece)

    qkv = seg(COL_QKV, 3 * FOX_WIDTH)
    q = qkv[:, :FOX_WIDTH] * (FOX_HEAD_DIM ** -0.5 * LOG2_E)
    k = qkv[:, FOX_WIDTH:2 * FOX_WIDTH]
    v = qkv[:, 2 * FOX_WIDTH:]
    for hd in range(FOX_HEADS):
        pair, hh = divmod(hd, HEADS_PER_VREG)
        ps = slice(pair * LANES, (pair + 1) * LANES)
        gs = slice(hd * LANES, (hd + 1) * LANES)
        mine = (lane // FOX_HEAD_DIM) == hh
        first = _forget_lane(hd)
        forget = (lane >= first) & (lane < first + FOX_SLOT)
        ones_col = jnp.where(lane == (1 - hh) * FOX_HEAD_DIM, 1.0, 0.0)
        q_ref[0, :, gs] = jnp.where(mine, q[:, ps], jnp.where(forget, f_q, 0.0)).astype(BF16)
        k_ref[0, :, gs] = jnp.where(mine, k[:, ps], jnp.where(forget, f_k, 0.0)).astype(BF16)
        v_ref[0, :, gs] = jnp.where(mine, v[:, ps], ones_col).astype(BF16)

    dt = _softplus(small_f[:, :LANES] + sbias_ref[...])
    a_neg = -jnp.exp(alog_ref[...])
    small_ref[0] = jnp.where(lane < LANE_A, dt, dt * a_neg)

    z_ref[0] = seg(COL_Z, SSD_WIDTH)


def _proj(x, mod_l, w_cat, conv_w, conv_b, small_bias, alog_vec, sconv_w, f_bias, *, layer):
    b, seq, d = x.shape
    tm = min(PROJ_TM, seq)
    assert seq % tm == 0
    nt = seq // tm
    tok = lambda width: pl.BlockSpec((1, tm, width), lambda bi, t: (bi, t, 0))
    const2 = lambda shape: pl.BlockSpec(shape, lambda bi, t: (0, 0))
    out_shapes = (
        jax.ShapeDtypeStruct((b, seq, SSD_WIDTH), F32),
        jax.ShapeDtypeStruct((b, seq, SSD_WIDTH), F32),
        jax.ShapeDtypeStruct((b, seq, SSD_BC), BF16),
        jax.ShapeDtypeStruct((b, seq, SSD_BC), BF16),
        jax.ShapeDtypeStruct((b, seq, FOX_GROUP), BF16),
        jax.ShapeDtypeStruct((b, seq, FOX_GROUP), BF16),
        jax.ShapeDtypeStruct((b, seq, FOX_GROUP), BF16),
        jax.ShapeDtypeStruct((b, seq, SCONV_WIDTH), BF16),
        jax.ShapeDtypeStruct((b, seq, LANES), F32),
    )
    out_specs = (
        tok(SSD_WIDTH), tok(SSD_WIDTH), tok(SSD_BC), tok(SSD_BC),
        tok(FOX_GROUP), tok(FOX_GROUP), tok(FOX_GROUP), tok(SCONV_WIDTH), tok(LANES),
    )
    return pl.pallas_call(
        functools.partial(_proj_kernel, tm=tm),
        grid=(b, nt),
        in_specs=[
            tok(d),
            pl.BlockSpec((None, 3 * N_SUB, d), lambda bi, t: (bi, 0, 0)),
            pl.BlockSpec((None, d, D_PROJ), lambda bi, t: (layer, 0, 0)),
            const2((SSD_CONV, SSD_CONV_DIM)),
            const2((1, SSD_CONV_DIM)),
            const2((1, LANES)),
            const2((1, LANES)),
            const2((SCONV_K, SCONV_WIDTH)),
            const2((1, LANES)),
            const2((PROJ_CUM, PROJ_CUM)),
        ],
        out_specs=out_specs,
        out_shape=out_shapes,
        scratch_shapes=[
            pltpu.VMEM((tm + SUBLANES, SSD_CONV_DIM), F32),
            pltpu.VMEM((tm + SUBLANES, SCONV_WIDTH), F32),
            pltpu.VMEM((SUBLANES, LANES), F32),
        ],
        compiler_params=_params("arbitrary", "arbitrary"),
    )(x, mod_l, w_cat, conv_w, conv_b, small_bias, alog_vec, sconv_w, f_bias, _lower_ones(PROJ_CUM))


def _ssd_kernel(xs_ref, bm_ref, cm_ref, z_ref, small_ref, dskip_ref, ng_ref, tri_ref, spread_ref,
                o_ref, state_scr, *, t, chunks):
    @pl.when(pl.program_id(1) == 0)
    def _():
        state_scr[...] = jnp.zeros_like(state_scr)

    lane = lax.broadcasted_iota(jnp.int32, (t, LANES), 1)
    row_i = lax.broadcasted_iota(jnp.int32, (t, t), 0)
    col_i = lax.broadcasted_iota(jnp.int32, (t, t), 1)
    lower = row_i >= col_i
    head_of_lane = lax.broadcasted_iota(jnp.int32, (t, GROUP_WIDTH), 1) // SSD_HEAD_DIM

    for ci in range(chunks):
        rows = slice(ci * t, (ci + 1) * t)
        small = small_ref[0, rows, :]
        a_only = jnp.where((lane >= LANE_A) & (lane < LANE_A2 + SSD_HEADS), small, 0.0)
        cs = jnp.zeros((t, LANES), F32)
        for part in _split_bf16(a_only, 3):
            cs = cs + jnp.dot(tri_ref[...], part, preferred_element_type=F32)
        cs_t = cs.T
        total = cs[t - 1:t, :]
        per_head = jnp.where(lane < LANE_A, small,
                             jnp.where(lane < LANE_A2, jnp.exp(cs), jnp.exp(total - cs)))
        wide = jnp.zeros((t, 3 * SSD_WIDTH), F32)
        for part in _split_bf16(per_head, 2):
            wide = wide + jnp.dot(part, spread_ref[...], preferred_element_type=F32)
        dt_x = wide[:, :SSD_WIDTH]
        decay_in = wide[:, SSD_WIDTH:2 * SSD_WIDTH]
        decay_out = wide[:, 2 * SSD_WIDTH:]
        decay_chunk = decay_in[t - 1:t, :]

        xs = xs_ref[0, rows, :]
        xdt = xs * dt_x
        ys = []
        for g in range(SSD_GROUPS):
            gs = slice(g * GROUP_WIDTH, (g + 1) * GROUP_WIDTH)
            ns = slice(g * SSD_STATE, (g + 1) * SSD_STATE)
            bg = bm_ref[0, rows, ns]
            cg = cm_ref[0, rows, ns]
            cb = lax.dot_general(cg, bg, (((1,), (1,)), ((), ())), preferred_element_type=F32)
            xg = xdt[:, gs]
            m_parts, x_parts = [], []
            for e in range(SSD_HPG):
                hd = g * SSD_HPG + e
                col = cs[:, LANE_A + hd:LANE_A + hd + 1]
                row = cs_t[LANE_A + hd:LANE_A + hd + 1, :]
                l_mat = jnp.exp(jnp.where(lower, col - row, -jnp.inf))
                m_parts.append((cb * l_mat).astype(BF16))
                x_parts.append(jnp.where(head_of_lane == e, xg, 0.0).astype(BF16))
            m_cat = jnp.concatenate(m_parts, axis=1)
            x_blk = jnp.concatenate(x_parts, axis=0)
            y_diag = jnp.dot(m_cat, x_blk, preferred_element_type=F32)

            state = state_scr[g]
            y_off = jnp.dot(cg, state.astype(BF16), preferred_element_type=F32) * decay_in[:, gs]
            xd = (xg * decay_out[:, gs]).astype(BF16)
            upd = lax.dot_general(bg, xd, (((0,), (0,)), ((), ())), preferred_element_type=F32)
            state_scr[g] = state * decay_chunk[:, gs] + upd
            ys.append(y_diag + y_off + dskip_ref[:, gs] * xs[:, gs])

        z = z_ref[0, rows, :]
        for g in range(SSD_GROUPS):
            gs = slice(g * GROUP_WIDTH, (g + 1) * GROUP_WIDTH)
            yg = ys[g] * _silu(z[:, gs])
            yg = yg * lax.rsqrt(jnp.mean(yg * yg, axis=-1, keepdims=True) + RMS_EPS)
            o_ref[0, rows, gs] = (yg * ng_ref[:, gs]).astype(BF16)


def _head_spread():
    src = jnp.arange(LANES)[:, None]
    col = jnp.arange(3 * SSD_WIDTH)[None, :]
    first_lane = jnp.array([LANE_DT, LANE_A, LANE_A2])[col // SSD_WIDTH]
    return (src == first_lane + (col % SSD_WIDTH) // SSD_HEAD_DIM).astype(BF16)


def _ssd(xs, bm, cm, z, small, d_skip_x, norm_g):
    b, seq, _ = xs.shape
    t = min(SSD_T, seq)
    rows = min(SSD_ROWS, seq)
    assert seq % rows == 0 and rows % t == 0
    tok = lambda width: pl.BlockSpec((1, rows, width), lambda bi, c: (bi, c, 0))
    const2 = lambda shape: pl.BlockSpec(shape, lambda bi, c: (0, 0))
    return pl.pallas_call(
        functools.partial(_ssd_kernel, t=t, chunks=rows // t),
        grid=(b, seq // rows),
        in_specs=[tok(SSD_WIDTH), tok(SSD_BC), tok(SSD_BC), tok(SSD_WIDTH), tok(LANES),
                  const2((1, SSD_WIDTH)), const2((1, SSD_WIDTH)),
                  const2((t, t)), const2((LANES, 3 * SSD_WIDTH))],
        out_specs=tok(SSD_WIDTH),
        out_shape=jax.ShapeDtypeStruct((b, seq, SSD_WIDTH), BF16),
        scratch_shapes=[pltpu.VMEM((SSD_GROUPS, SSD_STATE, GROUP_WIDTH), F32)],
        compiler_params=_params("arbitrary", "arbitrary"),
    )(xs, bm, cm, z, small, d_skip_x, norm_g, _lower_ones(t), _head_spread())


def _fox_kernel(q_ref, k_ref, v_ref, o_ref, m_scr, acc_scr, *, t):
    qi = pl.program_id(1)
    lane_head = lax.broadcasted_iota(jnp.int32, (t, LANES), 1) // FOX_HEAD_DIM
    row_i = lax.broadcasted_iota(jnp.int32, (t, t), 0)
    col_i = lax.broadcasted_iota(jnp.int32, (t, t), 1)
    causal = row_i >= col_i
    heads_per_vreg = LANES // FOX_HEAD_DIM

    def block(start, width, masked):
        ks = pl.ds(start, width)
        group = lambda hd: slice(hd * LANES, (hd + 1) * LANES)

        def logits(hd):
            s = lax.dot_general(q_ref[0, :, group(hd)], k_ref[0, ks, group(hd)],
                                (((1,), (1,)), ((), ())), preferred_element_type=F32)
            return jnp.where(causal, s, -jnp.inf) if masked else s

        def update(hd, s):
            m_prev = m_scr[hd]
            m_new = jnp.maximum(m_prev, jnp.max(s, axis=-1, keepdims=True))
            p = jnp.exp2(s - jnp.concatenate([m_new] * (width // LANES), axis=1)).astype(BF16)
            acc_scr[hd] = jnp.exp2(m_prev - m_new) * acc_scr[hd] + jnp.dot(
                p, v_ref[0, ks, group(hd)], preferred_element_type=F32)
            m_scr[hd] = m_new

        s_next = logits(0)
        for hd in range(FOX_HEADS):
            s_cur = s_next
            if hd + 1 < FOX_HEADS:
                s_next = logits(hd + 1)
            update(hd, s_cur)

    m_scr[...] = jnp.full(m_scr.shape, -jnp.inf, F32)
    acc_scr[...] = jnp.zeros(acc_scr.shape, F32)

    wide = 2 * t
    assert k_ref.shape[1] % wide == 0

    @pl.loop(0, qi // 2)
    def _(pj):
        block(pl.multiple_of(pj * wide, wide), wide, False)

    @pl.when(qi % 2 == 1)
    def _():
        block(pl.multiple_of((qi - 1) * t, t), t, False)

    block(pl.multiple_of(qi * t, t), t, True)
    for pair in range(FOX_HEADS // heads_per_vreg):
        out = jnp.zeros((t, LANES), F32)
        for hh in range(heads_per_vreg):
            acc = acc_scr[pair * heads_per_vreg + hh]
            ones_lane = (1 - hh) * FOX_HEAD_DIM
            row_sum = acc[:, ones_lane:ones_lane + 1]
            out = out + jnp.where(lane_head == hh, acc, 0.0) / row_sum
        o_ref[0, :, pair * LANES:(pair + 1) * LANES] = out.astype(BF16)


def _fox(qa, ka, va):
    b, seq, _ = qa.shape
    t = min(FOX_T, seq)
    assert seq % t == 0
    return pl.pallas_call(
        functools.partial(_fox_kernel, t=t),
        grid=(b, seq // t),
        in_specs=[
            pl.BlockSpec((1, t, FOX_GROUP), lambda bi, i: (bi, i, 0)),
            pl.BlockSpec((1, seq, FOX_GROUP), lambda bi, i: (bi, 0, 0)),
            pl.BlockSpec((1, seq, FOX_GROUP), lambda bi, i: (bi, 0, 0)),
        ],
        out_specs=pl.BlockSpec((1, t, FOX_WIDTH), lambda bi, i: (bi, i, 0)),
        out_shape=jax.ShapeDtypeStruct((b, seq, FOX_WIDTH), BF16),
        scratch_shapes=[pltpu.VMEM((FOX_HEADS, t, LANES), F32), pltpu.VMEM((FOX_HEADS, t, LANES), F32)],
        compiler_params=_params("arbitrary", "arbitrary"),
    )(qa, ka, va)


def _lower_ones(n):
    return jnp.tril(jnp.ones((n, n), BF16))


def _spread_forget(f):
    lead = f.shape[:-1]
    order = sorted(range(FOX_HEADS), key=_forget_lane)
    blocks, lane = [], 0
    for hd in order:
        blocks.append(jnp.zeros(lead + (_forget_lane(hd) - lane,), f.dtype))
        blocks.append(jnp.broadcast_to(f[..., hd:hd + 1], lead + (FOX_SLOT,)))
        lane = _forget_lane(hd) + FOX_SLOT
    blocks.append(jnp.zeros(lead + (LANES - lane,), f.dtype))
    return jnp.concatenate(blocks, axis=-1)


def _pack_mix_w_in(w):
    ssd_in = SSD_WIDTH + SSD_CONV_DIM + SSD_HEADS
    fox_in = 3 * FOX_WIDTH + FOX_HEADS
    dt_cols = w[..., SSD_WIDTH + SSD_CONV_DIM:ssd_in]
    f_cols = w[..., ssd_in + 3 * FOX_WIDTH:ssd_in + fox_in]
    pad = jnp.zeros(w.shape[:-1] + (LANES - 3 * SSD_HEADS,), w.dtype)
    return jnp.concatenate([
        w[..., :SSD_WIDTH + SSD_CONV_DIM],
        w[..., ssd_in:ssd_in + 3 * FOX_WIDTH],
        w[..., ssd_in + fox_in:],
        dt_cols, dt_cols, dt_cols, pad,
        _spread_forget(f_cols),
    ], axis=-1).astype(BF16)


def _lane_vec(pieces):
    row = jnp.zeros((LANES,), F32)
    for off, vec in pieces:
        row = row.at[off:off + vec.shape[0]].set(vec.astype(F32))
    return row.reshape(1, LANES)


def kernel(x, c, ln_in_g, ln_in_b, ada_w, ada_b, ffn1_w_in, ffn1_w_out, mix_w_in, mix_w_out, ssd_conv_w, ssd_conv_b, ssd_dt_bias, ssd_a_log, ssd_d, ssd_norm_g, fox_f_bias, sconv_w, ffn2_w_in, ffn2_w_out, ln_g, ln_b):
    b, seq, d = x.shape
    depth = ada_w.shape[0]
    rows = -(-b // SUBLANES) * SUBLANES
    c_pad = jnp.pad(c, ((0, rows - b), (0, 0)))
    mod = _ada(c_pad, ada_w, ada_b).reshape(depth, rows, 3 * N_SUB, d)

    w1_in, w1_out, w2_in, w2_out = ffn1_w_in, ffn1_w_out, ffn2_w_in, ffn2_w_out
    wm_in, wm_out = _pack_mix_w_in(mix_w_in), mix_w_out.astype(BF16)

    x2d = x.reshape(b * seq, d)
    for l in range(depth):
        mod_l = mod[l]
        x2d = _ffn(x2d, mod_l, w1_in, w1_out, ln_g[l, 0], ln_b[l, 0], layer=l, sub=0, seq=seq,
                   pre=(ln_in_g, ln_in_b) if l == 0 else None)

        small_bias = _lane_vec([(lane, ssd_dt_bias[l]) for lane in (LANE_DT, LANE_A, LANE_A2)])
        alog_vec = _lane_vec([(LANE_A, ssd_a_log[l]), (LANE_A2, ssd_a_log[l])])
        z, xs, bm, cm, qa, ka, va, y_sc, small = _proj(
            x2d.reshape(b, seq, d), mod_l, wm_in,
            ssd_conv_w[l], ssd_conv_b[l].reshape(1, SSD_CONV_DIM), small_bias, alog_vec, sconv_w[l],
            _spread_forget(fox_f_bias[l]).reshape(1, LANES), layer=l)
        d_skip_x = jnp.repeat(ssd_d[l], SSD_HEAD_DIM).reshape(1, SSD_WIDTH)
        y_ssd = _ssd(xs, bm, cm, z, small, d_skip_x, ssd_norm_g[l].reshape(1, SSD_WIDTH))
        y_fox = _fox(qa, ka, va)
        mixer_tail = (y_ssd.reshape(b * seq, SSD_WIDTH), y_fox.reshape(b * seq, FOX_WIDTH),
                      y_sc.reshape(b * seq, SCONV_WIDTH), wm_out, ln_g[l, 1], ln_b[l, 1])
        x2d = _ffn(x2d, mod_l, w2_in, w2_out, ln_g[l, 2], ln_b[l, 2], layer=l, sub=2, seq=seq,
                   mix=mixer_tail)
    return x2d.reshape(b, seq, d)
```

```python
import functools
import math

import jax
import jax.numpy as jnp
from jax import lax
from jax.experimental import pallas as pl
from jax.experimental.pallas import tpu as pltpu

F32 = jnp.float32
BF16 = jnp.bfloat16
HIGHEST = lax.Precision.HIGHEST

D_MODEL = 1024
DEPTH = 2
N_SUB = 3
D_FF = 2816
SSD_WIDTH = 512
SSD_HEAD_DIM = 64
SSD_HEADS = 8
SSD_GROUPS = 2
SSD_HPG = SSD_HEADS // SSD_GROUPS
SSD_STATE = 128
SSD_CONV = 4
SSD_BC = SSD_GROUPS * SSD_STATE
SSD_CONV_DIM = SSD_WIDTH + 2 * SSD_BC
GROUP_WIDTH = SSD_WIDTH // SSD_GROUPS
FOX_WIDTH = 256
FOX_HEAD_DIM = 64
FOX_HEADS = 4
SCONV_WIDTH = 256
SCONV_K = 3
ALPHA = (2 * DEPTH) ** 0.25
LN_EPS = 1e-5
RMS_EPS = 1e-5
LOG2_E = math.log2(math.e)

LANES = 128
SUBLANES = 8
VMEM_LIMIT = 56 * 1024 * 1024

COL_Z = 0
COL_XBC = COL_Z + SSD_WIDTH
COL_QKV = COL_XBC + SSD_CONV_DIM
COL_SC = COL_QKV + 3 * FOX_WIDTH
COL_SMALL = COL_SC + 3 * SCONV_WIDTH
COL_F = COL_SMALL + LANES
D_PROJ = COL_F + LANES
LANE_DT = 0
LANE_A = 8
LANE_A2 = 16
FOX_PIECES = 3
FOX_SLOT = 2 * FOX_PIECES
FOX_GROUP = FOX_HEADS * LANES
HEADS_PER_VREG = LANES // FOX_HEAD_DIM


def _forget_lane(hd):
    pair, hh = divmod(hd, HEADS_PER_VREG)
    return (HEADS_PER_VREG - 1 - hh) * FOX_HEAD_DIM + pair * FOX_SLOT

FFN_TM = 1024
FFN_TM_MIX = 512
FFN_TF = 256
FFN_HALF = 512
PROJ_TM = 512
PROJ_CUM = 128
SSD_T = 256
SSD_ROWS = 512
FOX_T = 512
ADA_TN = 1024


def _layer_norm(x, g, b):
    mu = jnp.mean(x, axis=-1, keepdims=True)
    xc = x - mu
    var = jnp.mean(xc * xc, axis=-1, keepdims=True)
    return xc * lax.rsqrt(var + LN_EPS) * g + b


def _silu(x):
    return x / (1.0 + jnp.exp2(x * (-LOG2_E)))


def _softplus(x):
    return jnp.maximum(x, 0.0) + jnp.log1p(jnp.exp(-jnp.abs(x)))


def _split_bf16(x, pieces):
    out = []
    for _ in range(pieces):
        part = x.astype(BF16)
        out.append(part)
        x = x - part.astype(F32)
    return out


def _params(*semantics):
    return pltpu.CompilerParams(dimension_semantics=semantics, vmem_limit_bytes=VMEM_LIMIT)


def _ada_kernel(c_ref, w_ref, b_ref, o_ref):
    o_ref[...] = jnp.dot(_silu(c_ref[...]), w_ref[...], preferred_element_type=F32,
                         precision=HIGHEST) + b_ref[...]


def _ada(c_pad, ada_w, ada_b):
    depth, d, n = ada_w.shape
    rows = c_pad.shape[0]
    return pl.pallas_call(
        _ada_kernel,
        grid=(depth, n // ADA_TN),
        in_specs=[
            pl.BlockSpec((rows, d), lambda l, j: (0, 0)),
            pl.BlockSpec((None, d, ADA_TN), lambda l, j: (l, 0, j)),
            pl.BlockSpec((None, 1, ADA_TN), lambda l, j: (l, 0, j)),
        ],
        out_specs=pl.BlockSpec((None, rows, ADA_TN), lambda l, j: (l, 0, j)),
        out_shape=jax.ShapeDtypeStruct((depth, rows, n), F32),
        compiler_params=_params("arbitrary", "arbitrary"),
    )(c_pad, ada_w, ada_b.reshape(depth, 1, n))


def _ffn_kernel(*refs, layer, sub, pre_ln, mix):
    x_ref, mod_ref, wi_hbm, wo_hbm, lng_ref, lnb_ref = refs[:6]
    n_scratch = 7
    extra = list(refs[6:-1 - n_scratch])
    o_ref = refs[-1 - n_scratch]
    act_scr, wi_ref, wo_ref, stage_g, stage_u, stage_o, sem = refs[-n_scratch:]
    if pre_ln:
        ing_ref, inb_ref = extra[:2]
        extra = extra[2:]
    if mix:
        yssd_ref, yfox_ref, ysc_ref, wm_ref, mlng_ref, mlnb_ref = extra
    shift = mod_ref[3 * sub:3 * sub + 1, :]
    scale = mod_ref[3 * sub + 1:3 * sub + 2, :]
    gain = mod_ref[3 * sub + 2:3 * sub + 3, :]
    halves = [slice(r, r + FFN_HALF) for r in range(0, x_ref.shape[0], FFN_HALF)]

    def entry(rows):
        x = x_ref[rows, :]
        if pre_ln:
            x = _layer_norm(x, ing_ref[...], inb_ref[...])
        if mix:
            y = jnp.dot(yssd_ref[rows, :], wm_ref[0:SSD_WIDTH, :], preferred_element_type=F32)
            y = y + jnp.dot(yfox_ref[rows, :], wm_ref[SSD_WIDTH:SSD_WIDTH + FOX_WIDTH, :],
                            preferred_element_type=F32)
            y = y + jnp.dot(ysc_ref[rows, :], wm_ref[SSD_WIDTH + FOX_WIDTH:, :],
                            preferred_element_type=F32)
            x = _layer_norm(ALPHA * x + mod_ref[5:6, :] * y, mlng_ref[...], mlnb_ref[...])
        return x

    n_chunks = D_FF // FFN_TF
    gate_cols = lambda c: slice(c * FFN_TF, (c + 1) * FFN_TF)
    up_cols = lambda c: slice(D_FF + c * FFN_TF, D_FF + (c + 1) * FFN_TF)

    def weight_copies(c):
        return (
            pltpu.make_async_copy(wi_hbm.at[layer, :, gate_cols(c)], stage_g, sem.at[0]),
            pltpu.make_async_copy(wi_hbm.at[layer, :, up_cols(c)], stage_u, sem.at[1]),
            pltpu.make_async_copy(wo_hbm.at[layer, gate_cols(c), :], stage_o, sem.at[2]),
        )

    def tile(load_weights):
        if load_weights:
            for copy in weight_copies(0):
                copy.start()
        xs = [entry(rows) for rows in halves]
        hs = [(x * (1.0 + scale) + shift).astype(BF16) for x in xs]

        def gate_up(r, c):
            gate = jnp.dot(hs[r], wi_ref[:, gate_cols(c)], preferred_element_type=F32)
            up = jnp.dot(hs[r], wi_ref[:, up_cols(c)], preferred_element_type=F32)
            act_scr[halves[r], gate_cols(c)] = (_silu(gate) * up).astype(BF16)

        if load_weights:
            for c in range(n_chunks):
                for copy in weight_copies(c):
                    copy.wait()
                wi_ref[:, gate_cols(c)] = stage_g[...].astype(BF16)
                wi_ref[:, up_cols(c)] = stage_u[...].astype(BF16)
                wo_ref[gate_cols(c), :] = stage_o[...].astype(BF16)
                if c + 1 < n_chunks:
                    for copy in weight_copies(c + 1):
                        copy.start()
                for r in range(len(halves)):
                    gate_up(r, c)
        else:
            for r in range(len(halves)):
                for c in range(n_chunks):
                    gate_up(r, c)
        ys = [jnp.dot(act_scr[rows, :], wo_ref[...], preferred_element_type=F32) for rows in halves]
        for rows, x, y in zip(halves, xs, ys):
            o_ref[rows, :] = _layer_norm(ALPHA * x + (0.5 * gain) * y, lng_ref[...], lnb_ref[...])

    first = pl.program_id(0) == 0
    pl.when(first)(lambda: tile(True))
    pl.when(jnp.logical_not(first))(lambda: tile(False))


def _ffn(x2d, mod_l, w_in, w_out, ln_g, ln_b, *, layer, sub, seq, pre=None, mix=None):
    n_tok, d = x2d.shape
    tm = min(FFN_TM if mix is None else FFN_TM_MIX, seq)
    assert seq % tm == 0 and n_tok % seq == 0 and D_FF % FFN_TF == 0 and tm % FFN_HALF == 0
    tiles_per_seq = seq // tm
    row = lambda v: v.reshape(1, d)
    resident = lambda shape: pl.BlockSpec(shape, lambda i: (0, 0), pipeline_mode=pl.Buffered(1))
    weight = lambda shape: pl.BlockSpec((None,) + shape, lambda i: (layer, 0, 0),
                                        pipeline_mode=pl.Buffered(1))
    in_specs = [
        pl.BlockSpec((tm, d), lambda i: (i, 0)),
        pl.BlockSpec((None, 3 * N_SUB, d), lambda i: (i // tiles_per_seq, 0, 0)),
        pl.BlockSpec(memory_space=pl.ANY),
        pl.BlockSpec(memory_space=pl.ANY),
        resident((1, d)),
        resident((1, d)),
    ]
    args = [x2d, mod_l, w_in, w_out, row(ln_g), row(ln_b)]
    if pre is not None:
        in_specs += [resident((1, d))] * 2
        args += [row(pre[0]), row(pre[1])]
    if mix is not None:
        y_ssd, y_fox, y_sc, w_mix, mix_g, mix_b = mix
        tok = lambda width: pl.BlockSpec((tm, width), lambda i: (i, 0))
        in_specs += [tok(SSD_WIDTH), tok(FOX_WIDTH), tok(SCONV_WIDTH), weight((d, d)),
                     resident((1, d)), resident((1, d))]
        args += [y_ssd, y_fox, y_sc, w_mix, row(mix_g), row(mix_b)]
    return pl.pallas_call(
        functools.partial(_ffn_kernel, layer=layer, sub=sub, pre_ln=pre is not None, mix=mix is not None),
        grid=(n_tok // tm,),
        in_specs=in_specs,
        out_specs=pl.BlockSpec((tm, d), lambda i: (i, 0)),
        out_shape=jax.ShapeDtypeStruct((n_tok, d), F32),
        scratch_shapes=[
            pltpu.VMEM((tm, D_FF), BF16),
            pltpu.VMEM((d, 2 * D_FF), BF16),
            pltpu.VMEM((D_FF, d), BF16),
            pltpu.VMEM((d, FFN_TF), F32),
            pltpu.VMEM((d, FFN_TF), F32),
            pltpu.VMEM((FFN_TF, d), F32),
            pltpu.SemaphoreType.DMA((3,)),
        ],
        compiler_params=_params("arbitrary"),
    )(*args)


def _proj_kernel(x_ref, mod_ref, w_ref, cw_ref, cb_ref, sbias_ref, alog_ref, scw_ref, fbias_ref, tri_ref,
                 z_ref, xs_ref, bm_ref, cm_ref, q_ref, k_ref, v_ref, ysc_ref, small_ref,
                 xbc_scr, u_scr, carry_scr, *, tm):
    t = pl.program_id(1)
    halo = SUBLANES

    @pl.when(t == 0)
    def _():
        xbc_scr[0:halo, :] = jnp.zeros((halo, SSD_CONV_DIM), F32)
        u_scr[0:halo, :] = jnp.zeros((halo, SCONV_WIDTH), F32)
        carry_scr[...] = jnp.zeros_like(carry_scr)

    @pl.when(t > 0)
    def _():
        xbc_scr[0:halo, :] = xbc_scr[tm:tm + halo, :]
        u_scr[0:halo, :] = u_scr[tm:tm + halo, :]

    shift = mod_ref[3:4, :]
    scale = mod_ref[4:5, :]
    h = (x_ref[0] * (1.0 + scale) + shift).astype(BF16)

    def seg(start, width):
        return jnp.dot(h, w_ref[:, start:start + width], preferred_element_type=F32)

    xbc_scr[halo:halo + tm, :] = seg(COL_XBC, SSD_CONV_DIM)
    conv = cb_ref[...] + cw_ref[SSD_CONV - 1:SSD_CONV, :] * xbc_scr[halo:halo + tm, :]
    for k in range(SSD_CONV - 1):
        back = SSD_CONV - 1 - k
        conv = conv + cw_ref[k:k + 1, :] * xbc_scr[halo - back:halo - back + tm, :]
    xbc = _silu(conv)
    xs_ref[0] = xbc[:, :SSD_WIDTH]
    bm_ref[0] = xbc[:, SSD_WIDTH:SSD_WIDTH + SSD_BC].astype(BF16)
    cm_ref[0] = xbc[:, SSD_WIDTH + SSD_BC:].astype(BF16)

    sc = seg(COL_SC, 3 * SCONV_WIDTH)
    u_scr[halo:halo + tm, :] = sc[:, SCONV_WIDTH:2 * SCONV_WIDTH] * sc[:, 2 * SCONV_WIDTH:]
    cu = scw_ref[SCONV_K - 1:SCONV_K, :] * u_scr[halo:halo + tm, :]
    for k in range(SCONV_K - 1):
        back = SCONV_K - 1 - k
        cu = cu + scw_ref[k:k + 1, :] * u_scr[halo - back:halo - back + tm, :]
    ysc_ref[0] = (sc[:, :SCONV_WIDTH] * cu).astype(BF16)

    lane = lax.broadcasted_iota(jnp.int32, (tm, LANES), 1)
    pos = lane % FOX_HEAD_DIM
    used = pos < (FOX_HEADS // HEADS_PER_VREG) * FOX_SLOT
    small_f = seg(COL_SMALL, 2 * LANES)
    log_f = jnp.where(used, -_softplus(-(small_f[:, LANES:] + fbias_ref[...])), 0.0)
    parts = _split_bf16(log_f, 3)
    offset = carry_scr[0:1, :]
    blocks = []
    for r in range(0, tm, PROJ_CUM):
        local = jnp.dot(tri_ref[...], parts[0][r:r + PROJ_CUM, :], preferred_element_type=F32)
        for part in parts[1:]:
            local = local + jnp.dot(tri_ref[...], part[r:r + PROJ_CUM, :], preferred_element_type=F32)
        blocks.append(local + offset)
        offset = offset + local[PROJ_CUM - 1:PROJ_CUM, :]
    cum_f = jnp.concatenate(blocks, axis=0)
    carry_scr[...] = jnp.broadcast_to(offset, carry_scr.shape)
    hi, mid, lo = (part.astype(F32) for part in _split_bf16(cum_f * LOG2_E, FOX_PIECES))
    piece = jnp.where(pos % FOX_PIECES == 0, hi, jnp.where(pos % FOX_PIECES == 1, mid, lo))
    in_a = pos % FOX_SLOT < FOX_PIECES
    f_q = jnp.where(in_a, piece, 1.0)
    f_k = jnp.where(in_a, 1.0, -piece)

    qkv = seg(COL_QKV, 3 * FOX_WIDTH)
    q = qkv[:, :FOX_WIDTH] * (FOX_HEAD_DIM ** -0.5 * LOG2_E)
    k = qkv[:, FOX_WIDTH:2 * FOX_WIDTH]
    v = qkv[:, 2 * FOX_WIDTH:]
    for hd in range(FOX_HEADS):
        pair, hh = divmod(hd, HEADS_PER_VREG)
        ps = slice(pair * LANES, (pair + 1) * LANES)
        gs = slice(hd * LANES, (hd + 1) * LANES)
        mine = (lane // FOX_HEAD_DIM) == hh
        first = _forget_lane(hd)
        forget = (lane >= first) & (lane < first + FOX_SLOT)
        ones_col = jnp.where(lane == (1 - hh) * FOX_HEAD_DIM, 1.0, 0.0)
        q_ref[0, :, gs] = jnp.where(mine, q[:, ps], jnp.where(forget, f_q, 0.0)).astype(BF16)
        k_ref[0, :, gs] = jnp.where(mine, k[:, ps], jnp.where(forget, f_k, 0.0)).astype(BF16)
        v_ref[0, :, gs] = jnp.where(mine, v[:, ps], ones_col).astype(BF16)

    dt = _softplus(small_f[:, :LANES] + sbias_ref[...])
    a_neg = -jnp.exp(alog_ref[...])
    small_ref[0] = jnp.where(lane < LANE_A, dt, dt * a_neg)

    z_ref[0] = seg(COL_Z, SSD_WIDTH)


def _proj(x, mod_l, w_cat, conv_w, conv_b, small_bias, alog_vec, sconv_w, f_bias, *, layer):
    b, seq, d = x.shape
    tm = min(PROJ_TM, seq)
    assert seq % tm == 0
    nt = seq // tm
    tok = lambda width: pl.BlockSpec((1, tm, width), lambda bi, t: (bi, t, 0))
    const2 = lambda shape: pl.BlockSpec(shape, lambda bi, t: (0, 0))
    out_shapes = (
        jax.ShapeDtypeStruct((b, seq, SSD_WIDTH), F32),
        jax.ShapeDtypeStruct((b, seq, SSD_WIDTH), F32),
        jax.ShapeDtypeStruct((b, seq, SSD_BC), BF16),
        jax.ShapeDtypeStruct((b, seq, SSD_BC), BF16),
        jax.ShapeDtypeStruct((b, seq, FOX_GROUP), BF16),
        jax.ShapeDtypeStruct((b, seq, FOX_GROUP), BF16),
        jax.ShapeDtypeStruct((b, seq, FOX_GROUP), BF16),
        jax.ShapeDtypeStruct((b, seq, SCONV_WIDTH), BF16),
        jax.ShapeDtypeStruct((b, seq, LANES), F32),
    )
    out_specs = (
        tok(SSD_WIDTH), tok(SSD_WIDTH), tok(SSD_BC), tok(SSD_BC),
        tok(FOX_GROUP), tok(FOX_GROUP), tok(FOX_GROUP), tok(SCONV_WIDTH), tok(LANES),
    )
    return pl.pallas_call(
        functools.partial(_proj_kernel, tm=tm),
        grid=(b, nt),
        in_specs=[
            tok(d),
            pl.BlockSpec((None, 3 * N_SUB, d), lambda bi, t: (bi, 0, 0)),
            pl.BlockSpec((None, d, D_PROJ), lambda bi, t: (layer, 0, 0)),
            const2((SSD_CONV, SSD_CONV_DIM)),
            const2((1, SSD_CONV_DIM)),
            const2((1, LANES)),
            const2((1, LANES)),
            const2((SCONV_K, SCONV_WIDTH)),
            const2((1, LANES)),
            const2((PROJ_CUM, PROJ_CUM)),
        ],
        out_specs=out_specs,
        out_shape=out_shapes,
        scratch_shapes=[
            pltpu.VMEM((tm + SUBLANES, SSD_CONV_DIM), F32),
            pltpu.VMEM((tm + SUBLANES, SCONV_WIDTH), F32),
            pltpu.VMEM((SUBLANES, LANES), F32),
        ],
        compiler_params=_params("arbitrary", "arbitrary"),
    )(x, mod_l, w_cat, conv_w, conv_b, small_bias, alog_vec, sconv_w, f_bias, _lower_ones(PROJ_CUM))


def _ssd_kernel(xs_ref, bm_ref, cm_ref, z_ref, small_ref, dskip_ref, ng_ref, tri_ref, spread_ref,
                o_ref, state_scr, *, t, chunks):
    @pl.when(pl.program_id(1) == 0)
    def _():
        state_scr[...] = jnp.zeros_like(state_scr)

    lane = lax.broadcasted_iota(jnp.int32, (t, LANES), 1)
    row_i = lax.broadcasted_iota(jnp.int32, (t, t), 0)
    col_i = lax.broadcasted_iota(jnp.int32, (t, t), 1)
    lower = row_i >= col_i
    head_of_lane = lax.broadcasted_iota(jnp.int32, (t, GROUP_WIDTH), 1) // SSD_HEAD_DIM

    for ci in range(chunks):
        rows = slice(ci * t, (ci + 1) * t)
        small = small_ref[0, rows, :]
        a_only = jnp.where((lane >= LANE_A) & (lane < LANE_A2 + SSD_HEADS), small, 0.0)
        cs = jnp.zeros((t, LANES), F32)
        for part in _split_bf16(a_only, 3):
            cs = cs + jnp.dot(tri_ref[...], part, preferred_element_type=F32)
        cs_t = cs.T
        total = cs[t - 1:t, :]
        per_head = jnp.where(lane < LANE_A, small,
                             jnp.where(lane < LANE_A2, jnp.exp(cs), jnp.exp(total - cs)))
        wide = jnp.zeros((t, 3 * SSD_WIDTH), F32)
        for part in _split_bf16(per_head, 2):
            wide = wide + jnp.dot(part, spread_ref[...], preferred_element_type=F32)
        dt_x = wide[:, :SSD_WIDTH]
        decay_in = wide[:, SSD_WIDTH:2 * SSD_WIDTH]
        decay_out = wide[:, 2 * SSD_WIDTH:]
        decay_chunk = decay_in[t - 1:t, :]

        xs = xs_ref[0, rows, :]
        xdt = xs * dt_x
        ys = []
        for g in range(SSD_GROUPS):
            gs = slice(g * GROUP_WIDTH, (g + 1) * GROUP_WIDTH)
            ns = slice(g * SSD_STATE, (g + 1) * SSD_STATE)
            bg = bm_ref[0, rows, ns]
            cg = cm_ref[0, rows, ns]
            cb = lax.dot_general(cg, bg, (((1,), (1,)), ((), ())), preferred_element_type=F32)
            xg = xdt[:, gs]
            m_parts, x_parts = [], []
            for e in range(SSD_HPG):
                hd = g * SSD_HPG + e
                col = cs[:, LANE_A + hd:LANE_A + hd + 1]
                row = cs_t[LANE_A + hd:LANE_A + hd + 1, :]
                l_mat = jnp.exp(jnp.where(lower, col - row, -jnp.inf))
                m_parts.append((cb * l_mat).astype(BF16))
                x_parts.append(jnp.where(head_of_lane == e, xg, 0.0).astype(BF16))
            m_cat = jnp.concatenate(m_parts, axis=1)
            x_blk = jnp.concatenate(x_parts, axis=0)
            y_diag = jnp.dot(m_cat, x_blk, preferred_element_type=F32)

            state = state_scr[g]
            y_off = jnp.dot(cg, state.astype(BF16), preferred_element_type=F32) * decay_in[:, gs]
            xd = (xg * decay_out[:, gs]).astype(BF16)
            upd = lax.dot_general(bg, xd, (((0,), (0,)), ((), ())), preferred_element_type=F32)
            state_scr[g] = state * decay_chunk[:, gs] + upd
            ys.append(y_diag + y_off + dskip_ref[:, gs] * xs[:, gs])

        z = z_ref[0, rows, :]
        for g in range(SSD_GROUPS):
            gs = slice(g * GROUP_WIDTH, (g + 1) * GROUP_WIDTH)
            yg = ys[g] * _silu(z[:, gs])
            yg = yg * lax.rsqrt(jnp.mean(yg * yg, axis=-1, keepdims=True) + RMS_EPS)
            o_ref[0, rows, gs] = (yg * ng_ref[:, gs]).astype(BF16)


def _head_spread():
    src = jnp.arange(LANES)[:, None]
    col = jnp.arange(3 * SSD_WIDTH)[None, :]
    first_lane = jnp.array([LANE_DT, LANE_A, LANE_A2])[col // SSD_WIDTH]
    return (src == first_lane + (col % SSD_WIDTH) // SSD_HEAD_DIM).astype(BF16)


def _ssd(xs, bm, cm, z, small, d_skip_x, norm_g):
    b, seq, _ = xs.shape
    t = min(SSD_T, seq)
    rows = min(SSD_ROWS, seq)
    assert seq % rows == 0 and rows % t == 0
    tok = lambda width: pl.BlockSpec((1, rows, width), lambda bi, c: (bi, c, 0))
    const2 = lambda shape: pl.BlockSpec(shape, lambda bi, c: (0, 0))
    return pl.pallas_call(
        functools.partial(_ssd_kernel, t=t, chunks=rows // t),
        grid=(b, seq // rows),
        in_specs=[tok(SSD_WIDTH), tok(SSD_BC), tok(SSD_BC), tok(SSD_WIDTH), tok(LANES),
                  const2((1, SSD_WIDTH)), const2((1, SSD_WIDTH)),
                  const2((t, t)), const2((LANES, 3 * SSD_WIDTH))],
        out_specs=tok(SSD_WIDTH),
        out_shape=jax.ShapeDtypeStruct((b, seq, SSD_WIDTH), BF16),
        scratch_shapes=[pltpu.VMEM((SSD_GROUPS, SSD_STATE, GROUP_WIDTH), F32)],
        compiler_params=_params("arbitrary", "arbitrary"),
    )(xs, bm, cm, z, small, d_skip_x, norm_g, _lower_ones(t), _head_spread())


def _fox_kernel(q_ref, k_ref, v_ref, o_ref, m_scr, acc_scr, *, t):
    qi = pl.program_id(1)
    lane_head = lax.broadcasted_iota(jnp.int32, (t, LANES), 1) // FOX_HEAD_DIM
    row_i = lax.broadcasted_iota(jnp.int32, (t, t), 0)
    col_i = lax.broadcasted_iota(jnp.int32, (t, t), 1)
    causal = row_i >= col_i
    heads_per_vreg = LANES // FOX_HEAD_DIM

    def block(start, width, masked):
        ks = pl.ds(start, width)
        group = lambda hd: slice(hd * LANES, (hd + 1) * LANES)

        def logits(hd):
            s = lax.dot_general(q_ref[0, :, group(hd)], k_ref[0, ks, group(hd)],
                                (((1,), (1,)), ((), ())), preferred_element_type=F32)
            return jnp.where(causal, s, -jnp.inf) if masked else s

        def update(hd, s):
            m_prev = m_scr[hd]
            m_new = jnp.maximum(m_prev, jnp.max(s, axis=-1, keepdims=True))
            p = jnp.exp2(s - jnp.concatenate([m_new] * (width // LANES), axis=1)).astype(BF16)
            acc_scr[hd] = jnp.exp2(m_prev - m_new) * acc_scr[hd] + jnp.dot(
                p, v_ref[0, ks, group(hd)], preferred_element_type=F32)
            m_scr[hd] = m_new

        s_next = logits(0)
        for hd in range(FOX_HEADS):
            s_cur = s_next
            if hd + 1 < FOX_HEADS:
                s_next = logits(hd + 1)
            update(hd, s_cur)

    m_scr[...] = jnp.full(m_scr.shape, -jnp.inf, F32)
    acc_scr[...] = jnp.zeros(acc_scr.shape, F32)

    wide = 2 * t
    assert k_ref.shape[1] % wide == 0

    @pl.loop(0, qi // 2)
    def _(pj):
        block(pl.multiple_of(pj * wide, wide), wide, False)

    @pl.when(qi % 2 == 1)
    def _():
        block(pl.multiple_of((qi - 1) * t, t), t, False)

    block(pl.multiple_of(qi * t, t), t, True)
    for pair in range(FOX_HEADS // heads_per_vreg):
        out = jnp.zeros((t, LANES), F32)
        for hh in range(heads_per_vreg):
            acc = acc_scr[pair * heads_per_vreg + hh]
            ones_lane = (1 - hh) * FOX_HEAD_DIM
            row_sum = acc[:, ones_lane:ones_lane + 1]
            out = out + jnp.where(lane_head == hh, acc, 0.0) / row_sum
        o_ref[0, :, pair * LANES:(pair + 1) * LANES] = out.astype(BF16)


def _fox(qa, ka, va):
    b, seq, _ = qa.shape
    t = min(FOX_T, seq)
    assert seq % t == 0
    return pl.pallas_call(
        functools.partial(_fox_kernel, t=t),
        grid=(b, seq // t),
        in_specs=[
            pl.BlockSpec((1, t, FOX_GROUP), lambda bi, i: (bi, i, 0)),
            pl.BlockSpec((1, seq, FOX_GROUP), lambda bi, i: (bi, 0, 0)),
            pl.BlockSpec((1, seq, FOX_GROUP), lambda bi, i: (bi, 0, 0)),
        ],
        out_specs=pl.BlockSpec((1, t, FOX_WIDTH), lambda bi, i: (bi, i, 0)),
        out_shape=jax.ShapeDtypeStruct((b, seq, FOX_WIDTH), BF16),
        scratch_shapes=[pltpu.VMEM((FOX_HEADS, t, LANES), F32), pltpu.VMEM((FOX_HEADS, t, LANES), F32)],
        compiler_params=_params("arbitrary", "arbitrary"),
    )(qa, ka, va)


def _lower_ones(n):
    return jnp.tril(jnp.ones((n, n), BF16))


def _spread_forget(f):
    lead = f.shape[:-1]
    order = sorted(range(FOX_HEADS), key=_forget_lane)
    blocks, lane = [], 0
    for hd in order:
        blocks.append(jnp.zeros(lead + (_forget_lane(hd) - lane,), f.dtype))
        blocks.append(jnp.broadcast_to(f[..., hd:hd + 1], lead + (FOX_SLOT,)))
        lane = _forget_lane(hd) + FOX_SLOT
    blocks.append(jnp.zeros(lead + (LANES - lane,), f.dtype))
    return jnp.concatenate(blocks, axis=-1)


def _pack_mix_w_in(w):
    ssd_in = SSD_WIDTH + SSD_CONV_DIM + SSD_HEADS
    fox_in = 3 * FOX_WIDTH + FOX_HEADS
    dt_cols = w[..., SSD_WIDTH + SSD_CONV_DIM:ssd_in]
    f_cols = w[..., ssd_in + 3 * FOX_WIDTH:ssd_in + fox_in]
    pad = jnp.zeros(w.shape[:-1] + (LANES - 3 * SSD_HEADS,), w.dtype)
    return jnp.concatenate([
        w[..., :SSD_WIDTH + SSD_CONV_DIM],
        w[..., ssd_in:ssd_in + 3 * FOX_WIDTH],
        w[..., ssd_in + fox_in:],
        dt_cols, dt_cols, dt_cols, pad,
        _spread_forget(f_cols),
    ], axis=-1).astype(BF16)


def _lane_vec(pieces):
    row = jnp.zeros((LANES,), F32)
    for off, vec in pieces:
        row = row.at[off:off + vec.shape[0]].set(vec.astype(F32))
    return row.reshape(1, LANES)


def kernel(x, c, ln_in_g, ln_in_b, ada_w, ada_b, ffn1_w_in, ffn1_w_out, mix_w_in, mix_w_out, ssd_conv_w, ssd_conv_b, ssd_dt_bias, ssd_a_log, ssd_d, ssd_norm_g, fox_f_bias, sconv_w, ffn2_w_in, ffn2_w_out, ln_g, ln_b):
    b, seq, d = x.shape
    depth = ada_w.shape[0]
    rows = -(-b // SUBLANES) * SUBLANES
    c_pad = jnp.pad(c, ((0, rows - b), (0, 0)))
    mod = _ada(c_pad, ada_w, ada_b).reshape(depth, rows, 3 * N_SUB, d)

    w1_in, w1_out, w2_in, w2_out = ffn1_w_in, ffn1_w_out, ffn2_w_in, ffn2_w_out
    wm_in, wm_out = _pack_mix_w_in(mix_w_in), mix_w_out.astype(BF16)

    x2d = x.reshape(b * seq, d)
    for l in range(depth):
        mod_l = mod[l]
        x2d = _ffn(x2d, mod_l, w1_in, w1_out, ln_g[l, 0], ln_b[l, 0], layer=l, sub=0, seq=seq,
                   pre=(ln_in_g, ln_in_b) if l == 0 else None)

        small_bias = _lane_vec([(lane, ssd_dt_bias[l]) for lane in (LANE_DT, LANE_A, LANE_A2)])
        alog_vec = _lane_vec([(LANE_A, ssd_a_log[l]), (LANE_A2, ssd_a_log[l])])
        z, xs, bm, cm, qa, ka, va, y_sc, small = _proj(
            x2d.reshape(b, seq, d), mod_l, wm_in,
            ssd_conv_w[l], ssd_conv_b[l].reshape(1, SSD_CONV_DIM), small_bias, alog_vec, sconv_w[l],
            _spread_forget(fox_f_bias[l]).reshape(1, LANES), layer=l)
        d_skip_x = jnp.repeat(ssd_d[l], SSD_HEAD_DIM).reshape(1, SSD_WIDTH)
        y_ssd = _ssd(xs, bm, cm, z, small, d_skip_x, ssd_norm_g[l].reshape(1, SSD_WIDTH))
        y_fox = _fox(qa, ka, va)
        mixer_tail = (y_ssd.reshape(b * seq, SSD_WIDTH), y_fox.reshape(b * seq, FOX_WIDTH),
                      y_sc.reshape(b * seq, SCONV_WIDTH), wm_out, ln_g[l, 1], ln_b[l, 1])
        x2d = _ffn(x2d, mod_l, w2_in, w2_out, ln_g[l, 2], ln_b[l, 2], layer=l, sub=2, seq=seq,
                   mix=mixer_tail)
    return x2d.reshape(b, seq, d)
```

```python
import functools
import math

import jax
import jax.numpy as jnp
from jax import lax
from jax.experimental import pallas as pl
from jax.experimental.pallas import tpu as pltpu

F32 = jnp.float32
BF16 = jnp.bfloat16

D_MODEL = 1024
DEPTH = 2
N_SUB = 3
D_FF = 2816
SSD_WIDTH = 512
SSD_HEAD_DIM = 64
SSD_HEADS = 8
SSD_GROUPS = 2
SSD_HPG = SSD_HEADS // SSD_GROUPS
SSD_STATE = 128
SSD_CONV = 4
SSD_BC = SSD_GROUPS * SSD_STATE
SSD_CONV_DIM = SSD_WIDTH + 2 * SSD_BC
GROUP_WIDTH = SSD_WIDTH // SSD_GROUPS
FOX_WIDTH = 256
FOX_HEAD_DIM = 64
FOX_HEADS = 4
SCONV_WIDTH = 256
SCONV_K = 3
ALPHA = (2 * DEPTH) ** 0.25
LN_EPS = 1e-5
RMS_EPS = 1e-5
LOG2_E = math.log2(math.e)

LANES = 128
SUBLANES = 8
VMEM_LIMIT = 56 * 1024 * 1024

COL_Z = 0
COL_XBC = COL_Z + SSD_WIDTH
COL_QKV = COL_XBC + SSD_CONV_DIM
COL_SC = COL_QKV + 3 * FOX_WIDTH
COL_SMALL = COL_SC + 3 * SCONV_WIDTH
COL_F = COL_SMALL + LANES
D_PROJ = COL_F + LANES
LANE_DT = 0
LANE_A = 8
LANE_A2 = 16
FOX_PIECES = 3
FOX_SLOT = 2 * FOX_PIECES
FOX_GROUP = FOX_HEADS * LANES
HEADS_PER_VREG = LANES // FOX_HEAD_DIM


def _forget_lane(hd):
    pair, hh = divmod(hd, HEADS_PER_VREG)
    return (HEADS_PER_VREG - 1 - hh) * FOX_HEAD_DIM + pair * FOX_SLOT

FFN_TM = 1024
FFN_TM_MIX = 512
FFN_TF = 256
FFN_HALF = 512
FFN_W_IN_ROWS = 64
FFN_W_OUT_ROWS = 256
PROJ_TM = 512
PROJ_CUM = 128
SSD_T = 256
SSD_ROWS = 512
FOX_T = 512
ADA_TN = 1024


def _layer_norm(x, g, b):
    mu = jnp.mean(x, axis=-1, keepdims=True)
    xc = x - mu
    var = jnp.mean(xc * xc, axis=-1, keepdims=True)
    return xc * lax.rsqrt(var + LN_EPS) * g + b


def _silu(x):
    return x / (1.0 + jnp.exp2(x * (-LOG2_E)))


def _softplus(x):
    return jnp.maximum(x, 0.0) + jnp.log1p(jnp.exp(-jnp.abs(x)))


def _split_bf16(x, pieces):
    out = []
    for _ in range(pieces):
        part = x.astype(BF16)
        out.append(part)
        x = x - part.astype(F32)
    return out


def _params(*semantics):
    return pltpu.CompilerParams(dimension_semantics=semantics, vmem_limit_bytes=VMEM_LIMIT)


def _ada_kernel(c_ref, w_ref, b_ref, o_ref, *, batch):
    act_hi, act_lo = _split_bf16(_silu(c_ref[...]), 2)
    row = lax.broadcasted_iota(jnp.int32, act_hi.shape, 0)
    lhs = jnp.where(row < batch, act_hi, act_lo)
    w_hi, w_lo = _split_bf16(w_ref[...], 2)
    y = jnp.dot(lhs, w_hi, preferred_element_type=F32) + jnp.dot(lhs, w_lo, preferred_element_type=F32)
    rows = y.shape[0]
    o_ref[...] = y + pltpu.roll(y, rows - batch, axis=0) + b_ref[...]


def _ada(c, ada_w, ada_b):
    depth, d, n = ada_w.shape
    batch = c.shape[0]
    rows = -(-2 * batch // SUBLANES) * SUBLANES
    c_pad = jnp.pad(jnp.concatenate([c, c], axis=0), ((0, rows - 2 * batch), (0, 0)))
    return pl.pallas_call(
        functools.partial(_ada_kernel, batch=batch),
        grid=(depth, n // ADA_TN),
        in_specs=[
            pl.BlockSpec((rows, d), lambda l, j: (0, 0)),
            pl.BlockSpec((None, d, ADA_TN), lambda l, j: (l, 0, j)),
            pl.BlockSpec((None, 1, ADA_TN), lambda l, j: (l, 0, j)),
        ],
        out_specs=pl.BlockSpec((None, rows, ADA_TN), lambda l, j: (l, 0, j)),
        out_shape=jax.ShapeDtypeStruct((depth, rows, n), F32),
        compiler_params=_params("arbitrary", "arbitrary"),
    )(c_pad, ada_w, ada_b.reshape(depth, 1, n))


def _ffn_kernel(*refs, layer, sub, pre_ln, mix):
    x_ref, mod_ref, wi_hbm, wo_hbm, lng_ref, lnb_ref = refs[:6]
    n_scratch = 6
    extra = list(refs[6:-1 - n_scratch])
    o_ref = refs[-1 - n_scratch]
    act_scr, wi_ref, wo_ref, stage_i, stage_o, sem = refs[-n_scratch:]
    if pre_ln:
        ing_ref, inb_ref = extra[:2]
        extra = extra[2:]
    if mix:
        yssd_ref, yfox_ref, ysc_ref, wm_ref, mlng_ref, mlnb_ref = extra
    shift = mod_ref[3 * sub:3 * sub + 1, :]
    scale = mod_ref[3 * sub + 1:3 * sub + 2, :]
    gain = mod_ref[3 * sub + 2:3 * sub + 3, :]
    halves = [slice(r, r + FFN_HALF) for r in range(0, x_ref.shape[0], FFN_HALF)]

    def entry(rows):
        x = x_ref[rows, :]
        if pre_ln:
            x = _layer_norm(x, ing_ref[...], inb_ref[...])
        if mix:
            y = jnp.dot(yssd_ref[rows, :], wm_ref[0:SSD_WIDTH, :], preferred_element_type=F32)
            y = y + jnp.dot(yfox_ref[rows, :], wm_ref[SSD_WIDTH:SSD_WIDTH + FOX_WIDTH, :],
                            preferred_element_type=F32)
            y = y + jnp.dot(ysc_ref[rows, :], wm_ref[SSD_WIDTH + FOX_WIDTH:, :],
                            preferred_element_type=F32)
            x = _layer_norm(ALPHA * x + mod_ref[5:6, :] * y, mlng_ref[...], mlnb_ref[...])
        return x

    @pl.when(pl.program_id(0) == 0)
    def _():
        def stream(src_hbm, dst_ref, stage, which, block_rows):
            n_blocks = src_hbm.shape[1] // block_rows
            copy = lambda b: pltpu.make_async_copy(
                src_hbm.at[layer, b * block_rows:(b + 1) * block_rows, :], stage.at[b % 2],
                sem.at[which, b % 2])
            copy(0).start()
            for b in range(n_blocks):
                if b + 1 < n_blocks:
                    copy(b + 1).start()
                copy(b).wait()
                dst_ref[b * block_rows:(b + 1) * block_rows, :] = stage[b % 2].astype(BF16)

        stream(wi_hbm, wi_ref, stage_i, 0, FFN_W_IN_ROWS)
        stream(wo_hbm, wo_ref, stage_o, 1, FFN_W_OUT_ROWS)

    xs = [entry(rows) for rows in halves]
    ys = []
    for rows, x in zip(halves, xs):
        h = (x * (1.0 + scale) + shift).astype(BF16)
        for c in range(D_FF // FFN_TF):
            cols = slice(c * FFN_TF, (c + 1) * FFN_TF)
            gate = jnp.dot(h, wi_ref[:, cols], preferred_element_type=F32)
            up = jnp.dot(h, wi_ref[:, D_FF + c * FFN_TF:D_FF + (c + 1) * FFN_TF],
                         preferred_element_type=F32)
            act_scr[rows, cols] = (_silu(gate) * up).astype(BF16)
        ys.append(jnp.dot(act_scr[rows, :], wo_ref[...], preferred_element_type=F32))
    for rows, x, y in zip(halves, xs, ys):
        o_ref[rows, :] = _layer_norm(ALPHA * x + (0.5 * gain) * y, lng_ref[...], lnb_ref[...])


def _ffn(x2d, mod_l, w_in, w_out, ln_g, ln_b, *, layer, sub, seq, pre=None, mix=None):
    n_tok, d = x2d.shape
    tm = min(FFN_TM if mix is None else FFN_TM_MIX, seq)
    assert seq % tm == 0 and n_tok % seq == 0 and D_FF % FFN_TF == 0 and tm % FFN_HALF == 0
    tiles_per_seq = seq // tm
    row = lambda v: v.reshape(1, d)
    resident = lambda shape: pl.BlockSpec(shape, lambda i: (0, 0), pipeline_mode=pl.Buffered(1))
    weight = lambda shape: pl.BlockSpec((None,) + shape, lambda i: (layer, 0, 0),
                                        pipeline_mode=pl.Buffered(1))
    in_specs = [
        pl.BlockSpec((tm, d), lambda i: (i, 0)),
        pl.BlockSpec((None, 3 * N_SUB, d), lambda i: (i // tiles_per_seq, 0, 0)),
        pl.BlockSpec(memory_space=pl.ANY),
        pl.BlockSpec(memory_space=pl.ANY),
        resident((1, d)),
        resident((1, d)),
    ]
    args = [x2d, mod_l, w_in, w_out, row(ln_g), row(ln_b)]
    if pre is not None:
        in_specs += [resident((1, d))] * 2
        args += [row(pre[0]), row(pre[1])]
    if mix is not None:
        y_ssd, y_fox, y_sc, w_mix, mix_g, mix_b = mix
        tok = lambda width: pl.BlockSpec((tm, width), lambda i: (i, 0))
        in_specs += [tok(SSD_WIDTH), tok(FOX_WIDTH), tok(SCONV_WIDTH), weight((d, d)),
                     resident((1, d)), resident((1, d))]
        args += [y_ssd, y_fox, y_sc, w_mix, row(mix_g), row(mix_b)]
    return pl.pallas_call(
        functools.partial(_ffn_kernel, layer=layer, sub=sub, pre_ln=pre is not None, mix=mix is not None),
        grid=(n_tok // tm,),
        in_specs=in_specs,
        out_specs=pl.BlockSpec((tm, d), lambda i: (i, 0)),
        out_shape=jax.ShapeDtypeStruct((n_tok, d), F32),
        scratch_shapes=[
            pltpu.VMEM((tm, D_FF), BF16),
            pltpu.VMEM((d, 2 * D_FF), BF16),
            pltpu.VMEM((D_FF, d), BF16),
            pltpu.VMEM((2, FFN_W_IN_ROWS, 2 * D_FF), F32),
            pltpu.VMEM((2, FFN_W_OUT_ROWS, d), F32),
            pltpu.SemaphoreType.DMA((2, 2)),
        ],
        compiler_params=_params("arbitrary"),
    )(*args)


def _proj_kernel(x_ref, mod_ref, w_ref, cw_ref, cb_ref, sbias_ref, alog_ref, scw_ref, fbias_ref, tri_ref,
                 z_ref, xs_ref, bm_ref, cm_ref, q_ref, k_ref, v_ref, ysc_ref, small_ref,
                 xbc_scr, u_scr, carry_scr, *, tm):
    t = pl.program_id(1)
    halo = SUBLANES

    @pl.when(t == 0)
    def _():
        xbc_scr[0:halo, :] = jnp.zeros((halo, SSD_CONV_DIM), F32)
        u_scr[0:halo, :] = jnp.zeros((halo, SCONV_WIDTH), F32)
        carry_scr[...] = jnp.zeros_like(carry_scr)

    @pl.when(t > 0)
    def _():
        xbc_scr[0:halo, :] = xbc_scr[tm:tm + halo, :]
        u_scr[0:halo, :] = u_scr[tm:tm + halo, :]

    shift = mod_ref[3:4, :]
    scale = mod_ref[4:5, :]
    h = (x_ref[0] * (1.0 + scale) + shift).astype(BF16)

    def seg(start, width):
        return jnp.dot(h, w_ref[:, start:start + width], preferred_element_type=F32)

    xbc_scr[halo:halo + tm, :] = seg(COL_XBC, SSD_CONV_DIM)
    conv = cb_ref[...] + cw_ref[SSD_CONV - 1:SSD_CONV, :] * xbc_scr[halo:halo + tm, :]
    for k in range(SSD_CONV - 1):
        back = SSD_CONV - 1 - k
        conv = conv + cw_ref[k:k + 1, :] * xbc_scr[halo - back:halo - back + tm, :]
    xbc = _silu(conv)
    xs_ref[0] = xbc[:, :SSD_WIDTH]
    bm_ref[0] = xbc[:, SSD_WIDTH:SSD_WIDTH + SSD_BC].astype(BF16)
    cm_ref[0] = xbc[:, SSD_WIDTH + SSD_BC:].astype(BF16)

    sc = seg(COL_SC, 3 * SCONV_WIDTH)
    u_scr[halo:halo + tm, :] = sc[:, SCONV_WIDTH:2 * SCONV_WIDTH] * sc[:, 2 * SCONV_WIDTH:]
    cu = scw_ref[SCONV_K - 1:SCONV_K, :] * u_scr[halo:halo + tm, :]
    for k in range(SCONV_K - 1):
        back = SCONV_K - 1 - k
        cu = cu + scw_ref[k:k + 1, :] * u_scr[halo - back:halo - back + tm, :]
    ysc_ref[0] = (sc[:, :SCONV_WIDTH] * cu).astype(BF16)

    lane = lax.broadcasted_iota(jnp.int32, (tm, LANES), 1)
    pos = lane % FOX_HEAD_DIM
    used = pos < (FOX_HEADS // HEADS_PER_VREG) * FOX_SLOT
    small_f = seg(COL_SMALL, 2 * LANES)
    log_f = jnp.where(used, -_softplus(-(small_f[:, LANES:] + fbias_ref[...])), 0.0)
    parts = _split_bf16(log_f, 3)
    offset = carry_scr[0:1, :]
    blocks = []
    for r in range(0, tm, PROJ_CUM):
        local = jnp.dot(tri_ref[...], parts[0][r:r + PROJ_CUM, :], preferred_element_type=F32)
        for part in parts[1:]:
            local = local + jnp.dot(tri_ref[...], part[r:r + PROJ_CUM, :], preferred_element_type=F32)
        blocks.append(local + offset)
        offset = offset + local[PROJ_CUM - 1:PROJ_CUM, :]
    cum_f = jnp.concatenate(blocks, axis=0)
    carry_scr[...] = jnp.broadcast_to(offset, carry_scr.shape)
    hi, mid, lo = (part.astype(F32) for part in _split_bf16(cum_f * LOG2_E, FOX_PIECES))
    piece = jnp.where(pos % FOX_PIECES == 0, hi, jnp.where(pos % FOX_PIECES == 1, mid, lo))
    in_a = pos % FOX_SLOT < FOX_PIECES
    f_q = jnp.where(in_a, piece, 1.0)
    f_k = jnp.where(in_a, 1.0, -piece)

    qkv = seg(COL_QKV, 3 * FOX_WIDTH)
    q = qkv[:, :FOX_WIDTH] * (FOX_HEAD_DIM ** -0.5 * LOG2_E)
    k = qkv[:, FOX_WIDTH:2 * FOX_WIDTH]
    v = qkv[:, 2 * FOX_WIDTH:]
    for hd in range(FOX_HEADS):
        pair, hh = divmod(hd, HEADS_PER_VREG)
        ps = slice(pair * LANES, (pair + 1) * LANES)
        gs = slice(hd * LANES, (hd + 1) * LANES)
        mine = (lane // FOX_HEAD_DIM) == hh
        first = _forget_lane(hd)
        forget = (lane >= first) & (lane < first + FOX_SLOT)
        ones_col = jnp.where(lane == (1 - hh) * FOX_HEAD_DIM, 1.0, 0.0)
        q_ref[0, :, gs] = jnp.where(mine, q[:, ps], jnp.where(forget, f_q, 0.0)).astype(BF16)
        k_ref[0, :, gs] = jnp.where(mine, k[:, ps], jnp.where(forget, f_k, 0.0)).astype(BF16)
        v_ref[0, :, gs] = jnp.where(mine, v[:, ps], ones_col).astype(BF16)

    dt = _softplus(small_f[:, :LANES] + sbias_ref[...])
    a_neg = -jnp.exp(alog_ref[...])
    small_ref[0] = jnp.where(lane < LANE_A, dt, dt * a_neg)

    z_ref[0] = seg(COL_Z, SSD_WIDTH)


def _proj(x, mod_l, w_cat, conv_w, conv_b, small_bias, alog_vec, sconv_w, f_bias, *, layer):
    b, seq, d = x.shape
    tm = min(PROJ_TM, seq)
    assert seq % tm == 0
    nt = seq // tm
    tok = lambda width: pl.BlockSpec((1, tm, width), lambda bi, t: (bi, t, 0))
    const2 = lambda shape: pl.BlockSpec(shape, lambda bi, t: (0, 0))
    out_shapes = (
        jax.ShapeDtypeStruct((b, seq, SSD_WIDTH), F32),
        jax.ShapeDtypeStruct((b, seq, SSD_WIDTH), F32),
        jax.ShapeDtypeStruct((b, seq, SSD_BC), BF16),
        jax.ShapeDtypeStruct((b, seq, SSD_BC), BF16),
        jax.ShapeDtypeStruct((b, seq, FOX_GROUP), BF16),
        jax.ShapeDtypeStruct((b, seq, FOX_GROUP), BF16),
        jax.ShapeDtypeStruct((b, seq, FOX_GROUP), BF16),
        jax.ShapeDtypeStruct((b, seq, SCONV_WIDTH), BF16),
        jax.ShapeDtypeStruct((b, seq, LANES), F32),
    )
    out_specs = (
        tok(SSD_WIDTH), tok(SSD_WIDTH), tok(SSD_BC), tok(SSD_BC),
        tok(FOX_GROUP), tok(FOX_GROUP), tok(FOX_GROUP), tok(SCONV_WIDTH), tok(LANES),
    )
    return pl.pallas_call(
        functools.partial(_proj_kernel, tm=tm),
        grid=(b, nt),
        in_specs=[
            tok(d),
            pl.BlockSpec((None, 3 * N_SUB, d), lambda bi, t: (bi, 0, 0)),
            pl.BlockSpec((None, d, D_PROJ), lambda bi, t: (layer, 0, 0)),
            const2((SSD_CONV, SSD_CONV_DIM)),
            const2((1, SSD_CONV_DIM)),
            const2((1, LANES)),
            const2((1, LANES)),
            const2((SCONV_K, SCONV_WIDTH)),
            const2((1, LANES)),
            const2((PROJ_CUM, PROJ_CUM)),
        ],
        out_specs=out_specs,
        out_shape=out_shapes,
        scratch_shapes=[
            pltpu.VMEM((tm + SUBLANES, SSD_CONV_DIM), F32),
            pltpu.VMEM((tm + SUBLANES, SCONV_WIDTH), F32),
            pltpu.VMEM((SUBLANES, LANES), F32),
        ],
        compiler_params=_params("arbitrary", "arbitrary"),
    )(x, mod_l, w_cat, conv_w, conv_b, small_bias, alog_vec, sconv_w, f_bias, _lower_ones(PROJ_CUM))


def _ssd_kernel(xs_ref, bm_ref, cm_ref, z_ref, small_ref, dskip_ref, ng_ref, tri_ref, spread_ref,
                o_ref, state_scr, *, t, chunks):
    @pl.when(pl.program_id(1) == 0)
    def _():
        state_scr[...] = jnp.zeros_like(state_scr)

    lane = lax.broadcasted_iota(jnp.int32, (t, LANES), 1)
    row_i = lax.broadcasted_iota(jnp.int32, (t, t), 0)
    col_i = lax.broadcasted_iota(jnp.int32, (t, t), 1)
    lower = row_i >= col_i
    head_of_lane = lax.broadcasted_iota(jnp.int32, (t, GROUP_WIDTH), 1) // SSD_HEAD_DIM

    for ci in range(chunks):
        rows = slice(ci * t, (ci + 1) * t)
        small = small_ref[0, rows, :]
        a_only = jnp.where((lane >= LANE_A) & (lane < LANE_A2 + SSD_HEADS), small, 0.0)
        cs = jnp.zeros((t, LANES), F32)
        for part in _split_bf16(a_only, 3):
            cs = cs + jnp.dot(tri_ref[...], part, preferred_element_type=F32)
        cs_t = cs.T
        total = cs[t - 1:t, :]
        per_head = jnp.where(lane < LANE_A, small,
                             jnp.where(lane < LANE_A2, jnp.exp(cs), jnp.exp(total - cs)))
        wide = jnp.zeros((t, 3 * SSD_WIDTH), F32)
        for part in _split_bf16(per_head, 2):
            wide = wide + jnp.dot(part, spread_ref[...], preferred_element_type=F32)
        dt_x = wide[:, :SSD_WIDTH]
        decay_in = wide[:, SSD_WIDTH:2 * SSD_WIDTH]
        decay_out = wide[:, 2 * SSD_WIDTH:]
        decay_chunk = decay_in[t - 1:t, :]

        xs = xs_ref[0, rows, :]
        xdt = xs * dt_x
        ys = []
        for g in range(SSD_GROUPS):
            gs = slice(g * GROUP_WIDTH, (g + 1) * GROUP_WIDTH)
            ns = slice(g * SSD_STATE, (g + 1) * SSD_STATE)
            bg = bm_ref[0, rows, ns]
            cg = cm_ref[0, rows, ns]
            cb = lax.dot_general(cg, bg, (((1,), (1,)), ((), ())), preferred_element_type=F32)
            xg = xdt[:, gs]
            m_parts, x_parts = [], []
            for e in range(SSD_HPG):
                hd = g * SSD_HPG + e
                col = cs[:, LANE_A + hd:LANE_A + hd + 1]
                row = cs_t[LANE_A + hd:LANE_A + hd + 1, :]
                l_mat = jnp.exp(jnp.where(lower, col - row, -jnp.inf))
                m_parts.append((cb * l_mat).astype(BF16))
                x_parts.append(jnp.where(head_of_lane == e, xg, 0.0).astype(BF16))
            m_cat = jnp.concatenate(m_parts, axis=1)
            x_blk = jnp.concatenate(x_parts, axis=0)
            y_diag = jnp.dot(m_cat, x_blk, preferred_element_type=F32)

            state = state_scr[g]
            y_off = jnp.dot(cg, state.astype(BF16), preferred_element_type=F32) * decay_in[:, gs]
            xd = (xg * decay_out[:, gs]).astype(BF16)
            upd = lax.dot_general(bg, xd, (((0,), (0,)), ((), ())), preferred_element_type=F32)
            state_scr[g] = state * decay_chunk[:, gs] + upd
            ys.append(y_diag + y_off + dskip_ref[:, gs] * xs[:, gs])

        z = z_ref[0, rows, :]
        for g in range(SSD_GROUPS):
            gs = slice(g * GROUP_WIDTH, (g + 1) * GROUP_WIDTH)
            yg = ys[g] * _silu(z[:, gs])
            yg = yg * lax.rsqrt(jnp.mean(yg * yg, axis=-1, keepdims=True) + RMS_EPS)
            o_ref[0, rows, gs] = (yg * ng_ref[:, gs]).astype(BF16)


def _head_spread():
    src = jnp.arange(LANES)[:, None]
    col = jnp.arange(3 * SSD_WIDTH)[None, :]
    first_lane = jnp.array([LANE_DT, LANE_A, LANE_A2])[col // SSD_WIDTH]
    return (src == first_lane + (col % SSD_WIDTH) // SSD_HEAD_DIM).astype(BF16)


def _ssd(xs, bm, cm, z, small, d_skip_x, norm_g):
    b, seq, _ = xs.shape
    t = min(SSD_T, seq)
    rows = min(SSD_ROWS, seq)
    assert seq % rows == 0 and rows % t == 0
    tok = lambda width: pl.BlockSpec((1, rows, width), lambda bi, c: (bi, c, 0))
    const2 = lambda shape: pl.BlockSpec(shape, lambda bi, c: (0, 0))
    return pl.pallas_call(
        functools.partial(_ssd_kernel, t=t, chunks=rows // t),
        grid=(b, seq // rows),
        in_specs=[tok(SSD_WIDTH), tok(SSD_BC), tok(SSD_BC), tok(SSD_WIDTH), tok(LANES),
                  const2((1, SSD_WIDTH)), const2((1, SSD_WIDTH)),
                  const2((t, t)), const2((LANES, 3 * SSD_WIDTH))],
        out_specs=tok(SSD_WIDTH),
        out_shape=jax.ShapeDtypeStruct((b, seq, SSD_WIDTH), BF16),
        scratch_shapes=[pltpu.VMEM((SSD_GROUPS, SSD_STATE, GROUP_WIDTH), F32)],
        compiler_params=_params("arbitrary", "arbitrary"),
    )(xs, bm, cm, z, small, d_skip_x, norm_g, _lower_ones(t), _head_spread())


def _fox_kernel(q_ref, k_ref, v_ref, o_ref, m_scr, acc_scr, *, t):
    qi = pl.program_id(1)
    lane_head = lax.broadcasted_iota(jnp.int32, (t, LANES), 1) // FOX_HEAD_DIM
    row_i = lax.broadcasted_iota(jnp.int32, (t, t), 0)
    col_i = lax.broadcasted_iota(jnp.int32, (t, t), 1)
    causal = row_i >= col_i
    heads_per_vreg = LANES // FOX_HEAD_DIM

    def block(start, width, masked):
        ks = pl.ds(start, width)
        group = lambda hd: slice(hd * LANES, (hd + 1) * LANES)

        def logits(hd):
            s = lax.dot_general(q_ref[0, :, group(hd)], k_ref[0, ks, group(hd)],
                                (((1,), (1,)), ((), ())), preferred_element_type=F32)
            return jnp.where(causal, s, -jnp.inf) if masked else s

        def update(hd, s):
            m_prev = m_scr[hd]
            m_new = jnp.maximum(m_prev, jnp.max(s, axis=-1, keepdims=True))
            p = jnp.exp2(s - jnp.concatenate([m_new] * (width // LANES), axis=1)).astype(BF16)
            acc_scr[hd] = jnp.exp2(m_prev - m_new) * acc_scr[hd] + jnp.dot(
                p, v_ref[0, ks, group(hd)], preferred_element_type=F32)
            m_scr[hd] = m_new

        s_next = logits(0)
        for hd in range(FOX_HEADS):
            s_cur = s_next
            if hd + 1 < FOX_HEADS:
                s_next = logits(hd + 1)
            update(hd, s_cur)

    m_scr[...] = jnp.full(m_scr.shape, -jnp.inf, F32)
    acc_scr[...] = jnp.zeros(acc_scr.shape, F32)

    wide = 2 * t
    assert k_ref.shape[1] % wide == 0

    @pl.loop(0, qi // 2)
    def _(pj):
        block(pl.multiple_of(pj * wide, wide), wide, False)

    @pl.when(qi % 2 == 1)
    def _():
        block(pl.multiple_of((qi - 1) * t, t), t, False)

    block(pl.multiple_of(qi * t, t), t, True)
    for pair in range(FOX_HEADS // heads_per_vreg):
        out = jnp.zeros((t, LANES), F32)
        for hh in range(heads_per_vreg):
            acc = acc_scr[pair * heads_per_vreg + hh]
            ones_lane = (1 - hh) * FOX_HEAD_DIM
            row_sum = acc[:, ones_lane:ones_lane + 1]
            out = out + jnp.where(lane_head == hh, acc, 0.0) / row_sum
        o_ref[0, :, pair * LANES:(pair + 1) * LANES] = out.astype(BF16)


def _fox(qa, ka, va):
    b, seq, _ = qa.shape
    t = min(FOX_T, seq)
    assert seq % t == 0
    return pl.pallas_call(
        functools.partial(_fox_kernel, t=t),
        grid=(b, seq // t),
        in_specs=[
            pl.BlockSpec((1, t, FOX_GROUP), lambda bi, i: (bi, i, 0)),
            pl.BlockSpec((1, seq, FOX_GROUP), lambda bi, i: (bi, 0, 0)),
            pl.BlockSpec((1, seq, FOX_GROUP), lambda bi, i: (bi, 0, 0)),
        ],
        out_specs=pl.BlockSpec((1, t, FOX_WIDTH), lambda bi, i: (bi, i, 0)),
        out_shape=jax.ShapeDtypeStruct((b, seq, FOX_WIDTH), BF16),
        scratch_shapes=[pltpu.VMEM((FOX_HEADS, t, LANES), F32), pltpu.VMEM((FOX_HEADS, t, LANES), F32)],
        compiler_params=_params("arbitrary", "arbitrary"),
    )(qa, ka, va)


def _lower_ones(n):
    return jnp.tril(jnp.ones((n, n), BF16))


def _spread_forget(f):
    lead = f.shape[:-1]
    order = sorted(range(FOX_HEADS), key=_forget_lane)
    blocks, lane = [], 0
    for hd in order:
        blocks.append(jnp.zeros(lead + (_forget_lane(hd) - lane,), f.dtype))
        blocks.append(jnp.broadcast_to(f[..., hd:hd + 1], lead + (FOX_SLOT,)))
        lane = _forget_lane(hd) + FOX_SLOT
    blocks.append(jnp.zeros(lead + (LANES - lane,), f.dtype))
    return jnp.concatenate(blocks, axis=-1)


def _pack_mix_w_in(w):
    ssd_in = SSD_WIDTH + SSD_CONV_DIM + SSD_HEADS
    fox_in = 3 * FOX_WIDTH + FOX_HEADS
    dt_cols = w[..., SSD_WIDTH + SSD_CONV_DIM:ssd_in]
    f_cols = w[..., ssd_in + 3 * FOX_WIDTH:ssd_in + fox_in]
    pad = jnp.zeros(w.shape[:-1] + (LANES - 3 * SSD_HEADS,), w.dtype)
    return jnp.concatenate([
        w[..., :SSD_WIDTH + SSD_CONV_DIM],
        w[..., ssd_in:ssd_in + 3 * FOX_WIDTH],
        w[..., ssd_in + fox_in:],
        dt_cols, dt_cols, dt_cols, pad,
        _spread_forget(f_cols),
    ], axis=-1).astype(BF16)


def _lane_vec(pieces):
    row = jnp.zeros((LANES,), F32)
    for off, vec in pieces:
        row = row.at[off:off + vec.shape[0]].set(vec.astype(F32))
    return row.reshape(1, LANES)


def kernel(x, c, ln_in_g, ln_in_b, ada_w, ada_b, ffn1_w_in, ffn1_w_out, mix_w_in, mix_w_out, ssd_conv_w, ssd_conv_b, ssd_dt_bias, ssd_a_log, ssd_d, ssd_norm_g, fox_f_bias, sconv_w, ffn2_w_in, ffn2_w_out, ln_g, ln_b):
    b, seq, d = x.shape
    depth = ada_w.shape[0]
    mod = _ada(c, ada_w, ada_b)
    mod = mod.reshape(depth, mod.shape[1], 3 * N_SUB, d)

    w1_in, w1_out, w2_in, w2_out = ffn1_w_in, ffn1_w_out, ffn2_w_in, ffn2_w_out
    wm_in, wm_out = _pack_mix_w_in(mix_w_in), mix_w_out.astype(BF16)

    x2d = x.reshape(b * seq, d)
    for l in range(depth):
        mod_l = mod[l]
        x2d = _ffn(x2d, mod_l, w1_in, w1_out, ln_g[l, 0], ln_b[l, 0], layer=l, sub=0, seq=seq,
                   pre=(ln_in_g, ln_in_b) if l == 0 else None)

        small_bias = _lane_vec([(lane, ssd_dt_bias[l]) for lane in (LANE_DT, LANE_A, LANE_A2)])
        alog_vec = _lane_vec([(LANE_A, ssd_a_log[l]), (LANE_A2, ssd_a_log[l])])
        z, xs, bm, cm, qa, ka, va, y_sc, small = _proj(
            x2d.reshape(b, seq, d), mod_l, wm_in,
            ssd_conv_w[l], ssd_conv_b[l].reshape(1, SSD_CONV_DIM), small_bias, alog_vec, sconv_w[l],
            _spread_forget(fox_f_bias[l]).reshape(1, LANES), layer=l)
        d_skip_x = jnp.repeat(ssd_d[l], SSD_HEAD_DIM).reshape(1, SSD_WIDTH)
        y_ssd = _ssd(xs, bm, cm, z, small, d_skip_x, ssd_norm_g[l].reshape(1, SSD_WIDTH))
        y_fox = _fox(qa, ka, va)
        mixer_tail = (y_ssd.reshape(b * seq, SSD_WIDTH), y_fox.reshape(b * seq, FOX_WIDTH),
                      y_sc.reshape(b * seq, SCONV_WIDTH), wm_out, ln_g[l, 1], ln_b[l, 1])
        x2d = _ffn(x2d, mod_l, w2_in, w2_out, ln_g[l, 2], ln_b[l, 2], layer=l, sub=2, seq=seq,
                   mix=mixer_tail)
    return x2d.reshape(b, seq, d)
```

```python
import functools
import math

import jax
import jax.numpy as jnp
from jax import lax
from jax.experimental import pallas as pl
from jax.experimental.pallas import tpu as pltpu

F32 = jnp.float32
BF16 = jnp.bfloat16

D_MODEL = 1024
DEPTH = 2
N_SUB = 3
D_FF = 2816
SSD_WIDTH = 512
SSD_HEAD_DIM = 64
SSD_HEADS = 8
SSD_GROUPS = 2
SSD_HPG = SSD_HEADS // SSD_GROUPS
SSD_STATE = 128
SSD_CONV = 4
SSD_BC = SSD_GROUPS * SSD_STATE
SSD_CONV_DIM = SSD_WIDTH + 2 * SSD_BC
GROUP_WIDTH = SSD_WIDTH // SSD_GROUPS
FOX_WIDTH = 256
FOX_HEAD_DIM = 64
FOX_HEADS = 4
SCONV_WIDTH = 256
SCONV_K = 3
ALPHA = (2 * DEPTH) ** 0.25
LN_EPS = 1e-5
RMS_EPS = 1e-5
LOG2_E = math.log2(math.e)

LANES = 128
SUBLANES = 8
VMEM_LIMIT = 56 * 1024 * 1024

RAW_DT = SSD_WIDTH + SSD_CONV_DIM
RAW_QKV = RAW_DT + SSD_HEADS
RAW_F = RAW_QKV + 3 * FOX_WIDTH
RAW_SC = RAW_F + FOX_HEADS
RAW_WIDTH = RAW_SC + 3 * SCONV_WIDTH
COL_Z = 0
COL_XBC = COL_Z + SSD_WIDTH
COL_QKV = COL_XBC + SSD_CONV_DIM
COL_SC = COL_QKV + 3 * FOX_WIDTH
COL_SMALL = COL_SC + 3 * SCONV_WIDTH
COL_F = COL_SMALL + LANES
D_PROJ = COL_F + LANES
LANE_DT = 0
LANE_A = 8
LANE_A2 = 16
FOX_PIECES = 3
FOX_SLOT = 2 * FOX_PIECES
FOX_GROUP = FOX_HEADS * LANES
HEADS_PER_VREG = LANES // FOX_HEAD_DIM


def _forget_lane(hd):
    pair, hh = divmod(hd, HEADS_PER_VREG)
    return (HEADS_PER_VREG - 1 - hh) * FOX_HEAD_DIM + pair * FOX_SLOT

FFN_TM = 1024
FFN_TM_MIX = 512
FFN_TF = 256
FFN_HALF = 512
FFN_W_IN_ROWS = 64
FFN_W_OUT_ROWS = 256
PROJ_TM = 512
PROJ_CUM = 128
SSD_T = 256
SSD_ROWS = 512
SSD_SPREAD_PIECES = 2
FOX_T = 512
ADA_TN = 1024


def _layer_norm(x, g, b):
    mu = jnp.mean(x, axis=-1, keepdims=True)
    xc = x - mu
    var = jnp.mean(xc * xc, axis=-1, keepdims=True)
    return xc * lax.rsqrt(var + LN_EPS) * g + b


def _silu(x):
    return x / (1.0 + jnp.exp2(x * (-LOG2_E)))


def _softplus(x):
    return jnp.maximum(x, 0.0) + jnp.log1p(jnp.exp(-jnp.abs(x)))


def _split_bf16(x, pieces):
    out = []
    for _ in range(pieces):
        part = x.astype(BF16)
        out.append(part)
        x = x - part.astype(F32)
    return out


def _params(*semantics):
    return pltpu.CompilerParams(dimension_semantics=semantics, vmem_limit_bytes=VMEM_LIMIT)


def _ada_kernel(c_ref, w_ref, b_ref, o_ref, *, batch):
    act_hi, act_lo = _split_bf16(_silu(c_ref[...]), 2)
    row = lax.broadcasted_iota(jnp.int32, act_hi.shape, 0)
    lhs = jnp.where(row < batch, act_hi, act_lo)
    w_hi, w_lo = _split_bf16(w_ref[...], 2)
    y = jnp.dot(lhs, w_hi, preferred_element_type=F32) + jnp.dot(lhs, w_lo, preferred_element_type=F32)
    rows = y.shape[0]
    o_ref[...] = y + pltpu.roll(y, rows - batch, axis=0) + b_ref[...]


def _ada(c, ada_w, ada_b):
    depth, d, n = ada_w.shape
    batch = c.shape[0]
    rows = -(-2 * batch // SUBLANES) * SUBLANES
    c_pad = jnp.pad(jnp.concatenate([c, c], axis=0), ((0, rows - 2 * batch), (0, 0)))
    return pl.pallas_call(
        functools.partial(_ada_kernel, batch=batch),
        grid=(depth, n // ADA_TN),
        in_specs=[
            pl.BlockSpec((rows, d), lambda l, j: (0, 0)),
            pl.BlockSpec((None, d, ADA_TN), lambda l, j: (l, 0, j)),
            pl.BlockSpec((None, 1, ADA_TN), lambda l, j: (l, 0, j)),
        ],
        out_specs=pl.BlockSpec((None, rows, ADA_TN), lambda l, j: (l, 0, j)),
        out_shape=jax.ShapeDtypeStruct((depth, rows, n), F32),
        compiler_params=_params("arbitrary", "arbitrary"),
    )(c_pad, ada_w, ada_b.reshape(depth, 1, n))


def _ffn_kernel(*refs, layer, sub, pre_ln, mix):
    x_ref, mod_ref, wi_hbm, wo_hbm, lng_ref, lnb_ref = refs[:6]
    n_scratch = 6
    extra = list(refs[6:-1 - n_scratch])
    o_ref = refs[-1 - n_scratch]
    act_scr, wi_ref, wo_ref, stage_i, stage_o, sem = refs[-n_scratch:]
    if pre_ln:
        ing_ref, inb_ref = extra[:2]
        extra = extra[2:]
    if mix:
        yssd_ref, yfox_ref, ysc_ref, wm_ref, mlng_ref, mlnb_ref = extra
    shift = mod_ref[3 * sub:3 * sub + 1, :]
    scale = mod_ref[3 * sub + 1:3 * sub + 2, :]
    gain = mod_ref[3 * sub + 2:3 * sub + 3, :]
    halves = [slice(r, r + FFN_HALF) for r in range(0, x_ref.shape[0], FFN_HALF)]

    def entry(rows):
        x = x_ref[rows, :]
        if pre_ln:
            x = _layer_norm(x, ing_ref[...], inb_ref[...])
        if mix:
            y = jnp.dot(yssd_ref[rows, :], wm_ref[0:SSD_WIDTH, :], preferred_element_type=F32)
            y = y + jnp.dot(yfox_ref[rows, :], wm_ref[SSD_WIDTH:SSD_WIDTH + FOX_WIDTH, :],
                            preferred_element_type=F32)
            y = y + jnp.dot(ysc_ref[rows, :], wm_ref[SSD_WIDTH + FOX_WIDTH:, :],
                            preferred_element_type=F32)
            x = _layer_norm(ALPHA * x + mod_ref[5:6, :] * y, mlng_ref[...], mlnb_ref[...])
        return x

    @pl.when(pl.program_id(0) == 0)
    def _():
        def stream(src_hbm, dst_ref, stage, which, block_rows):
            n_blocks = src_hbm.shape[1] // block_rows
            copy = lambda b: pltpu.make_async_copy(
                src_hbm.at[layer, b * block_rows:(b + 1) * block_rows, :], stage.at[b % 2],
                sem.at[which, b % 2])
            copy(0).start()
            for b in range(n_blocks):
                if b + 1 < n_blocks:
                    copy(b + 1).start()
                copy(b).wait()
                dst_ref[b * block_rows:(b + 1) * block_rows, :] = stage[b % 2].astype(BF16)

        stream(wi_hbm, wi_ref, stage_i, 0, FFN_W_IN_ROWS)
        stream(wo_hbm, wo_ref, stage_o, 1, FFN_W_OUT_ROWS)

    xs = [entry(rows) for rows in halves]
    ys = []
    for rows, x in zip(halves, xs):
        h = (x * (1.0 + scale) + shift).astype(BF16)
        for c in range(D_FF // FFN_TF):
            cols = slice(c * FFN_TF, (c + 1) * FFN_TF)
            gate = jnp.dot(h, wi_ref[:, cols], preferred_element_type=F32)
            up = jnp.dot(h, wi_ref[:, D_FF + c * FFN_TF:D_FF + (c + 1) * FFN_TF],
                         preferred_element_type=F32)
            act_scr[rows, cols] = (_silu(gate) * up).astype(BF16)
        ys.append(jnp.dot(act_scr[rows, :], wo_ref[...], preferred_element_type=F32))
    for rows, x, y in zip(halves, xs, ys):
        o_ref[rows, :] = _layer_norm(ALPHA * x + (0.5 * gain) * y, lng_ref[...], lnb_ref[...])


def _ffn(x2d, mod_l, w_in, w_out, ln_g, ln_b, *, layer, sub, seq, pre=None, mix=None):
    n_tok, d = x2d.shape
    tm = min(FFN_TM if mix is None else FFN_TM_MIX, seq)
    assert seq % tm == 0 and n_tok % seq == 0 and D_FF % FFN_TF == 0 and tm % FFN_HALF == 0
    tiles_per_seq = seq // tm
    row = lambda v: v.reshape(1, d)
    resident = lambda shape: pl.BlockSpec(shape, lambda i: (0, 0), pipeline_mode=pl.Buffered(1))
    weight = lambda shape: pl.BlockSpec((None,) + shape, lambda i: (layer, 0, 0),
                                        pipeline_mode=pl.Buffered(1))
    in_specs = [
        pl.BlockSpec((tm, d), lambda i: (i, 0)),
        pl.BlockSpec((None, 3 * N_SUB, d), lambda i: (i // tiles_per_seq, 0, 0)),
        pl.BlockSpec(memory_space=pl.ANY),
        pl.BlockSpec(memory_space=pl.ANY),
        resident((1, d)),
        resident((1, d)),
    ]
    args = [x2d, mod_l, w_in, w_out, row(ln_g), row(ln_b)]
    if pre is not None:
        in_specs += [resident((1, d))] * 2
        args += [row(pre[0]), row(pre[1])]
    if mix is not None:
        y_ssd, y_fox, y_sc, w_mix, mix_g, mix_b = mix
        tok = lambda width: pl.BlockSpec((tm, width), lambda i: (i, 0))
        in_specs += [tok(SSD_WIDTH), tok(FOX_WIDTH), tok(SCONV_WIDTH), weight((d, d)),
                     resident((1, d)), resident((1, d))]
        args += [y_ssd, y_fox, y_sc, w_mix, row(mix_g), row(mix_b)]
    return pl.pallas_call(
        functools.partial(_ffn_kernel, layer=layer, sub=sub, pre_ln=pre is not None, mix=mix is not None),
        grid=(n_tok // tm,),
        in_specs=in_specs,
        out_specs=pl.BlockSpec((tm, d), lambda i: (i, 0)),
        out_shape=jax.ShapeDtypeStruct((n_tok, d), F32),
        scratch_shapes=[
            pltpu.VMEM((tm, D_FF), BF16),
            pltpu.VMEM((d, 2 * D_FF), BF16),
            pltpu.VMEM((D_FF, d), BF16),
            pltpu.VMEM((2, FFN_W_IN_ROWS, 2 * D_FF), F32),
            pltpu.VMEM((2, FFN_W_OUT_ROWS, d), F32),
            pltpu.SemaphoreType.DMA((2, 2)),
        ],
        compiler_params=_params("arbitrary"),
    )(*args)


def _shift_lanes_left(blocks, shift, n_out):
    lane = lax.broadcasted_iota(jnp.int32, blocks[0].shape, 1)
    rolled = [pltpu.roll(blk, LANES - shift, axis=1) for blk in blocks]
    return [jnp.where(lane < LANES - shift, rolled[k], rolled[k + 1]) for k in range(n_out)]


def _proj_kernel(x_ref, mod_ref, wraw_ref, wtail_ref, wsmall_ref, cw_ref, cb_ref, sbias_ref, alog_ref,
                 scw_ref, fbias_ref, tri_ref,
                 z_ref, xs_ref, bm_ref, cm_ref, q_ref, k_ref, v_ref, ysc_ref, small_ref,
                 w_ref, xbc_scr, u_scr, carry_scr, *, tm):
    t = pl.program_id(1)
    halo = SUBLANES

    @pl.when((pl.program_id(0) == 0) & (t == 0))
    def _():
        raw_block = lambda m: wraw_ref[:, m * LANES:(m + 1) * LANES]
        w_ref[:, :COL_QKV] = wraw_ref[:, :COL_QKV].astype(BF16)
        for raw_start, col, width in ((RAW_QKV, COL_QKV, 3 * FOX_WIDTH), (RAW_SC, COL_SC, 3 * SCONV_WIDTH)):
            first, shift = divmod(raw_start, LANES)
            n_out = width // LANES
            blocks = [raw_block(m) if (m + 1) * LANES <= RAW_WIDTH else wtail_ref[...]
                      for m in range(first, first + n_out + 1)]
            for k, blk in enumerate(_shift_lanes_left(blocks, shift, n_out)):
                w_ref[:, col + k * LANES:col + (k + 1) * LANES] = blk.astype(BF16)
        w_ref[:, COL_SMALL:] = wsmall_ref[...]

    @pl.when(t == 0)
    def _():
        xbc_scr[0:halo, :] = jnp.zeros((halo, SSD_CONV_DIM), F32)
        u_scr[0:halo, :] = jnp.zeros((halo, SCONV_WIDTH), F32)
        carry_scr[...] = jnp.zeros_like(carry_scr)

    @pl.when(t > 0)
    def _():
        xbc_scr[0:halo, :] = xbc_scr[tm:tm + halo, :]
        u_scr[0:halo, :] = u_scr[tm:tm + halo, :]

    shift = mod_ref[3:4, :]
    scale = mod_ref[4:5, :]
    h = (x_ref[0] * (1.0 + scale) + shift).astype(BF16)

    def seg(start, width):
        return jnp.dot(h, w_ref[:, start:start + width], preferred_element_type=F32)

    xbc_scr[halo:halo + tm, :] = seg(COL_XBC, SSD_CONV_DIM)
    conv = cb_ref[...] + cw_ref[SSD_CONV - 1:SSD_CONV, :] * xbc_scr[halo:halo + tm, :]
    for k in range(SSD_CONV - 1):
        back = SSD_CONV - 1 - k
        conv = conv + cw_ref[k:k + 1, :] * xbc_scr[halo - back:halo - back + tm, :]
    xbc = _silu(conv)
    xs_ref[0] = xbc[:, :SSD_WIDTH]
    bm_ref[0] = xbc[:, SSD_WIDTH:SSD_WIDTH + SSD_BC].astype(BF16)
    cm_ref[0] = xbc[:, SSD_WIDTH + SSD_BC:].astype(BF16)

    sc = seg(COL_SC, 3 * SCONV_WIDTH)
    u_scr[halo:halo + tm, :] = sc[:, SCONV_WIDTH:2 * SCONV_WIDTH] * sc[:, 2 * SCONV_WIDTH:]
    cu = scw_ref[SCONV_K - 1:SCONV_K, :] * u_scr[halo:halo + tm, :]
    for k in range(SCONV_K - 1):
        back = SCONV_K - 1 - k
        cu = cu + scw_ref[k:k + 1, :] * u_scr[halo - back:halo - back + tm, :]
    ysc_ref[0] = (sc[:, :SCONV_WIDTH] * cu).astype(BF16)

    lane = lax.broadcasted_iota(jnp.int32, (tm, LANES), 1)
    pos = lane % FOX_HEAD_DIM
    used = pos < (FOX_HEADS // HEADS_PER_VREG) * FOX_SLOT
    small_f = seg(COL_SMALL, 2 * LANES)
    log_f = jnp.where(used, -_softplus(-(small_f[:, LANES:] + fbias_ref[...])), 0.0)
    parts = jnp.concatenate(_split_bf16(log_f, 3), axis=1)
    offset = carry_scr[0:1, :]
    blocks = []
    for r in range(0, tm, PROJ_CUM):
        local3 = jnp.dot(tri_ref[...], parts[r:r + PROJ_CUM, :], preferred_element_type=F32)
        local = local3[:, :LANES] + local3[:, LANES:2 * LANES] + local3[:, 2 * LANES:]
        blocks.append(local + offset)
        offset = offset + local[PROJ_CUM - 1:PROJ_CUM, :]
    cum_f = jnp.concatenate(blocks, axis=0)
    carry_scr[...] = jnp.broadcast_to(offset, carry_scr.shape)
    hi, mid, lo = (part.astype(F32) for part in _split_bf16(cum_f * LOG2_E, FOX_PIECES))
    piece = jnp.where(pos % FOX_PIECES == 0, hi, jnp.where(pos % FOX_PIECES == 1, mid, lo))
    in_a = pos % FOX_SLOT < FOX_PIECES
    f_q = jnp.where(in_a, piece, 1.0)
    f_k = jnp.where(in_a, 1.0, -piece)

    qkv = seg(COL_QKV, 3 * FOX_WIDTH)
    q = qkv[:, :FOX_WIDTH] * (FOX_HEAD_DIM ** -0.5 * LOG2_E)
    k = qkv[:, FOX_WIDTH:2 * FOX_WIDTH]
    v = qkv[:, 2 * FOX_WIDTH:]
    for hd in range(FOX_HEADS):
        pair, hh = divmod(hd, HEADS_PER_VREG)
        ps = slice(pair * LANES, (pair + 1) * LANES)
        gs = slice(hd * LANES, (hd + 1) * LANES)
        mine = (lane // FOX_HEAD_DIM) == hh
        first = _forget_lane(hd)
        forget = (lane >= first) & (lane < first + FOX_SLOT)
        ones_col = jnp.where(lane == (1 - hh) * FOX_HEAD_DIM, 1.0, 0.0)
        q_ref[0, :, gs] = jnp.where(mine, q[:, ps], jnp.where(forget, f_q, 0.0)).astype(BF16)
        k_ref[0, :, gs] = jnp.where(mine, k[:, ps], jnp.where(forget, f_k, 0.0)).astype(BF16)
        v_ref[0, :, gs] = jnp.where(mine, v[:, ps], ones_col).astype(BF16)

    dt = _softplus(small_f[:, :LANES] + sbias_ref[...])
    a_neg = -jnp.exp(alog_ref[...])
    small_ref[0] = jnp.where(lane < LANE_A, dt, dt * a_neg)

    z_ref[0] = seg(COL_Z, SSD_WIDTH)


def _proj(x, mod_l, w_raw, w_tail, w_small, conv_w, conv_b, small_bias, alog_vec, sconv_w, f_bias, *, layer):
    b, seq, d = x.shape
    tm = min(PROJ_TM, seq)
    assert seq % tm == 0
    nt = seq // tm
    tok = lambda width: pl.BlockSpec((1, tm, width), lambda bi, t: (bi, t, 0))
    const2 = lambda shape: pl.BlockSpec(shape, lambda bi, t: (0, 0))
    out_shapes = (
        jax.ShapeDtypeStruct((b, seq, SSD_WIDTH), F32),
        jax.ShapeDtypeStruct((b, seq, SSD_WIDTH), F32),
        jax.ShapeDtypeStruct((b, seq, SSD_BC), BF16),
        jax.ShapeDtypeStruct((b, seq, SSD_BC), BF16),
        jax.ShapeDtypeStruct((b, seq, FOX_GROUP), BF16),
        jax.ShapeDtypeStruct((b, seq, FOX_GROUP), BF16),
        jax.ShapeDtypeStruct((b, seq, FOX_GROUP), BF16),
        jax.ShapeDtypeStruct((b, seq, SCONV_WIDTH), BF16),
        jax.ShapeDtypeStruct((b, seq, LANES), F32),
    )
    out_specs = (
        tok(SSD_WIDTH), tok(SSD_WIDTH), tok(SSD_BC), tok(SSD_BC),
        tok(FOX_GROUP), tok(FOX_GROUP), tok(FOX_GROUP), tok(SCONV_WIDTH), tok(LANES),
    )
    return pl.pallas_call(
        functools.partial(_proj_kernel, tm=tm),
        grid=(b, nt),
        in_specs=[
            tok(d),
            pl.BlockSpec((None, 3 * N_SUB, d), lambda bi, t: (bi, 0, 0)),
            pl.BlockSpec((None, d, RAW_WIDTH), lambda bi, t: (layer, 0, 0), pipeline_mode=pl.Buffered(1)),
            pl.BlockSpec((None, d, LANES), lambda bi, t: (layer, 0, 0), pipeline_mode=pl.Buffered(1)),
            pl.BlockSpec((None, d, 2 * LANES), lambda bi, t: (layer, 0, 0), pipeline_mode=pl.Buffered(1)),
            const2((SSD_CONV, SSD_CONV_DIM)),
            const2((1, SSD_CONV_DIM)),
            const2((1, LANES)),
            const2((1, LANES)),
            const2((SCONV_K, SCONV_WIDTH)),
            const2((1, LANES)),
            const2((PROJ_CUM, PROJ_CUM)),
        ],
        out_specs=out_specs,
        out_shape=out_shapes,
        scratch_shapes=[
            pltpu.VMEM((d, D_PROJ), BF16),
            pltpu.VMEM((tm + SUBLANES, SSD_CONV_DIM), F32),
            pltpu.VMEM((tm + SUBLANES, SCONV_WIDTH), F32),
            pltpu.VMEM((SUBLANES, LANES), F32),
        ],
        compiler_params=_params("arbitrary", "arbitrary"),
    )(x, mod_l, w_raw, w_tail, w_small, conv_w, conv_b, small_bias, alog_vec, sconv_w, f_bias,
      _lower_ones(PROJ_CUM))


def _ssd_kernel(xs_ref, bm_ref, cm_ref, z_ref, small_ref, dskip_ref, ng_ref, tri_ref, spread_ref,
                o_ref, state_scr, *, t, chunks):
    @pl.when(pl.program_id(1) == 0)
    def _():
        state_scr[...] = jnp.zeros_like(state_scr)

    lane = lax.broadcasted_iota(jnp.int32, (t, LANES), 1)
    row_i = lax.broadcasted_iota(jnp.int32, (t, t), 0)
    col_i = lax.broadcasted_iota(jnp.int32, (t, t), 1)
    lower = row_i >= col_i
    head_of_lane = lax.broadcasted_iota(jnp.int32, (t, GROUP_WIDTH), 1) // SSD_HEAD_DIM

    for ci in range(chunks):
        rows = slice(ci * t, (ci + 1) * t)
        small = small_ref[0, rows, :]
        a_only = jnp.where((lane >= LANE_A) & (lane < LANE_A2 + SSD_HEADS), small, 0.0)
        cs3 = jnp.dot(tri_ref[...], jnp.concatenate(_split_bf16(a_only, 3), axis=1),
                      preferred_element_type=F32)
        cs = cs3[:, :LANES] + cs3[:, LANES:2 * LANES] + cs3[:, 2 * LANES:]
        cs_t = cs.T
        total = cs[t - 1:t, :]
        per_head = jnp.where(lane < LANE_A, small,
                             jnp.where(lane < LANE_A2, jnp.exp(cs), jnp.exp(total - cs)))
        wide = jnp.dot(jnp.concatenate(_split_bf16(per_head, SSD_SPREAD_PIECES), axis=1), spread_ref[...],
                       preferred_element_type=F32)
        dt_x = wide[:, :SSD_WIDTH]
        decay_in = wide[:, SSD_WIDTH:2 * SSD_WIDTH]
        decay_out = wide[:, 2 * SSD_WIDTH:]
        decay_chunk = decay_in[t - 1:t, :]

        xs = xs_ref[0, rows, :]
        xdt = xs * dt_x
        ys = []
        for g in range(SSD_GROUPS):
            gs = slice(g * GROUP_WIDTH, (g + 1) * GROUP_WIDTH)
            ns = slice(g * SSD_STATE, (g + 1) * SSD_STATE)
            bg = bm_ref[0, rows, ns]
            cg = cm_ref[0, rows, ns]
            cb = lax.dot_general(cg, bg, (((1,), (1,)), ((), ())), preferred_element_type=F32)
            xg = xdt[:, gs]
            m_parts, x_parts = [], []
            for e in range(SSD_HPG):
                hd = g * SSD_HPG + e
                col = cs[:, LANE_A + hd:LANE_A + hd + 1]
                row = cs_t[LANE_A + hd:LANE_A + hd + 1, :]
                l_mat = jnp.exp(jnp.where(lower, col - row, -jnp.inf))
                m_parts.append((cb * l_mat).astype(BF16))
                x_parts.append(jnp.where(head_of_lane == e, xg, 0.0).astype(BF16))
            m_cat = jnp.concatenate(m_parts, axis=1)
            x_blk = jnp.concatenate(x_parts, axis=0)
            y_diag = jnp.dot(m_cat, x_blk, preferred_element_type=F32)

            state = state_scr[g]
            y_off = jnp.dot(cg, state.astype(BF16), preferred_element_type=F32) * decay_in[:, gs]
            xd = (xg * decay_out[:, gs]).astype(BF16)
            upd = lax.dot_general(bg, xd, (((0,), (0,)), ((), ())), preferred_element_type=F32)
            state_scr[g] = state * decay_chunk[:, gs] + upd
            ys.append(y_diag + y_off + dskip_ref[:, gs] * xs[:, gs])

        z = z_ref[0, rows, :]
        for g in range(SSD_GROUPS):
            gs = slice(g * GROUP_WIDTH, (g + 1) * GROUP_WIDTH)
            yg = ys[g] * _silu(z[:, gs])
            yg = yg * lax.rsqrt(jnp.mean(yg * yg, axis=-1, keepdims=True) + RMS_EPS)
            o_ref[0, rows, gs] = (yg * ng_ref[:, gs]).astype(BF16)


def _head_spread():
    src = jnp.arange(SSD_SPREAD_PIECES * LANES)[:, None] % LANES
    col = jnp.arange(3 * SSD_WIDTH)[None, :]
    first_lane = jnp.array([LANE_DT, LANE_A, LANE_A2])[col // SSD_WIDTH]
    return (src == first_lane + (col % SSD_WIDTH) // SSD_HEAD_DIM).astype(BF16)


def _ssd(xs, bm, cm, z, small, d_skip_x, norm_g):
    b, seq, _ = xs.shape
    t = min(SSD_T, seq)
    rows = min(SSD_ROWS, seq)
    assert seq % rows == 0 and rows % t == 0
    tok = lambda width: pl.BlockSpec((1, rows, width), lambda bi, c: (bi, c, 0))
    const2 = lambda shape: pl.BlockSpec(shape, lambda bi, c: (0, 0))
    return pl.pallas_call(
        functools.partial(_ssd_kernel, t=t, chunks=rows // t),
        grid=(b, seq // rows),
        in_specs=[tok(SSD_WIDTH), tok(SSD_BC), tok(SSD_BC), tok(SSD_WIDTH), tok(LANES),
                  const2((1, SSD_WIDTH)), const2((1, SSD_WIDTH)),
                  const2((t, t)), const2((SSD_SPREAD_PIECES * LANES, 3 * SSD_WIDTH))],
        out_specs=tok(SSD_WIDTH),
        out_shape=jax.ShapeDtypeStruct((b, seq, SSD_WIDTH), BF16),
        scratch_shapes=[pltpu.VMEM((SSD_GROUPS, SSD_STATE, GROUP_WIDTH), F32)],
        compiler_params=_params("arbitrary", "arbitrary"),
    )(xs, bm, cm, z, small, d_skip_x, norm_g, _lower_ones(t), _head_spread())


def _fox_kernel(q_ref, k_ref, v_ref, o_ref, m_scr, acc_scr, *, t):
    qi = pl.program_id(1)
    lane_head = lax.broadcasted_iota(jnp.int32, (t, LANES), 1) // FOX_HEAD_DIM
    row_i = lax.broadcasted_iota(jnp.int32, (t, t), 0)
    col_i = lax.broadcasted_iota(jnp.int32, (t, t), 1)
    causal = row_i >= col_i
    heads_per_vreg = LANES // FOX_HEAD_DIM

    def block(start, width, masked):
        ks = pl.ds(start, width)
        group = lambda hd: slice(hd * LANES, (hd + 1) * LANES)

        def logits(hd):
            s = lax.dot_general(q_ref[0, :, group(hd)], k_ref[0, ks, group(hd)],
                                (((1,), (1,)), ((), ())), preferred_element_type=F32)
            return jnp.where(causal, s, -jnp.inf) if masked else s

        def update(hd, s):
            m_prev = m_scr[hd]
            m_new = jnp.maximum(m_prev, jnp.max(s, axis=-1, keepdims=True))
            p = jnp.exp2(s - jnp.concatenate([m_new] * (width // LANES), axis=1)).astype(BF16)
            acc_scr[hd] = jnp.exp2(m_prev - m_new) * acc_scr[hd] + jnp.dot(
                p, v_ref[0, ks, group(hd)], preferred_element_type=F32)
            m_scr[hd] = m_new

        s_next = logits(0)
        for hd in range(FOX_HEADS):
            s_cur = s_next
            if hd + 1 < FOX_HEADS:
                s_next = logits(hd + 1)
            update(hd, s_cur)

    m_scr[...] = jnp.full(m_scr.shape, -jnp.inf, F32)
    acc_scr[...] = jnp.zeros(acc_scr.shape, F32)

    wide = 2 * t
    assert k_ref.shape[1] % wide == 0

    @pl.loop(0, qi // 2)
    def _(pj):
        block(pl.multiple_of(pj * wide, wide), wide, False)

    @pl.when(qi % 2 == 1)
    def _():
        block(pl.multiple_of((qi - 1) * t, t), t, False)

    block(pl.multiple_of(qi * t, t), t, True)
    for pair in range(FOX_HEADS // heads_per_vreg):
        out = jnp.zeros((t, LANES), F32)
        for hh in range(heads_per_vreg):
            acc = acc_scr[pair * heads_per_vreg + hh]
            ones_lane = (1 - hh) * FOX_HEAD_DIM
            row_sum = acc[:, ones_lane:ones_lane + 1]
            out = out + jnp.where(lane_head == hh, acc, 0.0) / row_sum
        o_ref[0, :, pair * LANES:(pair + 1) * LANES] = out.astype(BF16)


def _fox(qa, ka, va):
    b, seq, _ = qa.shape
    t = min(FOX_T, seq)
    assert seq % t == 0
    return pl.pallas_call(
        functools.partial(_fox_kernel, t=t),
        grid=(b, seq // t),
        in_specs=[
            pl.BlockSpec((1, t, FOX_GROUP), lambda bi, i: (bi, i, 0)),
            pl.BlockSpec((1, seq, FOX_GROUP), lambda bi, i: (bi, 0, 0)),
            pl.BlockSpec((1, seq, FOX_GROUP), lambda bi, i: (bi, 0, 0)),
        ],
        out_specs=pl.BlockSpec((1, t, FOX_WIDTH), lambda bi, i: (bi, i, 0)),
        out_shape=jax.ShapeDtypeStruct((b, seq, FOX_WIDTH), BF16),
        scratch_shapes=[pltpu.VMEM((FOX_HEADS, t, LANES), F32), pltpu.VMEM((FOX_HEADS, t, LANES), F32)],
        compiler_params=_params("arbitrary", "arbitrary"),
    )(qa, ka, va)


def _lower_ones(n):
    return jnp.tril(jnp.ones((n, n), BF16))


def _spread_forget(f):
    lead = f.shape[:-1]
    order = sorted(range(FOX_HEADS), key=_forget_lane)
    blocks, lane = [], 0
    for hd in order:
        blocks.append(jnp.zeros(lead + (_forget_lane(hd) - lane,), f.dtype))
        blocks.append(jnp.broadcast_to(f[..., hd:hd + 1], lead + (FOX_SLOT,)))
        lane = _forget_lane(hd) + FOX_SLOT
    blocks.append(jnp.zeros(lead + (LANES - lane,), f.dtype))
    return jnp.concatenate(blocks, axis=-1)


def _mix_w_in_extras(w):
    tail_start = RAW_WIDTH // LANES * LANES
    tail = jnp.pad(w[..., tail_start:], ((0, 0), (0, 0), (0, LANES - (RAW_WIDTH - tail_start))))
    dt_cols = w[..., RAW_DT:RAW_QKV]
    pad = jnp.zeros(w.shape[:-1] + (LANES - 3 * SSD_HEADS,), w.dtype)
    small = jnp.concatenate([dt_cols, dt_cols, dt_cols, pad, _spread_forget(w[..., RAW_F:RAW_SC])], axis=-1)
    return tail, small.astype(BF16)


def _lane_vec(pieces):
    row = jnp.zeros((LANES,), F32)
    for off, vec in pieces:
        row = row.at[off:off + vec.shape[0]].set(vec.astype(F32))
    return row.reshape(1, LANES)


def kernel(x, c, ln_in_g, ln_in_b, ada_w, ada_b, ffn1_w_in, ffn1_w_out, mix_w_in, mix_w_out, ssd_conv_w, ssd_conv_b, ssd_dt_bias, ssd_a_log, ssd_d, ssd_norm_g, fox_f_bias, sconv_w, ffn2_w_in, ffn2_w_out, ln_g, ln_b):
    b, seq, d = x.shape
    depth = ada_w.shape[0]
    mod = _ada(c, ada_w, ada_b)
    mod = mod.reshape(depth, mod.shape[1], 3 * N_SUB, d)

    w1_in, w1_out, w2_in, w2_out = ffn1_w_in, ffn1_w_out, ffn2_w_in, ffn2_w_out
    wm_tail, wm_small = _mix_w_in_extras(mix_w_in)
    wm_out = mix_w_out.astype(BF16)

    x2d = x.reshape(b * seq, d)
    for l in range(depth):
        mod_l = mod[l]
        x2d = _ffn(x2d, mod_l, w1_in, w1_out, ln_g[l, 0], ln_b[l, 0], layer=l, sub=0, seq=seq,
                   pre=(ln_in_g, ln_in_b) if l == 0 else None)

        small_bias = _lane_vec([(lane, ssd_dt_bias[l]) for lane in (LANE_DT, LANE_A, LANE_A2)])
        alog_vec = _lane_vec([(LANE_A, ssd_a_log[l]), (LANE_A2, ssd_a_log[l])])
        z, xs, bm, cm, qa, ka, va, y_sc, small = _proj(
            x2d.reshape(b, seq, d), mod_l, mix_w_in, wm_tail, wm_small,
            ssd_conv_w[l], ssd_conv_b[l].reshape(1, SSD_CONV_DIM), small_bias, alog_vec, sconv_w[l],
            _spread_forget(fox_f_bias[l]).reshape(1, LANES), layer=l)
        d_skip_x = jnp.repeat(ssd_d[l], SSD_HEAD_DIM).reshape(1, SSD_WIDTH)
        y_ssd = _ssd(xs, bm, cm, z, small, d_skip_x, ssd_norm_g[l].reshape(1, SSD_WIDTH))
        y_fox = _fox(qa, ka, va)
        mixer_tail = (y_ssd.reshape(b * seq, SSD_WIDTH), y_fox.reshape(b * seq, FOX_WIDTH),
                      y_sc.reshape(b * seq, SCONV_WIDTH), wm_out, ln_g[l, 1], ln_b[l, 1])
        x2d = _ffn(x2d, mod_l, w2_in, w2_out, ln_g[l, 2], ln_b[l, 2], layer=l, sub=2, seq=seq,
                   mix=mixer_tail)
    return x2d.reshape(b, seq, d)
```

```python
import functools
import math

import jax
import jax.numpy as jnp
from jax import lax
from jax.experimental import pallas as pl
from jax.experimental.pallas import tpu as pltpu

F32 = jnp.float32
BF16 = jnp.bfloat16

D_MODEL = 1024
DEPTH = 2
N_SUB = 3
D_FF = 2816
SSD_WIDTH = 512
SSD_HEAD_DIM = 64
SSD_HEADS = 8
SSD_GROUPS = 2
SSD_HPG = SSD_HEADS // SSD_GROUPS
SSD_STATE = 128
SSD_CONV = 4
SSD_BC = SSD_GROUPS * SSD_STATE
SSD_CONV_DIM = SSD_WIDTH + 2 * SSD_BC
GROUP_WIDTH = SSD_WIDTH // SSD_GROUPS
FOX_WIDTH = 256
FOX_HEAD_DIM = 64
FOX_HEADS = 4
SCONV_WIDTH = 256
SCONV_K = 3
ALPHA = (2 * DEPTH) ** 0.25
LN_EPS = 1e-5
RMS_EPS = 1e-5
LOG2_E = math.log2(math.e)

LANES = 128
SUBLANES = 8
VMEM_LIMIT = 56 * 1024 * 1024

RAW_DT = SSD_WIDTH + SSD_CONV_DIM
RAW_QKV = RAW_DT + SSD_HEADS
RAW_F = RAW_QKV + 3 * FOX_WIDTH
RAW_SC = RAW_F + FOX_HEADS
RAW_WIDTH = RAW_SC + 3 * SCONV_WIDTH
COL_Z = 0
COL_XBC = COL_Z + SSD_WIDTH
COL_QKV = COL_XBC + SSD_CONV_DIM
COL_SC = COL_QKV + 3 * FOX_WIDTH
COL_SMALL = COL_SC + 3 * SCONV_WIDTH
COL_F = COL_SMALL + LANES
D_PROJ = COL_F + LANES
LANE_DT = 0
LANE_A = 8
LANE_A2 = 16
FOX_PIECES = 3
FOX_SLOT = 2 * FOX_PIECES
FOX_GROUP = FOX_HEADS * LANES
HEADS_PER_VREG = LANES // FOX_HEAD_DIM


def _forget_lane(hd):
    pair, hh = divmod(hd, HEADS_PER_VREG)
    return (HEADS_PER_VREG - 1 - hh) * FOX_HEAD_DIM + pair * FOX_SLOT

FFN_TM = 1024
FFN_TM_MIX = 512
FFN_TF = 256
FFN_HALF = 512
FFN_W_IN_ROWS = 64
FFN_W_OUT_ROWS = 256
PROJ_TM = 512
PROJ_CUM = 128
SSD_T = 256
SSD_ROWS = 512
SSD_SPREAD_PIECES = 2
FOX_T = 512
ADA_TN = 1024


def _layer_norm(x, g, b):
    mu = jnp.mean(x, axis=-1, keepdims=True)
    xc = x - mu
    var = jnp.mean(xc * xc, axis=-1, keepdims=True)
    return xc * lax.rsqrt(var + LN_EPS) * g + b


def _silu(x):
    return x / (1.0 + jnp.exp2(x * (-LOG2_E)))


def _softplus(x):
    return jnp.maximum(x, 0.0) + jnp.log1p(jnp.exp(-jnp.abs(x)))


def _split_bf16(x, pieces):
    out = []
    for _ in range(pieces):
        part = x.astype(BF16)
        out.append(part)
        x = x - part.astype(F32)
    return out


def _params(*semantics):
    return pltpu.CompilerParams(dimension_semantics=semantics, vmem_limit_bytes=VMEM_LIMIT)


def _ada_kernel(c_ref, w_ref, b_ref, o_ref, *, batch):
    act_hi, act_lo = _split_bf16(_silu(c_ref[...]), 2)
    row = lax.broadcasted_iota(jnp.int32, act_hi.shape, 0)
    lhs = jnp.where(row < batch, act_hi, act_lo)
    w_hi, w_lo = _split_bf16(w_ref[...], 2)
    y = jnp.dot(lhs, w_hi, preferred_element_type=F32) + jnp.dot(lhs, w_lo, preferred_element_type=F32)
    rows = y.shape[0]
    o_ref[...] = y + pltpu.roll(y, rows - batch, axis=0) + b_ref[...]


def _ada(c, ada_w, ada_b):
    depth, d, n = ada_w.shape
    batch = c.shape[0]
    rows = -(-2 * batch // SUBLANES) * SUBLANES
    c_pad = jnp.pad(jnp.concatenate([c, c], axis=0), ((0, rows - 2 * batch), (0, 0)))
    return pl.pallas_call(
        functools.partial(_ada_kernel, batch=batch),
        grid=(depth, n // ADA_TN),
        in_specs=[
            pl.BlockSpec((rows, d), lambda l, j: (0, 0)),
            pl.BlockSpec((None, d, ADA_TN), lambda l, j: (l, 0, j)),
            pl.BlockSpec((None, 1, ADA_TN), lambda l, j: (l, 0, j)),
        ],
        out_specs=pl.BlockSpec((None, rows, ADA_TN), lambda l, j: (l, 0, j)),
        out_shape=jax.ShapeDtypeStruct((depth, rows, n), F32),
        compiler_params=_params("arbitrary", "arbitrary"),
    )(c_pad, ada_w, ada_b.reshape(depth, 1, n))


def _ffn_kernel(*refs, layer, sub, pre_ln, mix):
    x_ref, mod_ref, wi_hbm, wo_hbm, lng_ref, lnb_ref = refs[:6]
    n_scratch = 6
    extra = list(refs[6:-1 - n_scratch])
    o_ref = refs[-1 - n_scratch]
    act_scr, wi_ref, wo_ref, stage_i, stage_o, sem = refs[-n_scratch:]
    if pre_ln:
        ing_ref, inb_ref = extra[:2]
        extra = extra[2:]
    if mix:
        yssd_ref, yfox_ref, ysc_ref, wm_ref, mlng_ref, mlnb_ref = extra
    shift = mod_ref[3 * sub:3 * sub + 1, :]
    scale = mod_ref[3 * sub + 1:3 * sub + 2, :]
    gain = mod_ref[3 * sub + 2:3 * sub + 3, :]
    halves = [slice(r, r + FFN_HALF) for r in range(0, x_ref.shape[0], FFN_HALF)]

    def entry(rows):
        x = x_ref[rows, :]
        if pre_ln:
            x = _layer_norm(x, ing_ref[...], inb_ref[...])
        if mix:
            y = jnp.dot(yssd_ref[rows, :], wm_ref[0:SSD_WIDTH, :], preferred_element_type=F32)
            y = y + jnp.dot(yfox_ref[rows, :], wm_ref[SSD_WIDTH:SSD_WIDTH + FOX_WIDTH, :],
                            preferred_element_type=F32)
            y = y + jnp.dot(ysc_ref[rows, :], wm_ref[SSD_WIDTH + FOX_WIDTH:, :],
                            preferred_element_type=F32)
            x = _layer_norm(ALPHA * x + mod_ref[5:6, :] * y, mlng_ref[...], mlnb_ref[...])
        return x

    @pl.when(pl.program_id(0) == 0)
    def _():
        def stream(src_hbm, dst_ref, stage, which, block_rows):
            n_blocks = src_hbm.shape[1] // block_rows
            copy = lambda b: pltpu.make_async_copy(
                src_hbm.at[layer, b * block_rows:(b + 1) * block_rows, :], stage.at[b % 2],
                sem.at[which, b % 2])
            copy(0).start()
            for b in range(n_blocks):
                if b + 1 < n_blocks:
                    copy(b + 1).start()
                copy(b).wait()
                dst_ref[b * block_rows:(b + 1) * block_rows, :] = stage[b % 2].astype(BF16)

        stream(wi_hbm, wi_ref, stage_i, 0, FFN_W_IN_ROWS)
        stream(wo_hbm, wo_ref, stage_o, 1, FFN_W_OUT_ROWS)

    xs = [entry(rows) for rows in halves]
    ys = []
    for rows, x in zip(halves, xs):
        h = (x * (1.0 + scale) + shift).astype(BF16)
        for c in range(D_FF // FFN_TF):
            cols = slice(c * FFN_TF, (c + 1) * FFN_TF)
            gate = jnp.dot(h, wi_ref[:, cols], preferred_element_type=F32)
            up = jnp.dot(h, wi_ref[:, D_FF + c * FFN_TF:D_FF + (c + 1) * FFN_TF],
                         preferred_element_type=F32)
            act_scr[rows, cols] = (_silu(gate) * up).astype(BF16)
        ys.append(jnp.dot(act_scr[rows, :], wo_ref[...], preferred_element_type=F32))
    for rows, x, y in zip(halves, xs, ys):
        o_ref[rows, :] = _layer_norm(ALPHA * x + (0.5 * gain) * y, lng_ref[...], lnb_ref[...])


def _ffn(x2d, mod_l, w_in, w_out, ln_g, ln_b, *, layer, sub, seq, pre=None, mix=None):
    n_tok, d = x2d.shape
    tm = min(FFN_TM if mix is None else FFN_TM_MIX, seq)
    assert seq % tm == 0 and n_tok % seq == 0 and D_FF % FFN_TF == 0 and tm % FFN_HALF == 0
    tiles_per_seq = seq // tm
    row = lambda v: v.reshape(1, d)
    resident = lambda shape: pl.BlockSpec(shape, lambda i: (0, 0), pipeline_mode=pl.Buffered(1))
    weight = lambda shape: pl.BlockSpec((None,) + shape, lambda i: (layer, 0, 0),
                                        pipeline_mode=pl.Buffered(1))
    in_specs = [
        pl.BlockSpec((tm, d), lambda i: (i, 0)),
        pl.BlockSpec((None, 3 * N_SUB, d), lambda i: (i // tiles_per_seq, 0, 0)),
        pl.BlockSpec(memory_space=pl.ANY),
        pl.BlockSpec(memory_space=pl.ANY),
        resident((1, d)),
        resident((1, d)),
    ]
    args = [x2d, mod_l, w_in, w_out, row(ln_g), row(ln_b)]
    if pre is not None:
        in_specs += [resident((1, d))] * 2
        args += [row(pre[0]), row(pre[1])]
    if mix is not None:
        y_ssd, y_fox, y_sc, w_mix, mix_g, mix_b = mix
        tok = lambda width: pl.BlockSpec((tm, width), lambda i: (i, 0))
        in_specs += [tok(SSD_WIDTH), tok(FOX_WIDTH), tok(SCONV_WIDTH), weight((d, d)),
                     resident((1, d)), resident((1, d))]
        args += [y_ssd, y_fox, y_sc, w_mix, row(mix_g), row(mix_b)]
    return pl.pallas_call(
        functools.partial(_ffn_kernel, layer=layer, sub=sub, pre_ln=pre is not None, mix=mix is not None),
        grid=(n_tok // tm,),
        in_specs=in_specs,
        out_specs=pl.BlockSpec((tm, d), lambda i: (i, 0)),
        out_shape=jax.ShapeDtypeStruct((n_tok, d), F32),
        scratch_shapes=[
            pltpu.VMEM((tm, D_FF), BF16),
            pltpu.VMEM((d, 2 * D_FF), BF16),
            pltpu.VMEM((D_FF, d), BF16),
            pltpu.VMEM((2, FFN_W_IN_ROWS, 2 * D_FF), F32),
            pltpu.VMEM((2, FFN_W_OUT_ROWS, d), F32),
            pltpu.SemaphoreType.DMA((2, 2)),
        ],
        compiler_params=_params("arbitrary"),
    )(*args)


def _proj_kernel(wt_ref, x_ref, mod_ref, cw_ref, cb_ref, sbias_ref, alog_ref, scw_ref, fbias_ref, tri_ref,
                 z_ref, xs_ref, bm_ref, cm_ref, q_ref, k_ref, v_ref, ysc_ref, small_ref,
                 w_ref, pad_scr, xbc_scr, u_scr, carry_scr, *, tm):
    t = pl.program_id(1)
    halo = SUBLANES

    @pl.when((pl.program_id(0) == 0) & (t == 0))
    def _():
        w_ref[:COL_QKV, :] = wt_ref[:COL_QKV, :].astype(BF16)
        w_ref[COL_QKV:COL_SC, :] = wt_ref[RAW_QKV:RAW_F, :].astype(BF16)
        w_ref[COL_SC:COL_SMALL, :] = wt_ref[RAW_SC:, :].astype(BF16)
        pad_scr[...] = jnp.zeros_like(pad_scr)
        for lane0 in (LANE_DT, LANE_A, LANE_A2):
            pad_scr[lane0:lane0 + SSD_HEADS, :] = wt_ref[RAW_DT:RAW_QKV, :]
        for hd in range(FOX_HEADS):
            lane0 = LANES + _forget_lane(hd)
            pad_scr[lane0:lane0 + FOX_SLOT, :] = jnp.broadcast_to(
                wt_ref[RAW_F + hd:RAW_F + hd + 1, :], (FOX_SLOT, wt_ref.shape[1]))
        w_ref[COL_SMALL:, :] = pad_scr[...].astype(BF16)

    @pl.when(t == 0)
    def _():
        xbc_scr[0:halo, :] = jnp.zeros((halo, SSD_CONV_DIM), F32)
        u_scr[0:halo, :] = jnp.zeros((halo, SCONV_WIDTH), F32)
        carry_scr[...] = jnp.zeros_like(carry_scr)

    @pl.when(t > 0)
    def _():
        xbc_scr[0:halo, :] = xbc_scr[tm:tm + halo, :]
        u_scr[0:halo, :] = u_scr[tm:tm + halo, :]

    shift = mod_ref[3:4, :]
    scale = mod_ref[4:5, :]
    h = (x_ref[0] * (1.0 + scale) + shift).astype(BF16)

    def seg(start, width):
        return lax.dot_general(h, w_ref[start:start + width, :], (((1,), (1,)), ((), ())),
                               preferred_element_type=F32)

    xbc_scr[halo:halo + tm, :] = seg(COL_XBC, SSD_CONV_DIM)
    conv = cb_ref[...] + cw_ref[SSD_CONV - 1:SSD_CONV, :] * xbc_scr[halo:halo + tm, :]
    for k in range(SSD_CONV - 1):
        back = SSD_CONV - 1 - k
        conv = conv + cw_ref[k:k + 1, :] * xbc_scr[halo - back:halo - back + tm, :]
    xbc = _silu(conv)
    xs_ref[0] = xbc[:, :SSD_WIDTH]
    bm_ref[0] = xbc[:, SSD_WIDTH:SSD_WIDTH + SSD_BC].astype(BF16)
    cm_ref[0] = xbc[:, SSD_WIDTH + SSD_BC:].astype(BF16)

    sc = seg(COL_SC, 3 * SCONV_WIDTH)
    u_scr[halo:halo + tm, :] = sc[:, SCONV_WIDTH:2 * SCONV_WIDTH] * sc[:, 2 * SCONV_WIDTH:]
    cu = scw_ref[SCONV_K - 1:SCONV_K, :] * u_scr[halo:halo + tm, :]
    for k in range(SCONV_K - 1):
        back = SCONV_K - 1 - k
        cu = cu + scw_ref[k:k + 1, :] * u_scr[halo - back:halo - back + tm, :]
    ysc_ref[0] = (sc[:, :SCONV_WIDTH] * cu).astype(BF16)

    lane = lax.broadcasted_iota(jnp.int32, (tm, LANES), 1)
    pos = lane % FOX_HEAD_DIM
    used = pos < (FOX_HEADS // HEADS_PER_VREG) * FOX_SLOT
    small_f = seg(COL_SMALL, 2 * LANES)
    log_f = jnp.where(used, -_softplus(-(small_f[:, LANES:] + fbias_ref[...])), 0.0)
    parts = jnp.concatenate(_split_bf16(log_f, 3), axis=1)
    offset = carry_scr[0:1, :]
    blocks = []
    for r in range(0, tm, PROJ_CUM):
        local3 = jnp.dot(tri_ref[...], parts[r:r + PROJ_CUM, :], preferred_element_type=F32)
        local = local3[:, :LANES] + local3[:, LANES:2 * LANES] + local3[:, 2 * LANES:]
        blocks.append(local + offset)
        offset = offset + local[PROJ_CUM - 1:PROJ_CUM, :]
    cum_f = jnp.concatenate(blocks, axis=0)
    carry_scr[...] = jnp.broadcast_to(offset, carry_scr.shape)
    hi, mid, lo = (part.astype(F32) for part in _split_bf16(cum_f * LOG2_E, FOX_PIECES))
    piece = jnp.where(pos % FOX_PIECES == 0, hi, jnp.where(pos % FOX_PIECES == 1, mid, lo))
    in_a = pos % FOX_SLOT < FOX_PIECES
    f_q = jnp.where(in_a, piece, 1.0)
    f_k = jnp.where(in_a, 1.0, -piece)

    qkv = seg(COL_QKV, 3 * FOX_WIDTH)
    q = qkv[:, :FOX_WIDTH] * (FOX_HEAD_DIM ** -0.5 * LOG2_E)
    k = qkv[:, FOX_WIDTH:2 * FOX_WIDTH]
    v = qkv[:, 2 * FOX_WIDTH:]
    for hd in range(FOX_HEADS):
        pair, hh = divmod(hd, HEADS_PER_VREG)
        ps = slice(pair * LANES, (pair + 1) * LANES)
        gs = slice(hd * LANES, (hd + 1) * LANES)
        mine = (lane // FOX_HEAD_DIM) == hh
        first = _forget_lane(hd)
        forget = (lane >= first) & (lane < first + FOX_SLOT)
        ones_col = jnp.where(lane == (1 - hh) * FOX_HEAD_DIM, 1.0, 0.0)
        q_ref[0, :, gs] = jnp.where(mine, q[:, ps], jnp.where(forget, f_q, 0.0)).astype(BF16)
        k_ref[0, :, gs] = jnp.where(mine, k[:, ps], jnp.where(forget, f_k, 0.0)).astype(BF16)
        v_ref[0, :, gs] = jnp.where(mine, v[:, ps], ones_col).astype(BF16)

    dt = _softplus(small_f[:, :LANES] + sbias_ref[...])
    a_neg = -jnp.exp(alog_ref[...])
    small_ref[0] = jnp.where(lane < LANE_A, dt, dt * a_neg)

    z_ref[0] = seg(COL_Z, SSD_WIDTH)


def _proj(w_t, x, mod_l, conv_w, conv_b, small_bias, alog_vec, sconv_w, f_bias, *, layer):
    b, seq, d = x.shape
    tm = min(PROJ_TM, seq)
    assert seq % tm == 0
    nt = seq // tm
    tok = lambda width: pl.BlockSpec((1, tm, width), lambda bi, t: (bi, t, 0))
    const2 = lambda shape: pl.BlockSpec(shape, lambda bi, t: (0, 0))
    out_shapes = (
        jax.ShapeDtypeStruct((b, seq, SSD_WIDTH), F32),
        jax.ShapeDtypeStruct((b, seq, SSD_WIDTH), F32),
        jax.ShapeDtypeStruct((b, seq, SSD_BC), BF16),
        jax.ShapeDtypeStruct((b, seq, SSD_BC), BF16),
        jax.ShapeDtypeStruct((b, seq, FOX_GROUP), BF16),
        jax.ShapeDtypeStruct((b, seq, FOX_GROUP), BF16),
        jax.ShapeDtypeStruct((b, seq, FOX_GROUP), BF16),
        jax.ShapeDtypeStruct((b, seq, SCONV_WIDTH), BF16),
        jax.ShapeDtypeStruct((b, seq, LANES), F32),
    )
    out_specs = (
        tok(SSD_WIDTH), tok(SSD_WIDTH), tok(SSD_BC), tok(SSD_BC),
        tok(FOX_GROUP), tok(FOX_GROUP), tok(FOX_GROUP), tok(SCONV_WIDTH), tok(LANES),
    )
    return pl.pallas_call(
        functools.partial(_proj_kernel, tm=tm),
        grid=(b, nt),
        in_specs=[
            pl.BlockSpec((RAW_WIDTH, d), lambda bi, t: (0, layer), pipeline_mode=pl.Buffered(1)),
            tok(d),
            pl.BlockSpec((None, 3 * N_SUB, d), lambda bi, t: (bi, 0, 0)),
            const2((SSD_CONV, SSD_CONV_DIM)),
            const2((1, SSD_CONV_DIM)),
            const2((1, LANES)),
            const2((1, LANES)),
            const2((SCONV_K, SCONV_WIDTH)),
            const2((1, LANES)),
            const2((PROJ_CUM, PROJ_CUM)),
        ],
        out_specs=out_specs,
        out_shape=out_shapes,
        scratch_shapes=[
            pltpu.VMEM((D_PROJ, d), BF16),
            pltpu.VMEM((D_PROJ - COL_SMALL, d), F32),
            pltpu.VMEM((tm + SUBLANES, SSD_CONV_DIM), F32),
            pltpu.VMEM((tm + SUBLANES, SCONV_WIDTH), F32),
            pltpu.VMEM((SUBLANES, LANES), F32),
        ],
        compiler_params=_params("arbitrary", "arbitrary"),
    )(w_t, x, mod_l, conv_w, conv_b, small_bias, alog_vec, sconv_w, f_bias, _lower_ones(PROJ_CUM))


def _ssd_kernel(xs_ref, bm_ref, cm_ref, z_ref, small_ref, dskip_ref, ng_ref, tri_ref, spread_ref,
                o_ref, state_scr, *, t, chunks):
    @pl.when(pl.program_id(1) == 0)
    def _():
        state_scr[...] = jnp.zeros_like(state_scr)

    lane = lax.broadcasted_iota(jnp.int32, (t, LANES), 1)
    row_i = lax.broadcasted_iota(jnp.int32, (t, t), 0)
    col_i = lax.broadcasted_iota(jnp.int32, (t, t), 1)
    lower = row_i >= col_i
    head_of_lane = lax.broadcasted_iota(jnp.int32, (t, GROUP_WIDTH), 1) // SSD_HEAD_DIM

    for ci in range(chunks):
        rows = slice(ci * t, (ci + 1) * t)
        small = small_ref[0, rows, :]
        a_only = jnp.where((lane >= LANE_A) & (lane < LANE_A2 + SSD_HEADS), small, 0.0)
        cs3 = jnp.dot(tri_ref[...], jnp.concatenate(_split_bf16(a_only, 3), axis=1),
                      preferred_element_type=F32)
        cs = cs3[:, :LANES] + cs3[:, LANES:2 * LANES] + cs3[:, 2 * LANES:]
        cs_t = cs.T
        total = cs[t - 1:t, :]
        per_head = jnp.where(lane < LANE_A, small,
                             jnp.where(lane < LANE_A2, jnp.exp(cs), jnp.exp(total - cs)))
        wide = jnp.dot(jnp.concatenate(_split_bf16(per_head, SSD_SPREAD_PIECES), axis=1), spread_ref[...],
                       preferred_element_type=F32)
        dt_x = wide[:, :SSD_WIDTH]
        decay_in = wide[:, SSD_WIDTH:2 * SSD_WIDTH]
        decay_out = wide[:, 2 * SSD_WIDTH:]
        decay_chunk = decay_in[t - 1:t, :]

        xs = xs_ref[0, rows, :]
        xdt = xs * dt_x
        ys = []
        for g in range(SSD_GROUPS):
            gs = slice(g * GROUP_WIDTH, (g + 1) * GROUP_WIDTH)
            ns = slice(g * SSD_STATE, (g + 1) * SSD_STATE)
            bg = bm_ref[0, rows, ns]
            cg = cm_ref[0, rows, ns]
            cb = lax.dot_general(cg, bg, (((1,), (1,)), ((), ())), preferred_element_type=F32)
            xg = xdt[:, gs]
            m_parts, x_parts = [], []
            for e in range(SSD_HPG):
                hd = g * SSD_HPG + e
                col = cs[:, LANE_A + hd:LANE_A + hd + 1]
                row = cs_t[LANE_A + hd:LANE_A + hd + 1, :]
                l_mat = jnp.exp(jnp.where(lower, col - row, -jnp.inf))
                m_parts.append((cb * l_mat).astype(BF16))
                x_parts.append(jnp.where(head_of_lane == e, xg, 0.0).astype(BF16))
            m_cat = jnp.concatenate(m_parts, axis=1)
            x_blk = jnp.concatenate(x_parts, axis=0)
            y_diag = jnp.dot(m_cat, x_blk, preferred_element_type=F32)

            state = state_scr[g]
            y_off = jnp.dot(cg, state.astype(BF16), preferred_element_type=F32) * decay_in[:, gs]
            xd = (xg * decay_out[:, gs]).astype(BF16)
            upd = lax.dot_general(bg, xd, (((0,), (0,)), ((), ())), preferred_element_type=F32)
            state_scr[g] = state * decay_chunk[:, gs] + upd
            ys.append(y_diag + y_off + dskip_ref[:, gs] * xs[:, gs])

        z = z_ref[0, rows, :]
        for g in range(SSD_GROUPS):
            gs = slice(g * GROUP_WIDTH, (g + 1) * GROUP_WIDTH)
            yg = ys[g] * _silu(z[:, gs])
            yg = yg * lax.rsqrt(jnp.mean(yg * yg, axis=-1, keepdims=True) + RMS_EPS)
            o_ref[0, rows, gs] = (yg * ng_ref[:, gs]).astype(BF16)


def _head_spread():
    src = jnp.arange(SSD_SPREAD_PIECES * LANES)[:, None] % LANES
    col = jnp.arange(3 * SSD_WIDTH)[None, :]
    first_lane = jnp.array([LANE_DT, LANE_A, LANE_A2])[col // SSD_WIDTH]
    return (src == first_lane + (col % SSD_WIDTH) // SSD_HEAD_DIM).astype(BF16)


def _ssd(xs, bm, cm, z, small, d_skip_x, norm_g):
    b, seq, _ = xs.shape
    t = min(SSD_T, seq)
    rows = min(SSD_ROWS, seq)
    assert seq % rows == 0 and rows % t == 0
    tok = lambda width: pl.BlockSpec((1, rows, width), lambda bi, c: (bi, c, 0))
    const2 = lambda shape: pl.BlockSpec(shape, lambda bi, c: (0, 0))
    return pl.pallas_call(
        functools.partial(_ssd_kernel, t=t, chunks=rows // t),
        grid=(b, seq // rows),
        in_specs=[tok(SSD_WIDTH), tok(SSD_BC), tok(SSD_BC), tok(SSD_WIDTH), tok(LANES),
                  const2((1, SSD_WIDTH)), const2((1, SSD_WIDTH)),
                  const2((t, t)), const2((SSD_SPREAD_PIECES * LANES, 3 * SSD_WIDTH))],
        out_specs=tok(SSD_WIDTH),
        out_shape=jax.ShapeDtypeStruct((b, seq, SSD_WIDTH), BF16),
        scratch_shapes=[pltpu.VMEM((SSD_GROUPS, SSD_STATE, GROUP_WIDTH), F32)],
        compiler_params=_params("arbitrary", "arbitrary"),
    )(xs, bm, cm, z, small, d_skip_x, norm_g, _lower_ones(t), _head_spread())


def _fox_kernel(q_ref, k_ref, v_ref, o_ref, m_scr, acc_scr, *, t):
    qi = pl.program_id(1)
    lane_head = lax.broadcasted_iota(jnp.int32, (t, LANES), 1) // FOX_HEAD_DIM
    row_i = lax.broadcasted_iota(jnp.int32, (t, t), 0)
    col_i = lax.broadcasted_iota(jnp.int32, (t, t), 1)
    causal = row_i >= col_i
    heads_per_vreg = LANES // FOX_HEAD_DIM

    def block(start, width, masked):
        ks = pl.ds(start, width)
        group = lambda hd: slice(hd * LANES, (hd + 1) * LANES)

        def logits(hd):
            s = lax.dot_general(q_ref[0, :, group(hd)], k_ref[0, ks, group(hd)],
                                (((1,), (1,)), ((), ())), preferred_element_type=F32)
            return jnp.where(causal, s, -jnp.inf) if masked else s

        def update(hd, s):
            m_prev = m_scr[hd]
            m_new = jnp.maximum(m_prev, jnp.max(s, axis=-1, keepdims=True))
            p = jnp.exp2(s - jnp.concatenate([m_new] * (width // LANES), axis=1)).astype(BF16)
            acc_scr[hd] = jnp.exp2(m_prev - m_new) * acc_scr[hd] + jnp.dot(
                p, v_ref[0, ks, group(hd)], preferred_element_type=F32)
            m_scr[hd] = m_new

        s_next = logits(0)
        for hd in range(FOX_HEADS):
            s_cur = s_next
            if hd + 1 < FOX_HEADS:
                s_next = logits(hd + 1)
            update(hd, s_cur)

    m_scr[...] = jnp.full(m_scr.shape, -jnp.inf, F32)
    acc_scr[...] = jnp.zeros(acc_scr.shape, F32)

    wide = 2 * t
    assert k_ref.shape[1] % wide == 0

    @pl.loop(0, qi // 2)
    def _(pj):
        block(pl.multiple_of(pj * wide, wide), wide, False)

    @pl.when(qi % 2 == 1)
    def _():
        block(pl.multiple_of((qi - 1) * t, t), t, False)

    block(pl.multiple_of(qi * t, t), t, True)
    for pair in range(FOX_HEADS // heads_per_vreg):
        out = jnp.zeros((t, LANES), F32)
        for hh in range(heads_per_vreg):
            acc = acc_scr[pair * heads_per_vreg + hh]
            ones_lane = (1 - hh) * FOX_HEAD_DIM
            row_sum = acc[:, ones_lane:ones_lane + 1]
            out = out + jnp.where(lane_head == hh, acc, 0.0) / row_sum
        o_ref[0, :, pair * LANES:(pair + 1) * LANES] = out.astype(BF16)


def _fox(qa, ka, va):
    b, seq, _ = qa.shape
    t = min(FOX_T, seq)
    assert seq % t == 0
    return pl.pallas_call(
        functools.partial(_fox_kernel, t=t),
        grid=(b, seq // t),
        in_specs=[
            pl.BlockSpec((1, t, FOX_GROUP), lambda bi, i: (bi, i, 0)),
            pl.BlockSpec((1, seq, FOX_GROUP), lambda bi, i: (bi, 0, 0)),
            pl.BlockSpec((1, seq, FOX_GROUP), lambda bi, i: (bi, 0, 0)),
        ],
        out_specs=pl.BlockSpec((1, t, FOX_WIDTH), lambda bi, i: (bi, i, 0)),
        out_shape=jax.ShapeDtypeStruct((b, seq, FOX_WIDTH), BF16),
        scratch_shapes=[pltpu.VMEM((FOX_HEADS, t, LANES), F32), pltpu.VMEM((FOX_HEADS, t, LANES), F32)],
        compiler_params=_params("arbitrary", "arbitrary"),
    )(qa, ka, va)


def _lower_ones(n):
    return jnp.tril(jnp.ones((n, n), BF16))


def _spread_forget(f):
    lead = f.shape[:-1]
    order = sorted(range(FOX_HEADS), key=_forget_lane)
    blocks, lane = [], 0
    for hd in order:
        blocks.append(jnp.zeros(lead + (_forget_lane(hd) - lane,), f.dtype))
        blocks.append(jnp.broadcast_to(f[..., hd:hd + 1], lead + (FOX_SLOT,)))
        lane = _forget_lane(hd) + FOX_SLOT
    blocks.append(jnp.zeros(lead + (LANES - lane,), f.dtype))
    return jnp.concatenate(blocks, axis=-1)


def _lane_vec(pieces):
    row = jnp.zeros((LANES,), F32)
    for off, vec in pieces:
        row = row.at[off:off + vec.shape[0]].set(vec.astype(F32))
    return row.reshape(1, LANES)


def kernel(x, c, ln_in_g, ln_in_b, ada_w, ada_b, ffn1_w_in, ffn1_w_out, mix_w_in, mix_w_out, ssd_conv_w, ssd_conv_b, ssd_dt_bias, ssd_a_log, ssd_d, ssd_norm_g, fox_f_bias, sconv_w, ffn2_w_in, ffn2_w_out, ln_g, ln_b):
    b, seq, d = x.shape
    depth = ada_w.shape[0]
    mod = _ada(c, ada_w, ada_b)
    mod = mod.reshape(depth, mod.shape[1], 3 * N_SUB, d)

    w1_in, w1_out, w2_in, w2_out = ffn1_w_in, ffn1_w_out, ffn2_w_in, ffn2_w_out
    wm_in_t = jnp.transpose(mix_w_in, (2, 0, 1)).reshape(mix_w_in.shape[2], depth * d)
    wm_out = mix_w_out.astype(BF16)

    x2d = x.reshape(b * seq, d)
    for l in range(depth):
        mod_l = mod[l]
        x2d = _ffn(x2d, mod_l, w1_in, w1_out, ln_g[l, 0], ln_b[l, 0], layer=l, sub=0, seq=seq,
                   pre=(ln_in_g, ln_in_b) if l == 0 else None)

        small_bias = _lane_vec([(lane, ssd_dt_bias[l]) for lane in (LANE_DT, LANE_A, LANE_A2)])
        alog_vec = _lane_vec([(LANE_A, ssd_a_log[l]), (LANE_A2, ssd_a_log[l])])
        z, xs, bm, cm, qa, ka, va, y_sc, small = _proj(
            wm_in_t, x2d.reshape(b, seq, d), mod_l,
            ssd_conv_w[l], ssd_conv_b[l].reshape(1, SSD_CONV_DIM), small_bias, alog_vec, sconv_w[l],
            _spread_forget(fox_f_bias[l]).reshape(1, LANES), layer=l)
        d_skip_x = jnp.repeat(ssd_d[l], SSD_HEAD_DIM).reshape(1, SSD_WIDTH)
        y_ssd = _ssd(xs, bm, cm, z, small, d_skip_x, ssd_norm_g[l].reshape(1, SSD_WIDTH))
        y_fox = _fox(qa, ka, va)
        mixer_tail = (y_ssd.reshape(b * seq, SSD_WIDTH), y_fox.reshape(b * seq, FOX_WIDTH),
                      y_sc.reshape(b * seq, SCONV_WIDTH), wm_out, ln_g[l, 1], ln_b[l, 1])
        x2d = _ffn(x2d, mod_l, w2_in, w2_out, ln_g[l, 2], ln_b[l, 2], layer=l, sub=2, seq=seq,
                   mix=mixer_tail)
    return x2d.reshape(b, seq, d)
```

```python
import functools
import math

import jax
import jax.numpy as jnp
from jax import lax
from jax.experimental import pallas as pl
from jax.experimental.pallas import tpu as pltpu

F32 = jnp.float32
BF16 = jnp.bfloat16

D_MODEL = 1024
DEPTH = 2
N_SUB = 3
D_FF = 2816
SSD_WIDTH = 512
SSD_HEAD_DIM = 64
SSD_HEADS = 8
SSD_GROUPS = 2
SSD_HPG = SSD_HEADS // SSD_GROUPS
SSD_STATE = 128
SSD_CONV = 4
SSD_BC = SSD_GROUPS * SSD_STATE
SSD_CONV_DIM = SSD_WIDTH + 2 * SSD_BC
GROUP_WIDTH = SSD_WIDTH // SSD_GROUPS
FOX_WIDTH = 256
FOX_HEAD_DIM = 64
FOX_HEADS = 4
SCONV_WIDTH = 256
SCONV_K = 3
ALPHA = (2 * DEPTH) ** 0.25
LN_EPS = 1e-5
RMS_EPS = 1e-5
LOG2_E = math.log2(math.e)

LANES = 128
SUBLANES = 8
VMEM_LIMIT = 56 * 1024 * 1024

RAW_DT = SSD_WIDTH + SSD_CONV_DIM
RAW_QKV = RAW_DT + SSD_HEADS
RAW_F = RAW_QKV + 3 * FOX_WIDTH
RAW_SC = RAW_F + FOX_HEADS
RAW_WIDTH = RAW_SC + 3 * SCONV_WIDTH
COL_Z = 0
COL_XBC = COL_Z + SSD_WIDTH
COL_QKV = COL_XBC + SSD_CONV_DIM
COL_SC = COL_QKV + 3 * FOX_WIDTH
COL_SMALL = COL_SC + 3 * SCONV_WIDTH
COL_F = COL_SMALL + LANES
D_PROJ = COL_F + LANES
LANE_DT = 0
LANE_A = 8
LANE_A2 = 16
FOX_PIECES = 3
FOX_SLOT = 2 * FOX_PIECES
FOX_GROUP = FOX_HEADS * LANES
HEADS_PER_VREG = LANES // FOX_HEAD_DIM


def _forget_lane(hd):
    pair, hh = divmod(hd, HEADS_PER_VREG)
    return (HEADS_PER_VREG - 1 - hh) * FOX_HEAD_DIM + pair * FOX_SLOT

FFN_TM = 1024
FFN_TM_MIX = 1024
FFN_TF = 256
FFN_HALF = 512
FFN_W_IN_ROWS = 32
FFN_W_OUT_ROWS = 128
PROJ_TM = 512
PROJ_CUM = 128
SSD_T = 256
SSD_ROWS = 512
SSD_SPREAD_PIECES = 2
FOX_T = 512
ADA_TN = 2304


def _layer_norm(x, g, b):
    mu = jnp.mean(x, axis=-1, keepdims=True)
    xc = x - mu
    var = jnp.mean(xc * xc, axis=-1, keepdims=True)
    return xc * lax.rsqrt(var + LN_EPS) * g + b


def _silu(x):
    return x / (1.0 + jnp.exp2(x * (-LOG2_E)))


def _softplus(x):
    return jnp.maximum(x, 0.0) + jnp.log1p(jnp.exp(-jnp.abs(x)))


def _split_bf16(x, pieces):
    out = []
    for _ in range(pieces):
        part = x.astype(BF16)
        out.append(part)
        x = x - part.astype(F32)
    return out


def _params(*semantics):
    return pltpu.CompilerParams(dimension_semantics=semantics, vmem_limit_bytes=VMEM_LIMIT)


def _ada_kernel(c_ref, w_ref, b_ref, o_ref, *, batch):
    act_hi, act_lo = _split_bf16(_silu(c_ref[...]), 2)
    row = lax.broadcasted_iota(jnp.int32, act_hi.shape, 0)
    lhs = jnp.where(row < batch, act_hi, act_lo)
    w_hi, w_lo = _split_bf16(w_ref[...], 2)
    y = jnp.dot(lhs, w_hi, preferred_element_type=F32) + jnp.dot(lhs, w_lo, preferred_element_type=F32)
    rows = y.shape[0]
    o_ref[...] = y + pltpu.roll(y, rows - batch, axis=0) + b_ref[...]


def _ada(c, ada_w, ada_b):
    depth, d, n = ada_w.shape
    batch = c.shape[0]
    rows = -(-2 * batch // SUBLANES) * SUBLANES
    c_pad = jnp.pad(jnp.concatenate([c, c], axis=0), ((0, rows - 2 * batch), (0, 0)))
    return pl.pallas_call(
        functools.partial(_ada_kernel, batch=batch),
        grid=(depth, n // ADA_TN),
        in_specs=[
            pl.BlockSpec((rows, d), lambda l, j: (0, 0)),
            pl.BlockSpec((None, d, ADA_TN), lambda l, j: (l, 0, j)),
            pl.BlockSpec((None, 1, ADA_TN), lambda l, j: (l, 0, j)),
        ],
        out_specs=pl.BlockSpec((None, rows, ADA_TN), lambda l, j: (l, 0, j)),
        out_shape=jax.ShapeDtypeStruct((depth, rows, n), F32),
        compiler_params=_params("arbitrary", "arbitrary"),
    )(c_pad, ada_w, ada_b.reshape(depth, 1, n))


def _ffn_kernel(*refs, layer, sub, pre_ln, mix):
    x_ref, mod_ref, wi_hbm, wo_hbm, lng_ref, lnb_ref = refs[:6]
    n_scratch = 6
    extra = list(refs[6:-1 - n_scratch])
    o_ref = refs[-1 - n_scratch]
    act_scr, wi_ref, wo_ref, stage_i, stage_o, sem = refs[-n_scratch:]
    if pre_ln:
        ing_ref, inb_ref = extra[:2]
        extra = extra[2:]
    if mix:
        yssd_ref, yfox_ref, ysc_ref, wm_ref, mlng_ref, mlnb_ref = extra
    shift = mod_ref[3 * sub:3 * sub + 1, :]
    scale = mod_ref[3 * sub + 1:3 * sub + 2, :]
    gain = mod_ref[3 * sub + 2:3 * sub + 3, :]
    halves = [slice(r, r + FFN_HALF) for r in range(0, x_ref.shape[0], FFN_HALF)]

    def entry(rows):
        x = x_ref[rows, :]
        if pre_ln:
            x = _layer_norm(x, ing_ref[...], inb_ref[...])
        if mix:
            y = jnp.dot(yssd_ref[rows, :], wm_ref[0:SSD_WIDTH, :], preferred_element_type=F32)
            y = y + jnp.dot(yfox_ref[rows, :], wm_ref[SSD_WIDTH:SSD_WIDTH + FOX_WIDTH, :],
                            preferred_element_type=F32)
            y = y + jnp.dot(ysc_ref[rows, :], wm_ref[SSD_WIDTH + FOX_WIDTH:, :],
                            preferred_element_type=F32)
            x = _layer_norm(ALPHA * x + mod_ref[5:6, :] * y, mlng_ref[...], mlnb_ref[...])
        return x

    @pl.when(pl.program_id(0) == 0)
    def _():
        def stream(src_hbm, dst_ref, stage, which, block_rows):
            n_blocks = src_hbm.shape[1] // block_rows
            copy = lambda b: pltpu.make_async_copy(
                src_hbm.at[layer, b * block_rows:(b + 1) * block_rows, :], stage.at[b % 2],
                sem.at[which, b % 2])
            copy(0).start()
            for b in range(n_blocks):
                if b + 1 < n_blocks:
                    copy(b + 1).start()
                copy(b).wait()
                dst_ref[b * block_rows:(b + 1) * block_rows, :] = stage[b % 2].astype(BF16)

        stream(wi_hbm, wi_ref, stage_i, 0, FFN_W_IN_ROWS)
        stream(wo_hbm, wo_ref, stage_o, 1, FFN_W_OUT_ROWS)

    xs = [entry(rows) for rows in halves]
    ys = []
    for rows, x in zip(halves, xs):
        h = (x * (1.0 + scale) + shift).astype(BF16)
        for c in range(D_FF // FFN_TF):
            cols = slice(c * FFN_TF, (c + 1) * FFN_TF)
            gate = jnp.dot(h, wi_ref[:, cols], preferred_element_type=F32)
            up = jnp.dot(h, wi_ref[:, D_FF + c * FFN_TF:D_FF + (c + 1) * FFN_TF],
                         preferred_element_type=F32)
            act_scr[rows, cols] = (_silu(gate) * up).astype(BF16)
        ys.append(jnp.dot(act_scr[rows, :], wo_ref[...], preferred_element_type=F32))
    for rows, x, y in zip(halves, xs, ys):
        o_ref[rows, :] = _layer_norm(ALPHA * x + (0.5 * gain) * y, lng_ref[...], lnb_ref[...])


def _ffn(x2d, mod_l, w_in, w_out, ln_g, ln_b, *, layer, sub, seq, pre=None, mix=None):
    n_tok, d = x2d.shape
    tm = min(FFN_TM if mix is None else FFN_TM_MIX, seq)
    assert seq % tm == 0 and n_tok % seq == 0 and D_FF % FFN_TF == 0 and tm % FFN_HALF == 0
    tiles_per_seq = seq // tm
    row = lambda v: v.reshape(1, d)
    resident = lambda shape: pl.BlockSpec(shape, lambda i: (0, 0), pipeline_mode=pl.Buffered(1))
    weight = lambda shape: pl.BlockSpec((None,) + shape, lambda i: (layer, 0, 0),
                                        pipeline_mode=pl.Buffered(1))
    in_specs = [
        pl.BlockSpec((tm, d), lambda i: (i, 0)),
        pl.BlockSpec((None, 3 * N_SUB, d), lambda i: (i // tiles_per_seq, 0, 0)),
        pl.BlockSpec(memory_space=pl.ANY),
        pl.BlockSpec(memory_space=pl.ANY),
        resident((1, d)),
        resident((1, d)),
    ]
    args = [x2d, mod_l, w_in, w_out, row(ln_g), row(ln_b)]
    if pre is not None:
        in_specs += [resident((1, d))] * 2
        args += [row(pre[0]), row(pre[1])]
    if mix is not None:
        y_ssd, y_fox, y_sc, w_mix, mix_g, mix_b = mix
        tok = lambda width: pl.BlockSpec((tm, width), lambda i: (i, 0))
        in_specs += [tok(SSD_WIDTH), tok(FOX_WIDTH), tok(SCONV_WIDTH), weight((d, d)),
                     resident((1, d)), resident((1, d))]
        args += [y_ssd, y_fox, y_sc, w_mix, row(mix_g), row(mix_b)]
    return pl.pallas_call(
        functools.partial(_ffn_kernel, layer=layer, sub=sub, pre_ln=pre is not None, mix=mix is not None),
        grid=(n_tok // tm,),
        in_specs=in_specs,
        out_specs=pl.BlockSpec((tm, d), lambda i: (i, 0)),
        out_shape=jax.ShapeDtypeStruct((n_tok, d), F32),
        scratch_shapes=[
            pltpu.VMEM((tm, D_FF), BF16),
            pltpu.VMEM((d, 2 * D_FF), BF16),
            pltpu.VMEM((D_FF, d), BF16),
            pltpu.VMEM((2, FFN_W_IN_ROWS, 2 * D_FF), F32),
            pltpu.VMEM((2, FFN_W_OUT_ROWS, d), F32),
            pltpu.SemaphoreType.DMA((2, 2)),
        ],
        compiler_params=_params("arbitrary"),
    )(*args)


def _shift_lanes_left(blocks, shift, n_out):
    lane = lax.broadcasted_iota(jnp.int32, blocks[0].shape, 1)
    rolled = [pltpu.roll(blk, LANES - shift, axis=1) for blk in blocks]
    return [jnp.where(lane < LANES - shift, rolled[k], rolled[k + 1]) for k in range(n_out)]


def _proj_kernel(x_ref, mod_ref, wraw_ref, wtail_ref, wsmall_ref, cw_ref, cb_ref, sbias_ref, alog_ref,
                 scw_ref, fbias_ref, tri_ref,
                 z_ref, xs_ref, bm_ref, cm_ref, q_ref, k_ref, v_ref, ysc_ref, small_ref,
                 w_ref, xbc_scr, u_scr, carry_scr, *, tm):
    t = pl.program_id(1)
    halo = SUBLANES

    @pl.when((pl.program_id(0) == 0) & (t == 0))
    def _():
        raw_block = lambda m: wraw_ref[:, m * LANES:(m + 1) * LANES]
        w_ref[:, :COL_QKV] = wraw_ref[:, :COL_QKV].astype(BF16)
        for raw_start, col, width in ((RAW_QKV, COL_QKV, 3 * FOX_WIDTH), (RAW_SC, COL_SC, 3 * SCONV_WIDTH)):
            first, shift = divmod(raw_start, LANES)
            n_out = width // LANES
            blocks = [raw_block(m) if (m + 1) * LANES <= RAW_WIDTH else wtail_ref[...]
                      for m in range(first, first + n_out + 1)]
            for k, blk in enumerate(_shift_lanes_left(blocks, shift, n_out)):
                w_ref[:, col + k * LANES:col + (k + 1) * LANES] = blk.astype(BF16)
        w_ref[:, COL_SMALL:] = wsmall_ref[...]

    @pl.when(t == 0)
    def _():
        xbc_scr[0:halo, :] = jnp.zeros((halo, SSD_CONV_DIM), F32)
        u_scr[0:halo, :] = jnp.zeros((halo, SCONV_WIDTH), F32)
        carry_scr[...] = jnp.zeros_like(carry_scr)

    @pl.when(t > 0)
    def _():
        xbc_scr[0:halo, :] = xbc_scr[tm:tm + halo, :]
        u_scr[0:halo, :] = u_scr[tm:tm + halo, :]

    shift = mod_ref[3:4, :]
    scale = mod_ref[4:5, :]
    h = (x_ref[0] * (1.0 + scale) + shift).astype(BF16)

    def seg(start, width):
        return jnp.dot(h, w_ref[:, start:start + width], preferred_element_type=F32)

    xbc_scr[halo:halo + tm, :] = seg(COL_XBC, SSD_CONV_DIM)
    conv = cb_ref[...] + cw_ref[SSD_CONV - 1:SSD_CONV, :] * xbc_scr[halo:halo + tm, :]
    for k in range(SSD_CONV - 1):
        back = SSD_CONV - 1 - k
        conv = conv + cw_ref[k:k + 1, :] * xbc_scr[halo - back:halo - back + tm, :]
    xbc = _silu(conv)
    xs_ref[0] = xbc[:, :SSD_WIDTH]
    bm_ref[0] = xbc[:, SSD_WIDTH:SSD_WIDTH + SSD_BC].astype(BF16)
    cm_ref[0] = xbc[:, SSD_WIDTH + SSD_BC:].astype(BF16)

    sc = seg(COL_SC, 3 * SCONV_WIDTH)
    u_scr[halo:halo + tm, :] = sc[:, SCONV_WIDTH:2 * SCONV_WIDTH] * sc[:, 2 * SCONV_WIDTH:]
    cu = scw_ref[SCONV_K - 1:SCONV_K, :] * u_scr[halo:halo + tm, :]
    for k in range(SCONV_K - 1):
        back = SCONV_K - 1 - k
        cu = cu + scw_ref[k:k + 1, :] * u_scr[halo - back:halo - back + tm, :]
    ysc_ref[0] = (sc[:, :SCONV_WIDTH] * cu).astype(BF16)

    lane = lax.broadcasted_iota(jnp.int32, (tm, LANES), 1)
    pos = lane % FOX_HEAD_DIM
    used = pos < (FOX_HEADS // HEADS_PER_VREG) * FOX_SLOT
    small_f = seg(COL_SMALL, 2 * LANES)
    log_f = jnp.where(used, -_softplus(-(small_f[:, LANES:] + fbias_ref[...])), 0.0)
    parts = jnp.concatenate(_split_bf16(log_f, 3), axis=1)
    offset = carry_scr[0:1, :]
    blocks = []
    for r in range(0, tm, PROJ_CUM):
        local3 = jnp.dot(tri_ref[...], parts[r:r + PROJ_CUM, :], preferred_element_type=F32)
        local = local3[:, :LANES] + local3[:, LANES:2 * LANES] + local3[:, 2 * LANES:]
        blocks.append(local + offset)
        offset = offset + local[PROJ_CUM - 1:PROJ_CUM, :]
    cum_f = jnp.concatenate(blocks, axis=0)
    carry_scr[...] = jnp.broadcast_to(offset, carry_scr.shape)
    hi, mid, lo = (part.astype(F32) for part in _split_bf16(cum_f * LOG2_E, FOX_PIECES))
    piece = jnp.where(pos % FOX_PIECES == 0, hi, jnp.where(pos % FOX_PIECES == 1, mid, lo))
    in_a = pos % FOX_SLOT < FOX_PIECES
    f_q = jnp.where(in_a, piece, 1.0)
    f_k = jnp.where(in_a, 1.0, -piece)

    qkv = seg(COL_QKV, 3 * FOX_WIDTH)
    q = qkv[:, :FOX_WIDTH] * (FOX_HEAD_DIM ** -0.5 * LOG2_E)
    k = qkv[:, FOX_WIDTH:2 * FOX_WIDTH]
    v = qkv[:, 2 * FOX_WIDTH:]
    for hd in range(FOX_HEADS):
        pair, hh = divmod(hd, HEADS_PER_VREG)
        ps = slice(pair * LANES, (pair + 1) * LANES)
        gs = slice(hd * LANES, (hd + 1) * LANES)
        mine = (lane // FOX_HEAD_DIM) == hh
        first = _forget_lane(hd)
        forget = (lane >= first) & (lane < first + FOX_SLOT)
        ones_col = jnp.where(lane == (1 - hh) * FOX_HEAD_DIM, 1.0, 0.0)
        q_ref[0, :, gs] = jnp.where(mine, q[:, ps], jnp.where(forget, f_q, 0.0)).astype(BF16)
        k_ref[0, :, gs] = jnp.where(mine, k[:, ps], jnp.where(forget, f_k, 0.0)).astype(BF16)
        v_ref[0, :, gs] = jnp.where(mine, v[:, ps], ones_col).astype(BF16)

    dt = _softplus(small_f[:, :LANES] + sbias_ref[...])
    a_neg = -jnp.exp(alog_ref[...])
    small_ref[0] = jnp.where(lane < LANE_A, dt, dt * a_neg)

    z_ref[0] = seg(COL_Z, SSD_WIDTH)


def _proj(x, mod_l, w_raw, w_tail, w_small, conv_w, conv_b, small_bias, alog_vec, sconv_w, f_bias, *, layer):
    b, seq, d = x.shape
    tm = min(PROJ_TM, seq)
    assert seq % tm == 0
    nt = seq // tm
    tok = lambda width: pl.BlockSpec((1, tm, width), lambda bi, t: (bi, t, 0))
    const2 = lambda shape: pl.BlockSpec(shape, lambda bi, t: (0, 0))
    out_shapes = (
        jax.ShapeDtypeStruct((b, seq, SSD_WIDTH), F32),
        jax.ShapeDtypeStruct((b, seq, SSD_WIDTH), F32),
        jax.ShapeDtypeStruct((b, seq, SSD_BC), BF16),
        jax.ShapeDtypeStruct((b, seq, SSD_BC), BF16),
        jax.ShapeDtypeStruct((b, seq, FOX_GROUP), BF16),
        jax.ShapeDtypeStruct((b, seq, FOX_GROUP), BF16),
        jax.ShapeDtypeStruct((b, seq, FOX_GROUP), BF16),
        jax.ShapeDtypeStruct((b, seq, SCONV_WIDTH), BF16),
        jax.ShapeDtypeStruct((b, seq, LANES), F32),
    )
    out_specs = (
        tok(SSD_WIDTH), tok(SSD_WIDTH), tok(SSD_BC), tok(SSD_BC),
        tok(FOX_GROUP), tok(FOX_GROUP), tok(FOX_GROUP), tok(SCONV_WIDTH), tok(LANES),
    )
    return pl.pallas_call(
        functools.partial(_proj_kernel, tm=tm),
        grid=(b, nt),
        in_specs=[
            tok(d),
            pl.BlockSpec((None, 3 * N_SUB, d), lambda bi, t: (bi, 0, 0)),
            pl.BlockSpec((None, d, RAW_WIDTH), lambda bi, t: (layer, 0, 0), pipeline_mode=pl.Buffered(1)),
            pl.BlockSpec((None, d, LANES), lambda bi, t: (layer, 0, 0), pipeline_mode=pl.Buffered(1)),
            pl.BlockSpec((None, d, 2 * LANES), lambda bi, t: (layer, 0, 0), pipeline_mode=pl.Buffered(1)),
            const2((SSD_CONV, SSD_CONV_DIM)),
            const2((1, SSD_CONV_DIM)),
            const2((1, LANES)),
            const2((1, LANES)),
            const2((SCONV_K, SCONV_WIDTH)),
            const2((1, LANES)),
            const2((PROJ_CUM, PROJ_CUM)),
        ],
        out_specs=out_specs,
        out_shape=out_shapes,
        scratch_shapes=[
            pltpu.VMEM((d, D_PROJ), BF16),
            pltpu.VMEM((tm + SUBLANES, SSD_CONV_DIM), F32),
            pltpu.VMEM((tm + SUBLANES, SCONV_WIDTH), F32),
            pltpu.VMEM((SUBLANES, LANES), F32),
        ],
        compiler_params=_params("arbitrary", "arbitrary"),
    )(x, mod_l, w_raw, w_tail, w_small, conv_w, conv_b, small_bias, alog_vec, sconv_w, f_bias,
      _lower_ones(PROJ_CUM))


def _ssd_kernel(xs_ref, bm_ref, cm_ref, z_ref, small_ref, dskip_ref, ng_ref, tri_ref, spread_ref,
                o_ref, state_scr, *, t, chunks):
    @pl.when(pl.program_id(1) == 0)
    def _():
        state_scr[...] = jnp.zeros_like(state_scr)

    lane = lax.broadcasted_iota(jnp.int32, (t, LANES), 1)
    row_i = lax.broadcasted_iota(jnp.int32, (t, t), 0)
    col_i = lax.broadcasted_iota(jnp.int32, (t, t), 1)
    lower = row_i >= col_i
    head_of_lane = lax.broadcasted_iota(jnp.int32, (t, GROUP_WIDTH), 1) // SSD_HEAD_DIM

    for ci in range(chunks):
        rows = slice(ci * t, (ci + 1) * t)
        small = small_ref[0, rows, :]
        a_only = jnp.where((lane >= LANE_A) & (lane < LANE_A2 + SSD_HEADS), small, 0.0)
        cs3 = jnp.dot(tri_ref[...], jnp.concatenate(_split_bf16(a_only, 3), axis=1),
                      preferred_element_type=F32)
        cs = cs3[:, :LANES] + cs3[:, LANES:2 * LANES] + cs3[:, 2 * LANES:]
        cs_t = cs.T
        total = cs[t - 1:t, :]
        per_head = jnp.where(lane < LANE_A, small,
                             jnp.where(lane < LANE_A2, jnp.exp(cs), jnp.exp(total - cs)))
        wide = jnp.dot(jnp.concatenate(_split_bf16(per_head, SSD_SPREAD_PIECES), axis=1), spread_ref[...],
                       preferred_element_type=F32)
        dt_x = wide[:, :SSD_WIDTH]
        decay_in = wide[:, SSD_WIDTH:2 * SSD_WIDTH]
        decay_out = wide[:, 2 * SSD_WIDTH:]
        decay_chunk = decay_in[t - 1:t, :]

        xs = xs_ref[0, rows, :]
        xdt = xs * dt_x
        ys = []
        for g in range(SSD_GROUPS):
            gs = slice(g * GROUP_WIDTH, (g + 1) * GROUP_WIDTH)
            ns = slice(g * SSD_STATE, (g + 1) * SSD_STATE)
            bg = bm_ref[0, rows, ns]
            cg = cm_ref[0, rows, ns]
            cb = lax.dot_general(cg, bg, (((1,), (1,)), ((), ())), preferred_element_type=F32)
            xg = xdt[:, gs]
            m_parts, x_parts = [], []
            for e in range(SSD_HPG):
                hd = g * SSD_HPG + e
                col = cs[:, LANE_A + hd:LANE_A + hd + 1]
                row = cs_t[LANE_A + hd:LANE_A + hd + 1, :]
                l_mat = jnp.exp(jnp.where(lower, col - row, -jnp.inf))
                m_parts.append((cb * l_mat).astype(BF16))
                x_parts.append(jnp.where(head_of_lane == e, xg, 0.0).astype(BF16))
            m_cat = jnp.concatenate(m_parts, axis=1)
            x_blk = jnp.concatenate(x_parts, axis=0)
            y_diag = jnp.dot(m_cat, x_blk, preferred_element_type=F32)

            state = state_scr[g]
            y_off = jnp.dot(cg, state.astype(BF16), preferred_element_type=F32) * decay_in[:, gs]
            xd = (xg * decay_out[:, gs]).astype(BF16)
            upd = lax.dot_general(bg, xd, (((0,), (0,)), ((), ())), preferred_element_type=F32)
            state_scr[g] = state * decay_chunk[:, gs] + upd
            ys.append(y_diag + y_off + dskip_ref[:, gs] * xs[:, gs])

        z = z_ref[0, rows, :]
        for g in range(SSD_GROUPS):
            gs = slice(g * GROUP_WIDTH, (g + 1) * GROUP_WIDTH)
            yg = ys[g] * _silu(z[:, gs])
            yg = yg * lax.rsqrt(jnp.mean(yg * yg, axis=-1, keepdims=True) + RMS_EPS)
            o_ref[0, rows, gs] = (yg * ng_ref[:, gs]).astype(BF16)


def _head_spread():
    src = jnp.arange(SSD_SPREAD_PIECES * LANES)[:, None] % LANES
    col = jnp.arange(3 * SSD_WIDTH)[None, :]
    first_lane = jnp.array([LANE_DT, LANE_A, LANE_A2])[col // SSD_WIDTH]
    return (src == first_lane + (col % SSD_WIDTH) // SSD_HEAD_DIM).astype(BF16)


def _ssd(xs, bm, cm, z, small, d_skip_x, norm_g):
    b, seq, _ = xs.shape
    t = min(SSD_T, seq)
    rows = min(SSD_ROWS, seq)
    assert seq % rows == 0 and rows % t == 0
    tok = lambda width: pl.BlockSpec((1, rows, width), lambda bi, c: (bi, c, 0))
    const2 = lambda shape: pl.BlockSpec(shape, lambda bi, c: (0, 0))
    return pl.pallas_call(
        functools.partial(_ssd_kernel, t=t, chunks=rows // t),
        grid=(b, seq // rows),
        in_specs=[tok(SSD_WIDTH), tok(SSD_BC), tok(SSD_BC), tok(SSD_WIDTH), tok(LANES),
                  const2((1, SSD_WIDTH)), const2((1, SSD_WIDTH)),
                  const2((t, t)), const2((SSD_SPREAD_PIECES * LANES, 3 * SSD_WIDTH))],
        out_specs=tok(SSD_WIDTH),
        out_shape=jax.ShapeDtypeStruct((b, seq, SSD_WIDTH), BF16),
        scratch_shapes=[pltpu.VMEM((SSD_GROUPS, SSD_STATE, GROUP_WIDTH), F32)],
        compiler_params=_params("arbitrary", "arbitrary"),
    )(xs, bm, cm, z, small, d_skip_x, norm_g, _lower_ones(t), _head_spread())


def _fox_kernel(q_ref, k_ref, v_ref, o_ref, m_scr, acc_scr, *, t):
    qi = pl.program_id(1)
    lane_head = lax.broadcasted_iota(jnp.int32, (t, LANES), 1) // FOX_HEAD_DIM
    row_i = lax.broadcasted_iota(jnp.int32, (t, t), 0)
    col_i = lax.broadcasted_iota(jnp.int32, (t, t), 1)
    causal = row_i >= col_i
    heads_per_vreg = LANES // FOX_HEAD_DIM

    def block(start, width, masked):
        ks = pl.ds(start, width)
        group = lambda hd: slice(hd * LANES, (hd + 1) * LANES)

        def logits(hd):
            s = lax.dot_general(q_ref[0, :, group(hd)], k_ref[0, ks, group(hd)],
                                (((1,), (1,)), ((), ())), preferred_element_type=F32)
            return jnp.where(causal, s, -jnp.inf) if masked else s

        def update(hd, s):
            m_prev = m_scr[hd]
            m_new = jnp.maximum(m_prev, jnp.max(s, axis=-1, keepdims=True))
            p = jnp.exp2(s - jnp.concatenate([m_new] * (width // LANES), axis=1)).astype(BF16)
            acc_scr[hd] = jnp.exp2(m_prev - m_new) * acc_scr[hd] + jnp.dot(
                p, v_ref[0, ks, group(hd)], preferred_element_type=F32)
            m_scr[hd] = m_new

        s_next = logits(0)
        for hd in range(FOX_HEADS):
            s_cur = s_next
            if hd + 1 < FOX_HEADS:
                s_next = logits(hd + 1)
            update(hd, s_cur)

    m_scr[...] = jnp.full(m_scr.shape, -jnp.inf, F32)
    acc_scr[...] = jnp.zeros(acc_scr.shape, F32)

    wide = 2 * t
    assert k_ref.shape[1] % wide == 0

    @pl.loop(0, qi // 2)
    def _(pj):
        block(pl.multiple_of(pj * wide, wide), wide, False)

    @pl.when(qi % 2 == 1)
    def _():
        block(pl.multiple_of((qi - 1) * t, t), t, False)

    block(pl.multiple_of(qi * t, t), t, True)
    for pair in range(FOX_HEADS // heads_per_vreg):
        out = jnp.zeros((t, LANES), F32)
        for hh in range(heads_per_vreg):
            acc = acc_scr[pair * heads_per_vreg + hh]
            ones_lane = (1 - hh) * FOX_HEAD_DIM
            row_sum = acc[:, ones_lane:ones_lane + 1]
            out = out + jnp.where(lane_head == hh, acc, 0.0) / row_sum
        o_ref[0, :, pair * LANES:(pair + 1) * LANES] = out.astype(BF16)


def _fox(qa, ka, va):
    b, seq, _ = qa.shape
    t = min(FOX_T, seq)
    assert seq % t == 0
    return pl.pallas_call(
        functools.partial(_fox_kernel, t=t),
        grid=(b, seq // t),
        in_specs=[
            pl.BlockSpec((1, t, FOX_GROUP), lambda bi, i: (bi, i, 0)),
            pl.BlockSpec((1, seq, FOX_GROUP), lambda bi, i: (bi, 0, 0)),
            pl.BlockSpec((1, seq, FOX_GROUP), lambda bi, i: (bi, 0, 0)),
        ],
        out_specs=pl.BlockSpec((1, t, FOX_WIDTH), lambda bi, i: (bi, i, 0)),
        out_shape=jax.ShapeDtypeStruct((b, seq, FOX_WIDTH), BF16),
        scratch_shapes=[pltpu.VMEM((FOX_HEADS, t, LANES), F32), pltpu.VMEM((FOX_HEADS, t, LANES), F32)],
        compiler_params=_params("arbitrary", "arbitrary"),
    )(qa, ka, va)


def _lower_ones(n):
    return jnp.tril(jnp.ones((n, n), BF16))


def _spread_forget(f):
    lead = f.shape[:-1]
    order = sorted(range(FOX_HEADS), key=_forget_lane)
    blocks, lane = [], 0
    for hd in order:
        blocks.append(jnp.zeros(lead + (_forget_lane(hd) - lane,), f.dtype))
        blocks.append(jnp.broadcast_to(f[..., hd:hd + 1], lead + (FOX_SLOT,)))
        lane = _forget_lane(hd) + FOX_SLOT
    blocks.append(jnp.zeros(lead + (LANES - lane,), f.dtype))
    return jnp.concatenate(blocks, axis=-1)


def _mix_w_in_extras(w):
    tail_start = RAW_WIDTH // LANES * LANES
    tail = jnp.pad(w[..., tail_start:], ((0, 0), (0, 0), (0, LANES - (RAW_WIDTH - tail_start))))
    dt_cols = w[..., RAW_DT:RAW_QKV]
    pad = jnp.zeros(w.shape[:-1] + (LANES - 3 * SSD_HEADS,), w.dtype)
    small = jnp.concatenate([dt_cols, dt_cols, dt_cols, pad, _spread_forget(w[..., RAW_F:RAW_SC])], axis=-1)
    return tail, small.astype(BF16)


def _lane_vec(pieces):
    row = jnp.zeros((LANES,), F32)
    for off, vec in pieces:
        row = row.at[off:off + vec.shape[0]].set(vec.astype(F32))
    return row.reshape(1, LANES)


def kernel(x, c, ln_in_g, ln_in_b, ada_w, ada_b, ffn1_w_in, ffn1_w_out, mix_w_in, mix_w_out, ssd_conv_w, ssd_conv_b, ssd_dt_bias, ssd_a_log, ssd_d, ssd_norm_g, fox_f_bias, sconv_w, ffn2_w_in, ffn2_w_out, ln_g, ln_b):
    b, seq, d = x.shape
    depth = ada_w.shape[0]
    mod = _ada(c, ada_w, ada_b)
    mod = mod.reshape(depth, mod.shape[1], 3 * N_SUB, d)

    w1_in, w1_out, w2_in, w2_out = ffn1_w_in, ffn1_w_out, ffn2_w_in, ffn2_w_out
    wm_tail, wm_small = _mix_w_in_extras(mix_w_in)
    wm_out = mix_w_out.astype(BF16)

    x2d = x.reshape(b * seq, d)
    for l in range(depth):
        mod_l = mod[l]
        x2d = _ffn(x2d, mod_l, w1_in, w1_out, ln_g[l, 0], ln_b[l, 0], layer=l, sub=0, seq=seq,
                   pre=(ln_in_g, ln_in_b) if l == 0 else None)

        small_bias = _lane_vec([(lane, ssd_dt_bias[l]) for lane in (LANE_DT, LANE_A, LANE_A2)])
        alog_vec = _lane_vec([(LANE_A, ssd_a_log[l]), (LANE_A2, ssd_a_log[l])])
        z, xs, bm, cm, qa, ka, va, y_sc, small = _proj(
            x2d.reshape(b, seq, d), mod_l, mix_w_in, wm_tail, wm_small,
            ssd_conv_w[l], ssd_conv_b[l].reshape(1, SSD_CONV_DIM), small_bias, alog_vec, sconv_w[l],
            _spread_forget(fox_f_bias[l]).reshape(1, LANES), layer=l)
        d_skip_x = jnp.repeat(ssd_d[l], SSD_HEAD_DIM).reshape(1, SSD_WIDTH)
        y_ssd = _ssd(xs, bm, cm, z, small, d_skip_x, ssd_norm_g[l].reshape(1, SSD_WIDTH))
        y_fox = _fox(qa, ka, va)
        mixer_tail = (y_ssd.reshape(b * seq, SSD_WIDTH), y_fox.reshape(b * seq, FOX_WIDTH),
                      y_sc.reshape(b * seq, SCONV_WIDTH), wm_out, ln_g[l, 1], ln_b[l, 1])
        x2d = _ffn(x2d, mod_l, w2_in, w2_out, ln_g[l, 2], ln_b[l, 2], layer=l, sub=2, seq=seq,
                   mix=mixer_tail)
    return x2d.reshape(b, seq, d)
```

```python
import functools
import math

import jax
import jax.numpy as jnp
from jax import lax
from jax.experimental import pallas as pl
from jax.experimental.pallas import tpu as pltpu

F32 = jnp.float32
BF16 = jnp.bfloat16

D_MODEL = 1024
DEPTH = 2
N_SUB = 3
D_FF = 2816
SSD_WIDTH = 512
SSD_HEAD_DIM = 64
SSD_HEADS = 8
SSD_GROUPS = 2
SSD_HPG = SSD_HEADS // SSD_GROUPS
SSD_STATE = 128
SSD_CONV = 4
SSD_BC = SSD_GROUPS * SSD_STATE
SSD_CONV_DIM = SSD_WIDTH + 2 * SSD_BC
GROUP_WIDTH = SSD_WIDTH // SSD_GROUPS
FOX_WIDTH = 256
FOX_HEAD_DIM = 64
FOX_HEADS = 4
SCONV_WIDTH = 256
SCONV_K = 3
ALPHA = (2 * DEPTH) ** 0.25
LN_EPS = 1e-5
RMS_EPS = 1e-5
LOG2_E = math.log2(math.e)

LANES = 128
SUBLANES = 8
VMEM_LIMIT = 56 * 1024 * 1024

RAW_DT = SSD_WIDTH + SSD_CONV_DIM
RAW_QKV = RAW_DT + SSD_HEADS
RAW_F = RAW_QKV + 3 * FOX_WIDTH
RAW_SC = RAW_F + FOX_HEADS
RAW_WIDTH = RAW_SC + 3 * SCONV_WIDTH
COL_Z = 0
COL_XBC = COL_Z + SSD_WIDTH
COL_QKV = COL_XBC + SSD_CONV_DIM
COL_SC = COL_QKV + 3 * FOX_WIDTH
COL_SMALL = COL_SC + 3 * SCONV_WIDTH
COL_F = COL_SMALL + LANES
D_PROJ = COL_F + LANES
LANE_DT = 0
LANE_A = 8
LANE_A2 = 16
FOX_PIECES = 3
FOX_SLOT = 2 * FOX_PIECES
FOX_GROUP = FOX_HEADS * LANES
HEADS_PER_VREG = LANES // FOX_HEAD_DIM


def _forget_lane(hd):
    pair, hh = divmod(hd, HEADS_PER_VREG)
    return (HEADS_PER_VREG - 1 - hh) * FOX_HEAD_DIM + pair * FOX_SLOT

FFN_TM = 1024
FFN_TM_MIX = 512
FFN_TF = 256
FFN_HALF = 256
FFN_W_IN_ROWS = 64
FFN_W_OUT_ROWS = 256
PROJ_TM = 512
PROJ_CUM = 128
SSD_T = 256
SSD_ROWS = 512
SSD_SPREAD_PIECES = 2
FOX_T = 512
ADA_TN = 2304


def _layer_norm(x, g, b):
    mu = jnp.mean(x, axis=-1, keepdims=True)
    xc = x - mu
    var = jnp.mean(xc * xc, axis=-1, keepdims=True)
    return xc * lax.rsqrt(var + LN_EPS) * g + b


def _silu(x):
    return x / (1.0 + jnp.exp2(x * (-LOG2_E)))


def _softplus(x):
    return jnp.maximum(x, 0.0) + jnp.log1p(jnp.exp(-jnp.abs(x)))


def _split_bf16(x, pieces):
    out = []
    for _ in range(pieces):
        part = x.astype(BF16)
        out.append(part)
        x = x - part.astype(F32)
    return out


def _params(*semantics):
    return pltpu.CompilerParams(dimension_semantics=semantics, vmem_limit_bytes=VMEM_LIMIT)


def _ada_kernel(c_ref, w_ref, b_ref, o_ref, *, batch):
    act_hi, act_lo = _split_bf16(_silu(c_ref[...]), 2)
    row = lax.broadcasted_iota(jnp.int32, act_hi.shape, 0)
    lhs = jnp.where(row < batch, act_hi, act_lo)
    w_hi, w_lo = _split_bf16(w_ref[...], 2)
    y = jnp.dot(lhs, w_hi, preferred_element_type=F32) + jnp.dot(lhs, w_lo, preferred_element_type=F32)
    rows = y.shape[0]
    o_ref[...] = y + pltpu.roll(y, rows - batch, axis=0) + b_ref[...]


def _ada(c, ada_w, ada_b):
    depth, d, n = ada_w.shape
    batch = c.shape[0]
    rows = -(-2 * batch // SUBLANES) * SUBLANES
    c_pad = jnp.pad(jnp.concatenate([c, c], axis=0), ((0, rows - 2 * batch), (0, 0)))
    return pl.pallas_call(
        functools.partial(_ada_kernel, batch=batch),
        grid=(depth, n // ADA_TN),
        in_specs=[
            pl.BlockSpec((rows, d), lambda l, j: (0, 0)),
            pl.BlockSpec((None, d, ADA_TN), lambda l, j: (l, 0, j)),
            pl.BlockSpec((None, 1, ADA_TN), lambda l, j: (l, 0, j)),
        ],
        out_specs=pl.BlockSpec((None, rows, ADA_TN), lambda l, j: (l, 0, j)),
        out_shape=jax.ShapeDtypeStruct((depth, rows, n), F32),
        compiler_params=_params("arbitrary", "arbitrary"),
    )(c_pad, ada_w, ada_b.reshape(depth, 1, n))


def _ffn_kernel(*refs, layer, sub, pre_ln, mix):
    x_ref, mod_ref, wi_hbm, wo_hbm, lng_ref, lnb_ref = refs[:6]
    n_scratch = 6
    extra = list(refs[6:-1 - n_scratch])
    o_ref = refs[-1 - n_scratch]
    act_scr, wi_ref, wo_ref, stage_i, stage_o, sem = refs[-n_scratch:]
    if pre_ln:
        ing_ref, inb_ref = extra[:2]
        extra = extra[2:]
    if mix:
        yssd_ref, yfox_ref, ysc_ref, wm_ref, mlng_ref, mlnb_ref = extra
    shift = mod_ref[3 * sub:3 * sub + 1, :]
    scale = mod_ref[3 * sub + 1:3 * sub + 2, :]
    gain = mod_ref[3 * sub + 2:3 * sub + 3, :]
    halves = [slice(r, r + FFN_HALF) for r in range(0, x_ref.shape[0], FFN_HALF)]

    def entry(rows):
        x = x_ref[rows, :]
        if pre_ln:
            x = _layer_norm(x, ing_ref[...], inb_ref[...])
        if mix:
            y = jnp.dot(yssd_ref[rows, :], wm_ref[0:SSD_WIDTH, :], preferred_element_type=F32)
            y = y + jnp.dot(yfox_ref[rows, :], wm_ref[SSD_WIDTH:SSD_WIDTH + FOX_WIDTH, :],
                            preferred_element_type=F32)
            y = y + jnp.dot(ysc_ref[rows, :], wm_ref[SSD_WIDTH + FOX_WIDTH:, :],
                            preferred_element_type=F32)
            x = _layer_norm(ALPHA * x + mod_ref[5:6, :] * y, mlng_ref[...], mlnb_ref[...])
        return x

    @pl.when(pl.program_id(0) == 0)
    def _():
        def stream(src_hbm, dst_ref, stage, which, block_rows):
            n_blocks = src_hbm.shape[1] // block_rows
            copy = lambda b: pltpu.make_async_copy(
                src_hbm.at[layer, b * block_rows:(b + 1) * block_rows, :], stage.at[b % 2],
                sem.at[which, b % 2])
            copy(0).start()
            for b in range(n_blocks):
                if b + 1 < n_blocks:
                    copy(b + 1).start()
                copy(b).wait()
                dst_ref[b * block_rows:(b + 1) * block_rows, :] = stage[b % 2].astype(BF16)

        stream(wi_hbm, wi_ref, stage_i, 0, FFN_W_IN_ROWS)
        stream(wo_hbm, wo_ref, stage_o, 1, FFN_W_OUT_ROWS)

    xs = [entry(rows) for rows in halves]
    ys = []
    for rows, x in zip(halves, xs):
        h = (x * (1.0 + scale) + shift).astype(BF16)
        for c in range(D_FF // FFN_TF):
            cols = slice(c * FFN_TF, (c + 1) * FFN_TF)
            gate = jnp.dot(h, wi_ref[:, cols], preferred_element_type=F32)
            up = jnp.dot(h, wi_ref[:, D_FF + c * FFN_TF:D_FF + (c + 1) * FFN_TF],
                         preferred_element_type=F32)
            act_scr[rows, cols] = (_silu(gate) * up).astype(BF16)
        ys.append(jnp.dot(act_scr[rows, :], wo_ref[...], preferred_element_type=F32))
    for rows, x, y in zip(halves, xs, ys):
        o_ref[rows, :] = _layer_norm(ALPHA * x + (0.5 * gain) * y, lng_ref[...], lnb_ref[...])


def _ffn(x2d, mod_l, w_in, w_out, ln_g, ln_b, *, layer, sub, seq, pre=None, mix=None):
    n_tok, d = x2d.shape
    tm = min(FFN_TM if mix is None else FFN_TM_MIX, seq)
    assert seq % tm == 0 and n_tok % seq == 0 and D_FF % FFN_TF == 0 and tm % FFN_HALF == 0
    tiles_per_seq = seq // tm
    row = lambda v: v.reshape(1, d)
    resident = lambda shape: pl.BlockSpec(shape, lambda i: (0, 0), pipeline_mode=pl.Buffered(1))
    weight = lambda shape: pl.BlockSpec((None,) + shape, lambda i: (layer, 0, 0),
                                        pipeline_mode=pl.Buffered(1))
    in_specs = [
        pl.BlockSpec((tm, d), lambda i: (i, 0)),
        pl.BlockSpec((None, 3 * N_SUB, d), lambda i: (i // tiles_per_seq, 0, 0)),
        pl.BlockSpec(memory_space=pl.ANY),
        pl.BlockSpec(memory_space=pl.ANY),
        resident((1, d)),
        resident((1, d)),
    ]
    args = [x2d, mod_l, w_in, w_out, row(ln_g), row(ln_b)]
    if pre is not None:
        in_specs += [resident((1, d))] * 2
        args += [row(pre[0]), row(pre[1])]
    if mix is not None:
        y_ssd, y_fox, y_sc, w_mix, mix_g, mix_b = mix
        tok = lambda width: pl.BlockSpec((tm, width), lambda i: (i, 0))
        in_specs += [tok(SSD_WIDTH), tok(FOX_WIDTH), tok(SCONV_WIDTH), weight((d, d)),
                     resident((1, d)), resident((1, d))]
        args += [y_ssd, y_fox, y_sc, w_mix, row(mix_g), row(mix_b)]
    return pl.pallas_call(
        functools.partial(_ffn_kernel, layer=layer, sub=sub, pre_ln=pre is not None, mix=mix is not None),
        grid=(n_tok // tm,),
        in_specs=in_specs,
        out_specs=pl.BlockSpec((tm, d), lambda i: (i, 0)),
        out_shape=jax.ShapeDtypeStruct((n_tok, d), F32),
        scratch_shapes=[
            pltpu.VMEM((tm, D_FF), BF16),
            pltpu.VMEM((d, 2 * D_FF), BF16),
            pltpu.VMEM((D_FF, d), BF16),
            pltpu.VMEM((2, FFN_W_IN_ROWS, 2 * D_FF), F32),
            pltpu.VMEM((2, FFN_W_OUT_ROWS, d), F32),
            pltpu.SemaphoreType.DMA((2, 2)),
        ],
        compiler_params=_params("arbitrary"),
    )(*args)


def _shift_lanes_left(blocks, shift, n_out):
    lane = lax.broadcasted_iota(jnp.int32, blocks[0].shape, 1)
    rolled = [pltpu.roll(blk, LANES - shift, axis=1) for blk in blocks]
    return [jnp.where(lane < LANES - shift, rolled[k], rolled[k + 1]) for k in range(n_out)]


def _proj_kernel(x_ref, mod_ref, wraw_ref, wtail_ref, wsmall_ref, cw_ref, cb_ref, sbias_ref, alog_ref,
                 scw_ref, fbias_ref, tri_ref,
                 z_ref, xs_ref, bm_ref, cm_ref, q_ref, k_ref, v_ref, ysc_ref, small_ref,
                 w_ref, xbc_scr, u_scr, carry_scr, *, tm):
    t = pl.program_id(1)
    halo = SUBLANES

    @pl.when((pl.program_id(0) == 0) & (t == 0))
    def _():
        raw_block = lambda m: wraw_ref[:, m * LANES:(m + 1) * LANES]
        w_ref[:, :COL_QKV] = wraw_ref[:, :COL_QKV].astype(BF16)
        for raw_start, col, width in ((RAW_QKV, COL_QKV, 3 * FOX_WIDTH), (RAW_SC, COL_SC, 3 * SCONV_WIDTH)):
            first, shift = divmod(raw_start, LANES)
            n_out = width // LANES
            blocks = [raw_block(m) if (m + 1) * LANES <= RAW_WIDTH else wtail_ref[...]
                      for m in range(first, first + n_out + 1)]
            for k, blk in enumerate(_shift_lanes_left(blocks, shift, n_out)):
                w_ref[:, col + k * LANES:col + (k + 1) * LANES] = blk.astype(BF16)
        w_ref[:, COL_SMALL:] = wsmall_ref[...]

    @pl.when(t == 0)
    def _():
        xbc_scr[0:halo, :] = jnp.zeros((halo, SSD_CONV_DIM), F32)
        u_scr[0:halo, :] = jnp.zeros((halo, SCONV_WIDTH), F32)
        carry_scr[...] = jnp.zeros_like(carry_scr)

    @pl.when(t > 0)
    def _():
        xbc_scr[0:halo, :] = xbc_scr[tm:tm + halo, :]
        u_scr[0:halo, :] = u_scr[tm:tm + halo, :]

    shift = mod_ref[3:4, :]
    scale = mod_ref[4:5, :]
    h = (x_ref[0] * (1.0 + scale) + shift).astype(BF16)

    def seg(start, width):
        return jnp.dot(h, w_ref[:, start:start + width], preferred_element_type=F32)

    xbc_scr[halo:halo + tm, :] = seg(COL_XBC, SSD_CONV_DIM)
    conv = cb_ref[...] + cw_ref[SSD_CONV - 1:SSD_CONV, :] * xbc_scr[halo:halo + tm, :]
    for k in range(SSD_CONV - 1):
        back = SSD_CONV - 1 - k
        conv = conv + cw_ref[k:k + 1, :] * xbc_scr[halo - back:halo - back + tm, :]
    xbc = _silu(conv)
    xs_ref[0] = xbc[:, :SSD_WIDTH]
    bm_ref[0] = xbc[:, SSD_WIDTH:SSD_WIDTH + SSD_BC].astype(BF16)
    cm_ref[0] = xbc[:, SSD_WIDTH + SSD_BC:].astype(BF16)

    sc = seg(COL_SC, 3 * SCONV_WIDTH)
    u_scr[halo:halo + tm, :] = sc[:, SCONV_WIDTH:2 * SCONV_WIDTH] * sc[:, 2 * SCONV_WIDTH:]
    cu = scw_ref[SCONV_K - 1:SCONV_K, :] * u_scr[halo:halo + tm, :]
    for k in range(SCONV_K - 1):
        back = SCONV_K - 1 - k
        cu = cu + scw_ref[k:k + 1, :] * u_scr[halo - back:halo - back + tm, :]
    ysc_ref[0] = (sc[:, :SCONV_WIDTH] * cu).astype(BF16)

    lane = lax.broadcasted_iota(jnp.int32, (tm, LANES), 1)
    pos = lane % FOX_HEAD_DIM
    used = pos < (FOX_HEADS // HEADS_PER_VREG) * FOX_SLOT
    small_f = seg(COL_SMALL, 2 * LANES)
    log_f = jnp.where(used, -_softplus(-(small_f[:, LANES:] + fbias_ref[...])), 0.0)
    parts = jnp.concatenate(_split_bf16(log_f, 3), axis=1)
    offset = carry_scr[0:1, :]
    blocks = []
    for r in range(0, tm, PROJ_CUM):
        local3 = jnp.dot(tri_ref[...], parts[r:r + PROJ_CUM, :], preferred_element_type=F32)
        local = local3[:, :LANES] + local3[:, LANES:2 * LANES] + local3[:, 2 * LANES:]
        blocks.append(local + offset)
        offset = offset + local[PROJ_CUM - 1:PROJ_CUM, :]
    cum_f = jnp.concatenate(blocks, axis=0)
    carry_scr[...] = jnp.broadcast_to(offset, carry_scr.shape)
    hi, mid, lo = (part.astype(F32) for part in _split_bf16(cum_f * LOG2_E, FOX_PIECES))
    piece = jnp.where(pos % FOX_PIECES == 0, hi, jnp.where(pos % FOX_PIECES == 1, mid, lo))
    in_a = pos % FOX_SLOT < FOX_PIECES
    f_q = jnp.where(in_a, piece, 1.0)
    f_k = jnp.where(in_a, 1.0, -piece)

    qkv = seg(COL_QKV, 3 * FOX_WIDTH)
    q = qkv[:, :FOX_WIDTH] * (FOX_HEAD_DIM ** -0.5 * LOG2_E)
    k = qkv[:, FOX_WIDTH:2 * FOX_WIDTH]
    v = qkv[:, 2 * FOX_WIDTH:]
    for hd in range(FOX_HEADS):
        pair, hh = divmod(hd, HEADS_PER_VREG)
        ps = slice(pair * LANES, (pair + 1) * LANES)
        gs = slice(hd * LANES, (hd + 1) * LANES)
        mine = (lane // FOX_HEAD_DIM) == hh
        first = _forget_lane(hd)
        forget = (lane >= first) & (lane < first + FOX_SLOT)
        ones_col = jnp.where(lane == (1 - hh) * FOX_HEAD_DIM, 1.0, 0.0)
        q_ref[0, :, gs] = jnp.where(mine, q[:, ps], jnp.where(forget, f_q, 0.0)).astype(BF16)
        k_ref[0, :, gs] = jnp.where(mine, k[:, ps], jnp.where(forget, f_k, 0.0)).astype(BF16)
        v_ref[0, :, gs] = jnp.where(mine, v[:, ps], ones_col).astype(BF16)

    dt = _softplus(small_f[:, :LANES] + sbias_ref[...])
    a_neg = -jnp.exp(alog_ref[...])
    small_ref[0] = jnp.where(lane < LANE_A, dt, dt * a_neg)

    z_ref[0] = seg(COL_Z, SSD_WIDTH)


def _proj(x, mod_l, w_raw, w_tail, w_small, conv_w, conv_b, small_bias, alog_vec, sconv_w, f_bias, *, layer):
    b, seq, d = x.shape
    tm = min(PROJ_TM, seq)
    assert seq % tm == 0
    nt = seq // tm
    tok = lambda width: pl.BlockSpec((1, tm, width), lambda bi, t: (bi, t, 0))
    const2 = lambda shape: pl.BlockSpec(shape, lambda bi, t: (0, 0))
    out_shapes = (
        jax.ShapeDtypeStruct((b, seq, SSD_WIDTH), F32),
        jax.ShapeDtypeStruct((b, seq, SSD_WIDTH), F32),
        jax.ShapeDtypeStruct((b, seq, SSD_BC), BF16),
        jax.ShapeDtypeStruct((b, seq, SSD_BC), BF16),
        jax.ShapeDtypeStruct((b, seq, FOX_GROUP), BF16),
        jax.ShapeDtypeStruct((b, seq, FOX_GROUP), BF16),
        jax.ShapeDtypeStruct((b, seq, FOX_GROUP), BF16),
        jax.ShapeDtypeStruct((b, seq, SCONV_WIDTH), BF16),
        jax.ShapeDtypeStruct((b, seq, LANES), F32),
    )
    out_specs = (
        tok(SSD_WIDTH), tok(SSD_WIDTH), tok(SSD_BC), tok(SSD_BC),
        tok(FOX_GROUP), tok(FOX_GROUP), tok(FOX_GROUP), tok(SCONV_WIDTH), tok(LANES),
    )
    return pl.pallas_call(
        functools.partial(_proj_kernel, tm=tm),
        grid=(b, nt),
        in_specs=[
            tok(d),
            pl.BlockSpec((None, 3 * N_SUB, d), lambda bi, t: (bi, 0, 0)),
            pl.BlockSpec((None, d, RAW_WIDTH), lambda bi, t: (layer, 0, 0), pipeline_mode=pl.Buffered(1)),
            pl.BlockSpec((None, d, LANES), lambda bi, t: (layer, 0, 0), pipeline_mode=pl.Buffered(1)),
            pl.BlockSpec((None, d, 2 * LANES), lambda bi, t: (layer, 0, 0), pipeline_mode=pl.Buffered(1)),
            const2((SSD_CONV, SSD_CONV_DIM)),
            const2((1, SSD_CONV_DIM)),
            const2((1, LANES)),
            const2((1, LANES)),
            const2((SCONV_K, SCONV_WIDTH)),
            const2((1, LANES)),
            const2((PROJ_CUM, PROJ_CUM)),
        ],
        out_specs=out_specs,
        out_shape=out_shapes,
        scratch_shapes=[
            pltpu.VMEM((d, D_PROJ), BF16),
            pltpu.VMEM((tm + SUBLANES, SSD_CONV_DIM), F32),
            pltpu.VMEM((tm + SUBLANES, SCONV_WIDTH), F32),
            pltpu.VMEM((SUBLANES, LANES), F32),
        ],
        compiler_params=_params("arbitrary", "arbitrary"),
    )(x, mod_l, w_raw, w_tail, w_small, conv_w, conv_b, small_bias, alog_vec, sconv_w, f_bias,
      _lower_ones(PROJ_CUM))


def _ssd_kernel(xs_ref, bm_ref, cm_ref, z_ref, small_ref, dskip_ref, ng_ref, tri_ref, spread_ref,
                o_ref, state_scr, *, t, chunks):
    @pl.when(pl.program_id(1) == 0)
    def _():
        state_scr[...] = jnp.zeros_like(state_scr)

    lane = lax.broadcasted_iota(jnp.int32, (t, LANES), 1)
    row_i = lax.broadcasted_iota(jnp.int32, (t, t), 0)
    col_i = lax.broadcasted_iota(jnp.int32, (t, t), 1)
    lower = row_i >= col_i
    head_of_lane = lax.broadcasted_iota(jnp.int32, (t, GROUP_WIDTH), 1) // SSD_HEAD_DIM

    for ci in range(chunks):
        rows = slice(ci * t, (ci + 1) * t)
        small = small_ref[0, rows, :]
        a_only = jnp.where((lane >= LANE_A) & (lane < LANE_A2 + SSD_HEADS), small, 0.0)
        cs3 = jnp.dot(tri_ref[...], jnp.concatenate(_split_bf16(a_only, 3), axis=1),
                      preferred_element_type=F32)
        cs = cs3[:, :LANES] + cs3[:, LANES:2 * LANES] + cs3[:, 2 * LANES:]
        cs_t = cs.T
        total = cs[t - 1:t, :]
        per_head = jnp.where(lane < LANE_A, small,
                             jnp.where(lane < LANE_A2, jnp.exp(cs), jnp.exp(total - cs)))
        wide = jnp.dot(jnp.concatenate(_split_bf16(per_head, SSD_SPREAD_PIECES), axis=1), spread_ref[...],
                       preferred_element_type=F32)
        dt_x = wide[:, :SSD_WIDTH]
        decay_in = wide[:, SSD_WIDTH:2 * SSD_WIDTH]
        decay_out = wide[:, 2 * SSD_WIDTH:]
        decay_chunk = decay_in[t - 1:t, :]

        xs = xs_ref[0, rows, :]
        xdt = xs * dt_x
        ys = []
        for g in range(SSD_GROUPS):
            gs = slice(g * GROUP_WIDTH, (g + 1) * GROUP_WIDTH)
            ns = slice(g * SSD_STATE, (g + 1) * SSD_STATE)
            bg = bm_ref[0, rows, ns]
            cg = cm_ref[0, rows, ns]
            cb = lax.dot_general(cg, bg, (((1,), (1,)), ((), ())), preferred_element_type=F32)
            xg = xdt[:, gs]
            m_parts, x_parts = [], []
            for e in range(SSD_HPG):
                hd = g * SSD_HPG + e
                col = cs[:, LANE_A + hd:LANE_A + hd + 1]
                row = cs_t[LANE_A + hd:LANE_A + hd + 1, :]
                l_mat = jnp.exp(jnp.where(lower, col - row, -jnp.inf))
                m_parts.append((cb * l_mat).astype(BF16))
                x_parts.append(jnp.where(head_of_lane == e, xg, 0.0).astype(BF16))
            m_cat = jnp.concatenate(m_parts, axis=1)
            x_blk = jnp.concatenate(x_parts, axis=0)
            y_diag = jnp.dot(m_cat, x_blk, preferred_element_type=F32)

            state = state_scr[g]
            y_off = jnp.dot(cg, state.astype(BF16), preferred_element_type=F32) * decay_in[:, gs]
            xd = (xg * decay_out[:, gs]).astype(BF16)
            upd = lax.dot_general(bg, xd, (((0,), (0,)), ((), ())), preferred_element_type=F32)
            state_scr[g] = state * decay_chunk[:, gs] + upd
            ys.append(y_diag + y_off + dskip_ref[:, gs] * xs[:, gs])

        z = z_ref[0, rows, :]
        for g in range(SSD_GROUPS):
            gs = slice(g * GROUP_WIDTH, (g + 1) * GROUP_WIDTH)
            yg = ys[g] * _silu(z[:, gs])
            yg = yg * lax.rsqrt(jnp.mean(yg * yg, axis=-1, keepdims=True) + RMS_EPS)
            o_ref[0, rows, gs] = (yg * ng_ref[:, gs]).astype(BF16)


def _head_spread():
    src = jnp.arange(SSD_SPREAD_PIECES * LANES)[:, None] % LANES
    col = jnp.arange(3 * SSD_WIDTH)[None, :]
    first_lane = jnp.array([LANE_DT, LANE_A, LANE_A2])[col // SSD_WIDTH]
    return (src == first_lane + (col % SSD_WIDTH) // SSD_HEAD_DIM).astype(BF16)


def _ssd(xs, bm, cm, z, small, d_skip_x, norm_g):
    b, seq, _ = xs.shape
    t = min(SSD_T, seq)
    rows = min(SSD_ROWS, seq)
    assert seq % rows == 0 and rows % t == 0
    tok = lambda width: pl.BlockSpec((1, rows, width), lambda bi, c: (bi, c, 0))
    const2 = lambda shape: pl.BlockSpec(shape, lambda bi, c: (0, 0))
    return pl.pallas_call(
        functools.partial(_ssd_kernel, t=t, chunks=rows // t),
        grid=(b, seq // rows),
        in_specs=[tok(SSD_WIDTH), tok(SSD_BC), tok(SSD_BC), tok(SSD_WIDTH), tok(LANES),
                  const2((1, SSD_WIDTH)), const2((1, SSD_WIDTH)),
                  const2((t, t)), const2((SSD_SPREAD_PIECES * LANES, 3 * SSD_WIDTH))],
        out_specs=tok(SSD_WIDTH),
        out_shape=jax.ShapeDtypeStruct((b, seq, SSD_WIDTH), BF16),
        scratch_shapes=[pltpu.VMEM((SSD_GROUPS, SSD_STATE, GROUP_WIDTH), F32)],
        compiler_params=_params("arbitrary", "arbitrary"),
    )(xs, bm, cm, z, small, d_skip_x, norm_g, _lower_ones(t), _head_spread())


def _fox_kernel(q_ref, k_ref, v_ref, o_ref, m_scr, acc_scr, *, t):
    qi = pl.program_id(1)
    lane_head = lax.broadcasted_iota(jnp.int32, (t, LANES), 1) // FOX_HEAD_DIM
    row_i = lax.broadcasted_iota(jnp.int32, (t, t), 0)
    col_i = lax.broadcasted_iota(jnp.int32, (t, t), 1)
    causal = row_i >= col_i
    heads_per_vreg = LANES // FOX_HEAD_DIM

    def block(start, width, masked):
        ks = pl.ds(start, width)
        group = lambda hd: slice(hd * LANES, (hd + 1) * LANES)

        def logits(hd):
            s = lax.dot_general(q_ref[0, :, group(hd)], k_ref[0, ks, group(hd)],
                                (((1,), (1,)), ((), ())), preferred_element_type=F32)
            return jnp.where(causal, s, -jnp.inf) if masked else s

        def update(hd, s):
            m_prev = m_scr[hd]
            m_new = jnp.maximum(m_prev, jnp.max(s, axis=-1, keepdims=True))
            p = jnp.exp2(s - jnp.concatenate([m_new] * (width // LANES), axis=1)).astype(BF16)
            acc_scr[hd] = jnp.exp2(m_prev - m_new) * acc_scr[hd] + jnp.dot(
                p, v_ref[0, ks, group(hd)], preferred_element_type=F32)
            m_scr[hd] = m_new

        s_next = logits(0)
        for hd in range(FOX_HEADS):
            s_cur = s_next
            if hd + 1 < FOX_HEADS:
                s_next = logits(hd + 1)
            update(hd, s_cur)

    m_scr[...] = jnp.full(m_scr.shape, -jnp.inf, F32)
    acc_scr[...] = jnp.zeros(acc_scr.shape, F32)

    wide = 2 * t
    assert k_ref.shape[1] % wide == 0

    @pl.loop(0, qi // 2)
    def _(pj):
        block(pl.multiple_of(pj * wide, wide), wide, False)

    @pl.when(qi % 2 == 1)
    def _():
        block(pl.multiple_of((qi - 1) * t, t), t, False)

    block(pl.multiple_of(qi * t, t), t, True)
    for pair in range(FOX_HEADS // heads_per_vreg):
        out = jnp.zeros((t, LANES), F32)
        for hh in range(heads_per_vreg):
            acc = acc_scr[pair * heads_per_vreg + hh]
            ones_lane = (1 - hh) * FOX_HEAD_DIM
            row_sum = acc[:, ones_lane:ones_lane + 1]
            out = out + jnp.where(lane_head == hh, acc, 0.0) / row_sum
        o_ref[0, :, pair * LANES:(pair + 1) * LANES] = out.astype(BF16)


def _fox(qa, ka, va):
    b, seq, _ = qa.shape
    t = min(FOX_T, seq)
    assert seq % t == 0
    return pl.pallas_call(
        functools.partial(_fox_kernel, t=t),
        grid=(b, seq // t),
        in_specs=[
            pl.BlockSpec((1, t, FOX_GROUP), lambda bi, i: (bi, i, 0)),
            pl.BlockSpec((1, seq, FOX_GROUP), lambda bi, i: (bi, 0, 0)),
            pl.BlockSpec((1, seq, FOX_GROUP), lambda bi, i: (bi, 0, 0)),
        ],
        out_specs=pl.BlockSpec((1, t, FOX_WIDTH), lambda bi, i: (bi, i, 0)),
        out_shape=jax.ShapeDtypeStruct((b, seq, FOX_WIDTH), BF16),
        scratch_shapes=[pltpu.VMEM((FOX_HEADS, t, LANES), F32), pltpu.VMEM((FOX_HEADS, t, LANES), F32)],
        compiler_params=_params("arbitrary", "arbitrary"),
    )(qa, ka, va)


def _lower_ones(n):
    return jnp.tril(jnp.ones((n, n), BF16))


def _spread_forget(f):
    lead = f.shape[:-1]
    order = sorted(range(FOX_HEADS), key=_forget_lane)
    blocks, lane = [], 0
    for hd in order:
        blocks.append(jnp.zeros(lead + (_forget_lane(hd) - lane,), f.dtype))
        blocks.append(jnp.broadcast_to(f[..., hd:hd + 1], lead + (FOX_SLOT,)))
        lane = _forget_lane(hd) + FOX_SLOT
    blocks.append(jnp.zeros(lead + (LANES - lane,), f.dtype))
    return jnp.concatenate(blocks, axis=-1)


def _mix_w_in_extras(w):
    tail_start = RAW_WIDTH // LANES * LANES
    tail = jnp.pad(w[..., tail_start:], ((0, 0), (0, 0), (0, LANES - (RAW_WIDTH - tail_start))))
    dt_cols = w[..., RAW_DT:RAW_QKV]
    pad = jnp.zeros(w.shape[:-1] + (LANES - 3 * SSD_HEADS,), w.dtype)
    small = jnp.concatenate([dt_cols, dt_cols, dt_cols, pad, _spread_forget(w[..., RAW_F:RAW_SC])], axis=-1)
    return tail, small.astype(BF16)


def _lane_vec(pieces):
    row = jnp.zeros((LANES,), F32)
    for off, vec in pieces:
        row = row.at[off:off + vec.shape[0]].set(vec.astype(F32))
    return row.reshape(1, LANES)


def kernel(x, c, ln_in_g, ln_in_b, ada_w, ada_b, ffn1_w_in, ffn1_w_out, mix_w_in, mix_w_out, ssd_conv_w, ssd_conv_b, ssd_dt_bias, ssd_a_log, ssd_d, ssd_norm_g, fox_f_bias, sconv_w, ffn2_w_in, ffn2_w_out, ln_g, ln_b):
    b, seq, d = x.shape
    depth = ada_w.shape[0]
    mod = _ada(c, ada_w, ada_b)
    mod = mod.reshape(depth, mod.shape[1], 3 * N_SUB, d)

    w1_in, w1_out, w2_in, w2_out = ffn1_w_in, ffn1_w_out, ffn2_w_in, ffn2_w_out
    wm_tail, wm_small = _mix_w_in_extras(mix_w_in)
    wm_out = mix_w_out.astype(BF16)

    x2d = x.reshape(b * seq, d)
    for l in range(depth):
        mod_l = mod[l]
        x2d = _ffn(x2d, mod_l, w1_in, w1_out, ln_g[l, 0], ln_b[l, 0], layer=l, sub=0, seq=seq,
                   pre=(ln_in_g, ln_in_b) if l == 0 else None)

        small_bias = _lane_vec([(lane, ssd_dt_bias[l]) for lane in (LANE_DT, LANE_A, LANE_A2)])
        alog_vec = _lane_vec([(LANE_A, ssd_a_log[l]), (LANE_A2, ssd_a_log[l])])
        z, xs, bm, cm, qa, ka, va, y_sc, small = _proj(
            x2d.reshape(b, seq, d), mod_l, mix_w_in, wm_tail, wm_small,
            ssd_conv_w[l], ssd_conv_b[l].reshape(1, SSD_CONV_DIM), small_bias, alog_vec, sconv_w[l],
            _spread_forget(fox_f_bias[l]).reshape(1, LANES), layer=l)
        d_skip_x = jnp.repeat(ssd_d[l], SSD_HEAD_DIM).reshape(1, SSD_WIDTH)
        y_ssd = _ssd(xs, bm, cm, z, small, d_skip_x, ssd_norm_g[l].reshape(1, SSD_WIDTH))
        y_fox = _fox(qa, ka, va)
        mixer_tail = (y_ssd.reshape(b * seq, SSD_WIDTH), y_fox.reshape(b * seq, FOX_WIDTH),
                      y_sc.reshape(b * seq, SCONV_WIDTH), wm_out, ln_g[l, 1], ln_b[l, 1])
        x2d = _ffn(x2d, mod_l, w2_in, w2_out, ln_g[l, 2], ln_b[l, 2], layer=l, sub=2, seq=seq,
                   mix=mixer_tail)
    return x2d.reshape(b, seq, d)
```

```python
import functools
import math

import jax
import jax.numpy as jnp
from jax import lax
from jax.experimental import pallas as pl
from jax.experimental.pallas import tpu as pltpu

F32 = jnp.float32
BF16 = jnp.bfloat16

D_MODEL = 1024
DEPTH = 2
N_SUB = 3
D_FF = 2816
SSD_WIDTH = 512
SSD_HEAD_DIM = 64
SSD_HEADS = 8
SSD_GROUPS = 2
SSD_HPG = SSD_HEADS // SSD_GROUPS
SSD_STATE = 128
SSD_CONV = 4
SSD_BC = SSD_GROUPS * SSD_STATE
SSD_CONV_DIM = SSD_WIDTH + 2 * SSD_BC
GROUP_WIDTH = SSD_WIDTH // SSD_GROUPS
FOX_WIDTH = 256
FOX_HEAD_DIM = 64
FOX_HEADS = 4
SCONV_WIDTH = 256
SCONV_K = 3
ALPHA = (2 * DEPTH) ** 0.25
LN_EPS = 1e-5
RMS_EPS = 1e-5
LOG2_E = math.log2(math.e)

LANES = 128
SUBLANES = 8
VMEM_LIMIT = 56 * 1024 * 1024

RAW_DT = SSD_WIDTH + SSD_CONV_DIM
RAW_QKV = RAW_DT + SSD_HEADS
RAW_F = RAW_QKV + 3 * FOX_WIDTH
RAW_SC = RAW_F + FOX_HEADS
RAW_WIDTH = RAW_SC + 3 * SCONV_WIDTH
COL_Z = 0
COL_XBC = COL_Z + SSD_WIDTH
COL_QKV = COL_XBC + SSD_CONV_DIM
COL_SC = COL_QKV + 3 * FOX_WIDTH
COL_SMALL = COL_SC + 3 * SCONV_WIDTH
COL_F = COL_SMALL + LANES
D_PROJ = COL_F + LANES
LANE_DT = 0
LANE_A = 8
LANE_A2 = 16
FOX_PIECES = 3
FOX_SLOT = 2 * FOX_PIECES
FOX_GROUP = FOX_HEADS * LANES
HEADS_PER_VREG = LANES // FOX_HEAD_DIM
KVQ_K = 0
KVQ_V = FOX_GROUP
KVQ_Q = 2 * FOX_GROUP
KVQ_WIDTH = 3 * FOX_GROUP


def _forget_lane(hd):
    pair, hh = divmod(hd, HEADS_PER_VREG)
    return (HEADS_PER_VREG - 1 - hh) * FOX_HEAD_DIM + pair * FOX_SLOT

FFN_TM = 1024
FFN_TM_MIX = 512
FFN_TF = 256
FFN_HALF = 256
FFN_W_IN_ROWS = 64
FFN_W_OUT_ROWS = 256
PROJ_TM = 512
PROJ_CUM = 128
SSD_T = 256
SSD_ROWS = 2048
SSD_SPREAD_PIECES = 2
FOX_T = 512
ADA_TN = 2304


def _layer_norm(x, g, b):
    mu = jnp.mean(x, axis=-1, keepdims=True)
    xc = x - mu
    var = jnp.mean(xc * xc, axis=-1, keepdims=True)
    return xc * lax.rsqrt(var + LN_EPS) * g + b


def _silu(x):
    return x / (1.0 + jnp.exp2(x * (-LOG2_E)))


def _softplus(x):
    return jnp.maximum(x, 0.0) + jnp.log1p(jnp.exp(-jnp.abs(x)))


def _split_bf16(x, pieces):
    out = []
    for _ in range(pieces):
        part = x.astype(BF16)
        out.append(part)
        x = x - part.astype(F32)
    return out


def _params(*semantics):
    return pltpu.CompilerParams(dimension_semantics=semantics, vmem_limit_bytes=VMEM_LIMIT)


def _ada_kernel(c_ref, w_ref, b_ref, o_ref, *, batch):
    act_hi, act_lo = _split_bf16(_silu(c_ref[...]), 2)
    row = lax.broadcasted_iota(jnp.int32, act_hi.shape, 0)
    lhs = jnp.where(row < batch, act_hi, act_lo)
    w_hi, w_lo = _split_bf16(w_ref[...], 2)
    y = jnp.dot(lhs, w_hi, preferred_element_type=F32) + jnp.dot(lhs, w_lo, preferred_element_type=F32)
    rows = y.shape[0]
    o_ref[...] = y + pltpu.roll(y, rows - batch, axis=0) + b_ref[...]


def _ada(c, ada_w, ada_b):
    depth, d, n = ada_w.shape
    batch = c.shape[0]
    rows = -(-2 * batch // SUBLANES) * SUBLANES
    c_pad = jnp.pad(jnp.concatenate([c, c], axis=0), ((0, rows - 2 * batch), (0, 0)))
    return pl.pallas_call(
        functools.partial(_ada_kernel, batch=batch),
        grid=(depth, n // ADA_TN),
        in_specs=[
            pl.BlockSpec((rows, d), lambda l, j: (0, 0)),
            pl.BlockSpec((None, d, ADA_TN), lambda l, j: (l, 0, j)),
            pl.BlockSpec((None, 1, ADA_TN), lambda l, j: (l, 0, j)),
        ],
        out_specs=pl.BlockSpec((None, rows, ADA_TN), lambda l, j: (l, 0, j)),
        out_shape=jax.ShapeDtypeStruct((depth, rows, n), F32),
        compiler_params=_params("arbitrary", "arbitrary"),
    )(c_pad, ada_w, ada_b.reshape(depth, 1, n))


def _ffn_kernel(*refs, layer, sub, pre_ln, mix):
    x_ref, mod_ref, wi_hbm, wo_hbm, lng_ref, lnb_ref = refs[:6]
    n_scratch = 6
    extra = list(refs[6:-1 - n_scratch])
    o_ref = refs[-1 - n_scratch]
    act_scr, wi_ref, wo_ref, stage_i, stage_o, sem = refs[-n_scratch:]
    if pre_ln:
        ing_ref, inb_ref = extra[:2]
        extra = extra[2:]
    if mix:
        yssd_ref, yfox_ref, ysc_ref, wm_ref, mlng_ref, mlnb_ref = extra
    shift = mod_ref[3 * sub:3 * sub + 1, :]
    scale = mod_ref[3 * sub + 1:3 * sub + 2, :]
    gain = mod_ref[3 * sub + 2:3 * sub + 3, :]
    halves = [slice(r, r + FFN_HALF) for r in range(0, x_ref.shape[0], FFN_HALF)]

    def entry(rows):
        x = x_ref[rows, :]
        if pre_ln:
            x = _layer_norm(x, ing_ref[...], inb_ref[...])
        if mix:
            y = jnp.dot(yssd_ref[rows, :], wm_ref[0:SSD_WIDTH, :], preferred_element_type=F32)
            y = y + jnp.dot(yfox_ref[rows, :], wm_ref[SSD_WIDTH:SSD_WIDTH + FOX_WIDTH, :],
                            preferred_element_type=F32)
            y = y + jnp.dot(ysc_ref[rows, :], wm_ref[SSD_WIDTH + FOX_WIDTH:, :],
                            preferred_element_type=F32)
            x = _layer_norm(ALPHA * x + mod_ref[5:6, :] * y, mlng_ref[...], mlnb_ref[...])
        return x

    @pl.when(pl.program_id(0) == 0)
    def _():
        def stream(src_hbm, dst_ref, stage, which, block_rows):
            n_blocks = src_hbm.shape[1] // block_rows
            copy = lambda b: pltpu.make_async_copy(
                src_hbm.at[layer, b * block_rows:(b + 1) * block_rows, :], stage.at[b % 2],
                sem.at[which, b % 2])
            copy(0).start()
            for b in range(n_blocks):
                if b + 1 < n_blocks:
                    copy(b + 1).start()
                copy(b).wait()
                dst_ref[b * block_rows:(b + 1) * block_rows, :] = stage[b % 2].astype(BF16)

        stream(wi_hbm, wi_ref, stage_i, 0, FFN_W_IN_ROWS)
        stream(wo_hbm, wo_ref, stage_o, 1, FFN_W_OUT_ROWS)

    xs = [entry(rows) for rows in halves]
    ys = []
    for rows, x in zip(halves, xs):
        h = (x * (1.0 + scale) + shift).astype(BF16)
        for c in range(D_FF // FFN_TF):
            cols = slice(c * FFN_TF, (c + 1) * FFN_TF)
            gate = jnp.dot(h, wi_ref[:, cols], preferred_element_type=F32)
            up = jnp.dot(h, wi_ref[:, D_FF + c * FFN_TF:D_FF + (c + 1) * FFN_TF],
                         preferred_element_type=F32)
            act_scr[rows, cols] = (_silu(gate) * up).astype(BF16)
        ys.append(jnp.dot(act_scr[rows, :], wo_ref[...], preferred_element_type=F32))
    for rows, x, y in zip(halves, xs, ys):
        o_ref[rows, :] = _layer_norm(ALPHA * x + (0.5 * gain) * y, lng_ref[...], lnb_ref[...])


def _ffn(x2d, mod_l, w_in, w_out, ln_g, ln_b, *, layer, sub, seq, pre=None, mix=None):
    n_tok, d = x2d.shape
    tm = min(FFN_TM if mix is None else FFN_TM_MIX, seq)
    assert seq % tm == 0 and n_tok % seq == 0 and D_FF % FFN_TF == 0 and tm % FFN_HALF == 0
    tiles_per_seq = seq // tm
    row = lambda v: v.reshape(1, d)
    resident = lambda shape: pl.BlockSpec(shape, lambda i: (0, 0), pipeline_mode=pl.Buffered(1))
    weight = lambda shape: pl.BlockSpec((None,) + shape, lambda i: (layer, 0, 0),
                                        pipeline_mode=pl.Buffered(1))
    in_specs = [
        pl.BlockSpec((tm, d), lambda i: (i, 0)),
        pl.BlockSpec((None, 3 * N_SUB, d), lambda i: (i // tiles_per_seq, 0, 0)),
        pl.BlockSpec(memory_space=pl.ANY),
        pl.BlockSpec(memory_space=pl.ANY),
        resident((1, d)),
        resident((1, d)),
    ]
    args = [x2d, mod_l, w_in, w_out, row(ln_g), row(ln_b)]
    if pre is not None:
        in_specs += [resident((1, d))] * 2
        args += [row(pre[0]), row(pre[1])]
    if mix is not None:
        y_ssd, y_fox, y_sc, w_mix, mix_g, mix_b = mix
        tok = lambda width: pl.BlockSpec((tm, width), lambda i: (i, 0))
        in_specs += [tok(SSD_WIDTH), tok(FOX_WIDTH), tok(SCONV_WIDTH), weight((d, d)),
                     resident((1, d)), resident((1, d))]
        args += [y_ssd, y_fox, y_sc, w_mix, row(mix_g), row(mix_b)]
    return pl.pallas_call(
        functools.partial(_ffn_kernel, layer=layer, sub=sub, pre_ln=pre is not None, mix=mix is not None),
        grid=(n_tok // tm,),
        in_specs=in_specs,
        out_specs=pl.BlockSpec((tm, d), lambda i: (i, 0)),
        out_shape=jax.ShapeDtypeStruct((n_tok, d), F32),
        scratch_shapes=[
            pltpu.VMEM((tm, D_FF), BF16),
            pltpu.VMEM((d, 2 * D_FF), BF16),
            pltpu.VMEM((D_FF, d), BF16),
            pltpu.VMEM((2, FFN_W_IN_ROWS, 2 * D_FF), F32),
            pltpu.VMEM((2, FFN_W_OUT_ROWS, d), F32),
            pltpu.SemaphoreType.DMA((2, 2)),
        ],
        compiler_params=_params("arbitrary"),
    )(*args)


def _shift_lanes_left(blocks, shift, n_out):
    lane = lax.broadcasted_iota(jnp.int32, blocks[0].shape, 1)
    rolled = [pltpu.roll(blk, LANES - shift, axis=1) for blk in blocks]
    return [jnp.where(lane < LANES - shift, rolled[k], rolled[k + 1]) for k in range(n_out)]


def _proj_kernel(x_ref, mod_ref, wraw_ref, wtail_ref, wsmall_ref, cw_ref, cb_ref, sbias_ref, alog_ref,
                 scw_ref, fbias_ref, tri_ref,
                 zx_ref, bc_ref, kvq_ref, ysc_ref, small_ref,
                 w_ref, xbc_scr, u_scr, carry_scr, *, tm):
    t = pl.program_id(1)
    halo = SUBLANES

    @pl.when((pl.program_id(0) == 0) & (t == 0))
    def _():
        raw_block = lambda m: wraw_ref[:, m * LANES:(m + 1) * LANES]
        w_ref[:, :COL_QKV] = wraw_ref[:, :COL_QKV].astype(BF16)
        for raw_start, col, width in ((RAW_QKV, COL_QKV, 3 * FOX_WIDTH), (RAW_SC, COL_SC, 3 * SCONV_WIDTH)):
            first, shift = divmod(raw_start, LANES)
            n_out = width // LANES
            blocks = [raw_block(m) if (m + 1) * LANES <= RAW_WIDTH else wtail_ref[...]
                      for m in range(first, first + n_out + 1)]
            for k, blk in enumerate(_shift_lanes_left(blocks, shift, n_out)):
                w_ref[:, col + k * LANES:col + (k + 1) * LANES] = blk.astype(BF16)
        w_ref[:, COL_SMALL:] = wsmall_ref[...]

    @pl.when(t == 0)
    def _():
        xbc_scr[0:halo, :] = jnp.zeros((halo, SSD_CONV_DIM), F32)
        u_scr[0:halo, :] = jnp.zeros((halo, SCONV_WIDTH), F32)
        carry_scr[...] = jnp.zeros_like(carry_scr)

    @pl.when(t > 0)
    def _():
        xbc_scr[0:halo, :] = xbc_scr[tm:tm + halo, :]
        u_scr[0:halo, :] = u_scr[tm:tm + halo, :]

    shift = mod_ref[3:4, :]
    scale = mod_ref[4:5, :]
    h = (x_ref[0] * (1.0 + scale) + shift).astype(BF16)

    def seg(start, width):
        return jnp.dot(h, w_ref[:, start:start + width], preferred_element_type=F32)

    xbc_scr[halo:halo + tm, :] = seg(COL_XBC, SSD_CONV_DIM)
    conv = cb_ref[...] + cw_ref[SSD_CONV - 1:SSD_CONV, :] * xbc_scr[halo:halo + tm, :]
    for k in range(SSD_CONV - 1):
        back = SSD_CONV - 1 - k
        conv = conv + cw_ref[k:k + 1, :] * xbc_scr[halo - back:halo - back + tm, :]
    xbc = _silu(conv)
    zx_ref[0, :, SSD_WIDTH:] = xbc[:, :SSD_WIDTH]
    bc_ref[0] = xbc[:, SSD_WIDTH:].astype(BF16)

    sc = seg(COL_SC, 3 * SCONV_WIDTH)
    u_scr[halo:halo + tm, :] = sc[:, SCONV_WIDTH:2 * SCONV_WIDTH] * sc[:, 2 * SCONV_WIDTH:]
    cu = scw_ref[SCONV_K - 1:SCONV_K, :] * u_scr[halo:halo + tm, :]
    for k in range(SCONV_K - 1):
        back = SCONV_K - 1 - k
        cu = cu + scw_ref[k:k + 1, :] * u_scr[halo - back:halo - back + tm, :]
    ysc_ref[0] = (sc[:, :SCONV_WIDTH] * cu).astype(BF16)

    lane = lax.broadcasted_iota(jnp.int32, (tm, LANES), 1)
    pos = lane % FOX_HEAD_DIM
    used = pos < (FOX_HEADS // HEADS_PER_VREG) * FOX_SLOT
    small_f = seg(COL_SMALL, 2 * LANES)
    log_f = jnp.where(used, -_softplus(-(small_f[:, LANES:] + fbias_ref[...])), 0.0)
    parts = jnp.concatenate(_split_bf16(log_f, 3), axis=1)
    offset = carry_scr[0:1, :]
    blocks = []
    for r in range(0, tm, PROJ_CUM):
        local3 = jnp.dot(tri_ref[...], parts[r:r + PROJ_CUM, :], preferred_element_type=F32)
        local = local3[:, :LANES] + local3[:, LANES:2 * LANES] + local3[:, 2 * LANES:]
        blocks.append(local + offset)
        offset = offset + local[PROJ_CUM - 1:PROJ_CUM, :]
    cum_f = jnp.concatenate(blocks, axis=0)
    carry_scr[...] = jnp.broadcast_to(offset, carry_scr.shape)
    hi, mid, lo = (part.astype(F32) for part in _split_bf16(cum_f * LOG2_E, FOX_PIECES))
    piece = jnp.where(pos % FOX_PIECES == 0, hi, jnp.where(pos % FOX_PIECES == 1, mid, lo))
    in_a = pos % FOX_SLOT < FOX_PIECES
    f_q = jnp.where(in_a, piece, 1.0)
    f_k = jnp.where(in_a, 1.0, -piece)

    qkv = seg(COL_QKV, 3 * FOX_WIDTH)
    q = qkv[:, :FOX_WIDTH] * (FOX_HEAD_DIM ** -0.5 * LOG2_E)
    k = qkv[:, FOX_WIDTH:2 * FOX_WIDTH]
    v = qkv[:, 2 * FOX_WIDTH:]
    for hd in range(FOX_HEADS):
        pair, hh = divmod(hd, HEADS_PER_VREG)
        ps = slice(pair * LANES, (pair + 1) * LANES)
        gs = slice(hd * LANES, (hd + 1) * LANES)
        mine = (lane // FOX_HEAD_DIM) == hh
        first = _forget_lane(hd)
        forget = (lane >= first) & (lane < first + FOX_SLOT)
        ones_col = jnp.where(lane == (1 - hh) * FOX_HEAD_DIM, 1.0, 0.0)
        kvq_ref[0, :, KVQ_Q + hd * LANES:KVQ_Q + (hd + 1) * LANES] = jnp.where(
            mine, q[:, ps], jnp.where(forget, f_q, 0.0)).astype(BF16)
        kvq_ref[0, :, KVQ_K + hd * LANES:KVQ_K + (hd + 1) * LANES] = jnp.where(
            mine, k[:, ps], jnp.where(forget, f_k, 0.0)).astype(BF16)
        kvq_ref[0, :, KVQ_V + hd * LANES:KVQ_V + (hd + 1) * LANES] = jnp.where(
            mine, v[:, ps], ones_col).astype(BF16)

    dt = _softplus(small_f[:, :LANES] + sbias_ref[...])
    a_neg = -jnp.exp(alog_ref[...])
    small_ref[0] = jnp.where(lane < LANE_A, dt, dt * a_neg)

    zx_ref[0, :, :SSD_WIDTH] = seg(COL_Z, SSD_WIDTH)


def _proj(x, mod_l, w_raw, w_tail, w_small, conv_w, conv_b, small_bias, alog_vec, sconv_w, f_bias, *, layer):
    b, seq, d = x.shape
    tm = min(PROJ_TM, seq)
    assert seq % tm == 0
    nt = seq // tm
    tok = lambda width: pl.BlockSpec((1, tm, width), lambda bi, t: (bi, t, 0))
    const2 = lambda shape: pl.BlockSpec(shape, lambda bi, t: (0, 0))
    out_shapes = (
        jax.ShapeDtypeStruct((b, seq, 2 * SSD_WIDTH), F32),
        jax.ShapeDtypeStruct((b, seq, 2 * SSD_BC), BF16),
        jax.ShapeDtypeStruct((b, seq, KVQ_WIDTH), BF16),
        jax.ShapeDtypeStruct((b, seq, SCONV_WIDTH), BF16),
        jax.ShapeDtypeStruct((b, seq, LANES), F32),
    )
    out_specs = (tok(2 * SSD_WIDTH), tok(2 * SSD_BC), tok(KVQ_WIDTH), tok(SCONV_WIDTH), tok(LANES))
    return pl.pallas_call(
        functools.partial(_proj_kernel, tm=tm),
        grid=(b, nt),
        in_specs=[
            tok(d),
            pl.BlockSpec((None, 3 * N_SUB, d), lambda bi, t: (bi, 0, 0)),
            pl.BlockSpec((None, d, RAW_WIDTH), lambda bi, t: (layer, 0, 0), pipeline_mode=pl.Buffered(1)),
            pl.BlockSpec((None, d, LANES), lambda bi, t: (layer, 0, 0), pipeline_mode=pl.Buffered(1)),
            pl.BlockSpec((None, d, 2 * LANES), lambda bi, t: (layer, 0, 0), pipeline_mode=pl.Buffered(1)),
            const2((SSD_CONV, SSD_CONV_DIM)),
            const2((1, SSD_CONV_DIM)),
            const2((1, LANES)),
            const2((1, LANES)),
            const2((SCONV_K, SCONV_WIDTH)),
            const2((1, LANES)),
            const2((PROJ_CUM, PROJ_CUM)),
        ],
        out_specs=out_specs,
        out_shape=out_shapes,
        scratch_shapes=[
            pltpu.VMEM((d, D_PROJ), BF16),
            pltpu.VMEM((tm + SUBLANES, SSD_CONV_DIM), F32),
            pltpu.VMEM((tm + SUBLANES, SCONV_WIDTH), F32),
            pltpu.VMEM((SUBLANES, LANES), F32),
        ],
        compiler_params=_params("arbitrary", "arbitrary"),
    )(x, mod_l, w_raw, w_tail, w_small, conv_w, conv_b, small_bias, alog_vec, sconv_w, f_bias,
      _lower_ones(PROJ_CUM))


def _ssd_kernel(zx_ref, bc_ref, small_ref, dskip_ref, ng_ref, tri_ref, spread_ref,
                o_ref, state_scr, *, t, chunks):
    @pl.when(pl.program_id(1) == 0)
    def _():
        state_scr[...] = jnp.zeros_like(state_scr)

    lane = lax.broadcasted_iota(jnp.int32, (t, LANES), 1)
    row_i = lax.broadcasted_iota(jnp.int32, (t, t), 0)
    col_i = lax.broadcasted_iota(jnp.int32, (t, t), 1)
    lower = row_i >= col_i
    head_of_lane = lax.broadcasted_iota(jnp.int32, (t, GROUP_WIDTH), 1) // SSD_HEAD_DIM

    for ci in range(chunks):
        rows = slice(ci * t, (ci + 1) * t)
        small = small_ref[0, rows, :]
        a_only = jnp.where((lane >= LANE_A) & (lane < LANE_A2 + SSD_HEADS), small, 0.0)
        cs3 = jnp.dot(tri_ref[...], jnp.concatenate(_split_bf16(a_only, 3), axis=1),
                      preferred_element_type=F32)
        cs = cs3[:, :LANES] + cs3[:, LANES:2 * LANES] + cs3[:, 2 * LANES:]
        cs_t = cs.T
        total = cs[t - 1:t, :]
        per_head = jnp.where(lane < LANE_A, small,
                             jnp.where(lane < LANE_A2, jnp.exp(cs), jnp.exp(total - cs)))
        wide = jnp.dot(jnp.concatenate(_split_bf16(per_head, SSD_SPREAD_PIECES), axis=1), spread_ref[...],
                       preferred_element_type=F32)
        dt_x = wide[:, :SSD_WIDTH]
        decay_in = wide[:, SSD_WIDTH:2 * SSD_WIDTH]
        decay_out = wide[:, 2 * SSD_WIDTH:]
        decay_chunk = decay_in[t - 1:t, :]

        xs = zx_ref[0, rows, SSD_WIDTH:]
        xdt = xs * dt_x
        ys = []
        for g in range(SSD_GROUPS):
            gs = slice(g * GROUP_WIDTH, (g + 1) * GROUP_WIDTH)
            ns = slice(g * SSD_STATE, (g + 1) * SSD_STATE)
            bg = bc_ref[0, rows, ns]
            cg = bc_ref[0, rows, slice(SSD_BC + ns.start, SSD_BC + ns.stop)]
            cb = lax.dot_general(cg, bg, (((1,), (1,)), ((), ())), preferred_element_type=F32)
            xg = xdt[:, gs]
            m_parts, x_parts = [], []
            for e in range(SSD_HPG):
                hd = g * SSD_HPG + e
                col = cs[:, LANE_A + hd:LANE_A + hd + 1]
                row = cs_t[LANE_A + hd:LANE_A + hd + 1, :]
                l_mat = jnp.exp(jnp.where(lower, col - row, -jnp.inf))
                m_parts.append((cb * l_mat).astype(BF16))
                x_parts.append(jnp.where(head_of_lane == e, xg, 0.0).astype(BF16))
            m_cat = jnp.concatenate(m_parts, axis=1)
            x_blk = jnp.concatenate(x_parts, axis=0)
            y_diag = jnp.dot(m_cat, x_blk, preferred_element_type=F32)

            state = state_scr[g]
            y_off = jnp.dot(cg, state.astype(BF16), preferred_element_type=F32) * decay_in[:, gs]
            xd = (xg * decay_out[:, gs]).astype(BF16)
            upd = lax.dot_general(bg, xd, (((0,), (0,)), ((), ())), preferred_element_type=F32)
            state_scr[g] = state * decay_chunk[:, gs] + upd
            ys.append(y_diag + y_off + dskip_ref[:, gs] * xs[:, gs])

        z = zx_ref[0, rows, :SSD_WIDTH]
        for g in range(SSD_GROUPS):
            gs = slice(g * GROUP_WIDTH, (g + 1) * GROUP_WIDTH)
            yg = ys[g] * _silu(z[:, gs])
            yg = yg * lax.rsqrt(jnp.mean(yg * yg, axis=-1, keepdims=True) + RMS_EPS)
            o_ref[0, rows, gs] = (yg * ng_ref[:, gs]).astype(BF16)


def _head_spread():
    src = jnp.arange(SSD_SPREAD_PIECES * LANES)[:, None] % LANES
    col = jnp.arange(3 * SSD_WIDTH)[None, :]
    first_lane = jnp.array([LANE_DT, LANE_A, LANE_A2])[col // SSD_WIDTH]
    return (src == first_lane + (col % SSD_WIDTH) // SSD_HEAD_DIM).astype(BF16)


def _ssd(zx, bc, small, d_skip_x, norm_g):
    b, seq, _ = zx.shape
    t = min(SSD_T, seq)
    rows = min(SSD_ROWS, seq)
    assert seq % rows == 0 and rows % t == 0
    tok = lambda width: pl.BlockSpec((1, rows, width), lambda bi, c: (bi, c, 0))
    const2 = lambda shape: pl.BlockSpec(shape, lambda bi, c: (0, 0))
    return pl.pallas_call(
        functools.partial(_ssd_kernel, t=t, chunks=rows // t),
        grid=(b, seq // rows),
        in_specs=[tok(2 * SSD_WIDTH), tok(2 * SSD_BC), tok(LANES),
                  const2((1, SSD_WIDTH)), const2((1, SSD_WIDTH)),
                  const2((t, t)), const2((SSD_SPREAD_PIECES * LANES, 3 * SSD_WIDTH))],
        out_specs=tok(SSD_WIDTH),
        out_shape=jax.ShapeDtypeStruct((b, seq, SSD_WIDTH), BF16),
        scratch_shapes=[pltpu.VMEM((SSD_GROUPS, SSD_STATE, GROUP_WIDTH), F32)],
        compiler_params=_params("arbitrary", "arbitrary"),
    )(zx, bc, small, d_skip_x, norm_g, _lower_ones(t), _head_spread())


def _fox_kernel(q_ref, kv_ref, o_ref, m_scr, acc_scr, *, t):
    qi = pl.program_id(1)
    lane_head = lax.broadcasted_iota(jnp.int32, (t, LANES), 1) // FOX_HEAD_DIM
    row_i = lax.broadcasted_iota(jnp.int32, (t, t), 0)
    col_i = lax.broadcasted_iota(jnp.int32, (t, t), 1)
    causal = row_i >= col_i
    heads_per_vreg = LANES // FOX_HEAD_DIM

    def block(start, width, masked):
        ks = pl.ds(start, width)
        group = lambda hd, base=0: slice(base + hd * LANES, base + (hd + 1) * LANES)

        def logits(hd):
            s = lax.dot_general(q_ref[0, :, group(hd)], kv_ref[0, ks, group(hd, KVQ_K)],
                                (((1,), (1,)), ((), ())), preferred_element_type=F32)
            return jnp.where(causal, s, -jnp.inf) if masked else s

        def update(hd, s):
            m_prev = m_scr[hd]
            m_new = jnp.maximum(m_prev, jnp.max(s, axis=-1, keepdims=True))
            p = jnp.exp2(s - jnp.concatenate([m_new] * (width // LANES), axis=1)).astype(BF16)
            acc_scr[hd] = jnp.exp2(m_prev - m_new) * acc_scr[hd] + jnp.dot(
                p, kv_ref[0, ks, group(hd, KVQ_V)], preferred_element_type=F32)
            m_scr[hd] = m_new

        s_next = logits(0)
        for hd in range(FOX_HEADS):
            s_cur = s_next
            if hd + 1 < FOX_HEADS:
                s_next = logits(hd + 1)
            update(hd, s_cur)

    m_scr[...] = jnp.full(m_scr.shape, -jnp.inf, F32)
    acc_scr[...] = jnp.zeros(acc_scr.shape, F32)

    wide = 2 * t
    assert kv_ref.shape[1] % wide == 0

    @pl.loop(0, qi // 2)
    def _(pj):
        block(pl.multiple_of(pj * wide, wide), wide, False)

    @pl.when(qi % 2 == 1)
    def _():
        block(pl.multiple_of((qi - 1) * t, t), t, False)

    block(pl.multiple_of(qi * t, t), t, True)
    for pair in range(FOX_HEADS // heads_per_vreg):
        out = jnp.zeros((t, LANES), F32)
        for hh in range(heads_per_vreg):
            acc = acc_scr[pair * heads_per_vreg + hh]
            ones_lane = (1 - hh) * FOX_HEAD_DIM
            row_sum = acc[:, ones_lane:ones_lane + 1]
            out = out + jnp.where(lane_head == hh, acc, 0.0) / row_sum
        o_ref[0, :, pair * LANES:(pair + 1) * LANES] = out.astype(BF16)


def _fox(kvq):
    b, seq, _ = kvq.shape
    t = min(FOX_T, seq)
    assert seq % t == 0 and KVQ_K == 0 and KVQ_V == FOX_GROUP and KVQ_Q == 2 * FOX_GROUP
    return pl.pallas_call(
        functools.partial(_fox_kernel, t=t),
        grid=(b, seq // t),
        in_specs=[
            pl.BlockSpec((1, t, FOX_GROUP), lambda bi, i: (bi, i, KVQ_Q // FOX_GROUP)),
            pl.BlockSpec((1, seq, 2 * FOX_GROUP), lambda bi, i: (bi, 0, 0)),
        ],
        out_specs=pl.BlockSpec((1, t, FOX_WIDTH), lambda bi, i: (bi, i, 0)),
        out_shape=jax.ShapeDtypeStruct((b, seq, FOX_WIDTH), BF16),
        scratch_shapes=[pltpu.VMEM((FOX_HEADS, t, LANES), F32), pltpu.VMEM((FOX_HEADS, t, LANES), F32)],
        compiler_params=_params("arbitrary", "arbitrary"),
    )(kvq, kvq)


def _lower_ones(n):
    return jnp.tril(jnp.ones((n, n), BF16))


def _spread_forget(f):
    lead = f.shape[:-1]
    order = sorted(range(FOX_HEADS), key=_forget_lane)
    blocks, lane = [], 0
    for hd in order:
        blocks.append(jnp.zeros(lead + (_forget_lane(hd) - lane,), f.dtype))
        blocks.append(jnp.broadcast_to(f[..., hd:hd + 1], lead + (FOX_SLOT,)))
        lane = _forget_lane(hd) + FOX_SLOT
    blocks.append(jnp.zeros(lead + (LANES - lane,), f.dtype))
    return jnp.concatenate(blocks, axis=-1)


def _mix_w_in_extras(w):
    tail_start = RAW_WIDTH // LANES * LANES
    tail = jnp.pad(w[..., tail_start:], ((0, 0), (0, 0), (0, LANES - (RAW_WIDTH - tail_start))))
    dt_cols = w[..., RAW_DT:RAW_QKV]
    pad = jnp.zeros(w.shape[:-1] + (LANES - 3 * SSD_HEADS,), w.dtype)
    small = jnp.concatenate([dt_cols, dt_cols, dt_cols, pad, _spread_forget(w[..., RAW_F:RAW_SC])], axis=-1)
    return tail, small.astype(BF16)


def _lane_vec(pieces):
    row = jnp.zeros((LANES,), F32)
    for off, vec in pieces:
        row = row.at[off:off + vec.shape[0]].set(vec.astype(F32))
    return row.reshape(1, LANES)


def kernel(x, c, ln_in_g, ln_in_b, ada_w, ada_b, ffn1_w_in, ffn1_w_out, mix_w_in, mix_w_out, ssd_conv_w, ssd_conv_b, ssd_dt_bias, ssd_a_log, ssd_d, ssd_norm_g, fox_f_bias, sconv_w, ffn2_w_in, ffn2_w_out, ln_g, ln_b):
    b, seq, d = x.shape
    depth = ada_w.shape[0]
    mod = _ada(c, ada_w, ada_b)
    mod = mod.reshape(depth, mod.shape[1], 3 * N_SUB, d)

    w1_in, w1_out, w2_in, w2_out = ffn1_w_in, ffn1_w_out, ffn2_w_in, ffn2_w_out
    wm_tail, wm_small = _mix_w_in_extras(mix_w_in)
    wm_out = mix_w_out.astype(BF16)

    x2d = x.reshape(b * seq, d)
    for l in range(depth):
        mod_l = mod[l]
        x2d = _ffn(x2d, mod_l, w1_in, w1_out, ln_g[l, 0], ln_b[l, 0], layer=l, sub=0, seq=seq,
                   pre=(ln_in_g, ln_in_b) if l == 0 else None)

        small_bias = _lane_vec([(lane, ssd_dt_bias[l]) for lane in (LANE_DT, LANE_A, LANE_A2)])
        alog_vec = _lane_vec([(LANE_A, ssd_a_log[l]), (LANE_A2, ssd_a_log[l])])
        zx, bc, kvq, y_sc, small = _proj(
            x2d.reshape(b, seq, d), mod_l, mix_w_in, wm_tail, wm_small,
            ssd_conv_w[l], ssd_conv_b[l].reshape(1, SSD_CONV_DIM), small_bias, alog_vec, sconv_w[l],
            _spread_forget(fox_f_bias[l]).reshape(1, LANES), layer=l)
        d_skip_x = jnp.repeat(ssd_d[l], SSD_HEAD_DIM).reshape(1, SSD_WIDTH)
        y_ssd = _ssd(zx, bc, small, d_skip_x, ssd_norm_g[l].reshape(1, SSD_WIDTH))
        y_fox = _fox(kvq)
        mixer_tail = (y_ssd.reshape(b * seq, SSD_WIDTH), y_fox.reshape(b * seq, FOX_WIDTH),
                      y_sc.reshape(b * seq, SCONV_WIDTH), wm_out, ln_g[l, 1], ln_b[l, 1])
        x2d = _ffn(x2d, mod_l, w2_in, w2_out, ln_g[l, 2], ln_b[l, 2], layer=l, sub=2, seq=seq,
                   mix=mixer_tail)
    return x2d.reshape(b, seq, d)
```

```python
import functools
import math

import jax
import jax.numpy as jnp
from jax import lax
from jax.experimental import pallas as pl
from jax.experimental.pallas import tpu as pltpu

F32 = jnp.float32
BF16 = jnp.bfloat16

D_MODEL = 1024
DEPTH = 2
N_SUB = 3
D_FF = 2816
SSD_WIDTH = 512
SSD_HEAD_DIM = 64
SSD_HEADS = 8
SSD_GROUPS = 2
SSD_HPG = SSD_HEADS // SSD_GROUPS
SSD_STATE = 128
SSD_CONV = 4
SSD_BC = SSD_GROUPS * SSD_STATE
SSD_CONV_DIM = SSD_WIDTH + 2 * SSD_BC
GROUP_WIDTH = SSD_WIDTH // SSD_GROUPS
FOX_WIDTH = 256
FOX_HEAD_DIM = 64
FOX_HEADS = 4
SCONV_WIDTH = 256
SCONV_K = 3
ALPHA = (2 * DEPTH) ** 0.25
LN_EPS = 1e-5
RMS_EPS = 1e-5
LOG2_E = math.log2(math.e)

LANES = 128
SUBLANES = 8
VMEM_LIMIT = 56 * 1024 * 1024

RAW_DT = SSD_WIDTH + SSD_CONV_DIM
RAW_QKV = RAW_DT + SSD_HEADS
RAW_F = RAW_QKV + 3 * FOX_WIDTH
RAW_SC = RAW_F + FOX_HEADS
RAW_WIDTH = RAW_SC + 3 * SCONV_WIDTH
COL_Z = 0
COL_XBC = COL_Z + SSD_WIDTH
COL_QKV = COL_XBC + SSD_CONV_DIM
COL_SC = COL_QKV + 3 * FOX_WIDTH
COL_SMALL = COL_SC + 3 * SCONV_WIDTH
COL_F = COL_SMALL + LANES
D_PROJ = COL_F + LANES
LANE_DT = 0
LANE_A = 8
LANE_A2 = 16
FOX_PIECES = 3
FOX_SLOT = 2 * FOX_PIECES
FOX_GROUP = FOX_HEADS * LANES
HEADS_PER_VREG = LANES // FOX_HEAD_DIM
KVQ_K = 0
KVQ_V = FOX_GROUP
KVQ_Q = 2 * FOX_GROUP
KVQ_WIDTH = 3 * FOX_GROUP


def _forget_lane(hd):
    pair, hh = divmod(hd, HEADS_PER_VREG)
    return (HEADS_PER_VREG - 1 - hh) * FOX_HEAD_DIM + pair * FOX_SLOT

FFN_TM = 1024
FFN_TM_MIX = 512
FFN_TF = 256
FFN_HALF = 256
FFN_W_IN_ROWS = 64
FFN_W_OUT_ROWS = 256
PROJ_TM = 512
PROJ_CUM = 128
SSD_T = 256
SSD_ROWS = 2048
SSD_SPREAD_PIECES = 2
FOX_T = 512
ADA_TN = 2304


def _layer_norm(x, g, b):
    mu = jnp.mean(x, axis=-1, keepdims=True)
    xc = x - mu
    var = jnp.mean(xc * xc, axis=-1, keepdims=True)
    return xc * lax.rsqrt(var + LN_EPS) * g + b


def _silu(x):
    return x / (1.0 + jnp.exp2(x * (-LOG2_E)))


def _softplus(x):
    return jnp.maximum(x, 0.0) + jnp.log1p(jnp.exp(-jnp.abs(x)))


def _split_bf16(x, pieces):
    out = []
    for _ in range(pieces):
        part = x.astype(BF16)
        out.append(part)
        x = x - part.astype(F32)
    return out


def _params(*semantics):
    return pltpu.CompilerParams(dimension_semantics=semantics, vmem_limit_bytes=VMEM_LIMIT)


def _ada_kernel(c_ref, w_ref, b_ref, o_ref, *, batch):
    act_hi, act_lo = _split_bf16(_silu(c_ref[...]), 2)
    row = lax.broadcasted_iota(jnp.int32, act_hi.shape, 0)
    lhs = jnp.where(row < batch, act_hi, act_lo)
    w_hi, w_lo = _split_bf16(w_ref[...], 2)
    y = jnp.dot(lhs, w_hi, preferred_element_type=F32) + jnp.dot(lhs, w_lo, preferred_element_type=F32)
    rows = y.shape[0]
    o_ref[...] = y + pltpu.roll(y, rows - batch, axis=0) + b_ref[...]


def _ada(c, ada_w, ada_b):
    depth, d, n = ada_w.shape
    batch = c.shape[0]
    rows = -(-2 * batch // SUBLANES) * SUBLANES
    c_pad = jnp.pad(jnp.concatenate([c, c], axis=0), ((0, rows - 2 * batch), (0, 0)))
    return pl.pallas_call(
        functools.partial(_ada_kernel, batch=batch),
        grid=(depth, n // ADA_TN),
        in_specs=[
            pl.BlockSpec((rows, d), lambda l, j: (0, 0)),
            pl.BlockSpec((None, d, ADA_TN), lambda l, j: (l, 0, j)),
            pl.BlockSpec((None, 1, ADA_TN), lambda l, j: (l, 0, j)),
        ],
        out_specs=pl.BlockSpec((None, rows, ADA_TN), lambda l, j: (l, 0, j)),
        out_shape=jax.ShapeDtypeStruct((depth, rows, n), F32),
        compiler_params=_params("arbitrary", "arbitrary"),
    )(c_pad, ada_w, ada_b.reshape(depth, 1, n))


def _ffn_kernel(*refs, layer, sub, pre_ln, mix):
    x_ref, mod_ref, wi_hbm, wo_hbm, lng_ref, lnb_ref = refs[:6]
    n_scratch = 6
    extra = list(refs[6:-1 - n_scratch])
    o_ref = refs[-1 - n_scratch]
    act_scr, wi_ref, wo_ref, stage_i, stage_o, sem = refs[-n_scratch:]
    if pre_ln:
        ing_ref, inb_ref = extra[:2]
        extra = extra[2:]
    if mix:
        yssd_ref, yfox_ref, ysc_ref, wm_ref, mlng_ref, mlnb_ref = extra
    shift = mod_ref[3 * sub:3 * sub + 1, :]
    scale = mod_ref[3 * sub + 1:3 * sub + 2, :]
    gain = mod_ref[3 * sub + 2:3 * sub + 3, :]
    halves = [slice(r, r + FFN_HALF) for r in range(0, x_ref.shape[0], FFN_HALF)]

    def entry(rows):
        x = x_ref[rows, :]
        if pre_ln:
            x = _layer_norm(x, ing_ref[...], inb_ref[...])
        if mix:
            y_mix = jnp.concatenate([yssd_ref[rows, :], yfox_ref[rows, :], ysc_ref[rows, :]], axis=1)
            y = jnp.dot(y_mix, wm_ref[...], preferred_element_type=F32)
            x = _layer_norm(ALPHA * x + mod_ref[5:6, :] * y, mlng_ref[...], mlnb_ref[...])
        return x

    @pl.when(pl.program_id(0) == 0)
    def _():
        def stream(src_hbm, dst_ref, stage, which, block_rows):
            n_blocks = src_hbm.shape[1] // block_rows
            copy = lambda b: pltpu.make_async_copy(
                src_hbm.at[layer, b * block_rows:(b + 1) * block_rows, :], stage.at[b % 2],
                sem.at[which, b % 2])
            copy(0).start()
            for b in range(n_blocks):
                if b + 1 < n_blocks:
                    copy(b + 1).start()
                copy(b).wait()
                dst_ref[b * block_rows:(b + 1) * block_rows, :] = stage[b % 2].astype(BF16)

        stream(wi_hbm, wi_ref, stage_i, 0, FFN_W_IN_ROWS)
        stream(wo_hbm, wo_ref, stage_o, 1, FFN_W_OUT_ROWS)

    xs = [entry(rows) for rows in halves]
    ys = []
    for rows, x in zip(halves, xs):
        h = (x * (1.0 + scale) + shift).astype(BF16)
        for c in range(D_FF // FFN_TF):
            cols = slice(c * FFN_TF, (c + 1) * FFN_TF)
            gate = jnp.dot(h, wi_ref[:, cols], preferred_element_type=F32)
            up = jnp.dot(h, wi_ref[:, D_FF + c * FFN_TF:D_FF + (c + 1) * FFN_TF],
                         preferred_element_type=F32)
            act_scr[rows, cols] = (_silu(gate) * up).astype(BF16)
        ys.append(jnp.dot(act_scr[rows, :], wo_ref[...], preferred_element_type=F32))
    for rows, x, y in zip(halves, xs, ys):
        o_ref[rows, :] = _layer_norm(ALPHA * x + (0.5 * gain) * y, lng_ref[...], lnb_ref[...])


def _ffn(x2d, mod_l, w_in, w_out, ln_g, ln_b, *, layer, sub, seq, pre=None, mix=None):
    n_tok, d = x2d.shape
    tm = min(FFN_TM if mix is None else FFN_TM_MIX, seq)
    assert seq % tm == 0 and n_tok % seq == 0 and D_FF % FFN_TF == 0 and tm % FFN_HALF == 0
    tiles_per_seq = seq // tm
    row = lambda v: v.reshape(1, d)
    resident = lambda shape: pl.BlockSpec(shape, lambda i: (0, 0), pipeline_mode=pl.Buffered(1))
    weight = lambda shape: pl.BlockSpec((None,) + shape, lambda i: (layer, 0, 0),
                                        pipeline_mode=pl.Buffered(1))
    in_specs = [
        pl.BlockSpec((tm, d), lambda i: (i, 0)),
        pl.BlockSpec((None, 3 * N_SUB, d), lambda i: (i // tiles_per_seq, 0, 0)),
        pl.BlockSpec(memory_space=pl.ANY),
        pl.BlockSpec(memory_space=pl.ANY),
        resident((1, d)),
        resident((1, d)),
    ]
    args = [x2d, mod_l, w_in, w_out, row(ln_g), row(ln_b)]
    if pre is not None:
        in_specs += [resident((1, d))] * 2
        args += [row(pre[0]), row(pre[1])]
    if mix is not None:
        y_ssd, y_fox, y_sc, w_mix, mix_g, mix_b = mix
        tok = lambda width: pl.BlockSpec((tm, width), lambda i: (i, 0))
        in_specs += [tok(SSD_WIDTH), tok(FOX_WIDTH), tok(SCONV_WIDTH), weight((d, d)),
                     resident((1, d)), resident((1, d))]
        args += [y_ssd, y_fox, y_sc, w_mix, row(mix_g), row(mix_b)]
    return pl.pallas_call(
        functools.partial(_ffn_kernel, layer=layer, sub=sub, pre_ln=pre is not None, mix=mix is not None),
        grid=(n_tok // tm,),
        in_specs=in_specs,
        out_specs=pl.BlockSpec((tm, d), lambda i: (i, 0)),
        out_shape=jax.ShapeDtypeStruct((n_tok, d), F32),
        scratch_shapes=[
            pltpu.VMEM((tm, D_FF), BF16),
            pltpu.VMEM((d, 2 * D_FF), BF16),
            pltpu.VMEM((D_FF, d), BF16),
            pltpu.VMEM((2, FFN_W_IN_ROWS, 2 * D_FF), F32),
            pltpu.VMEM((2, FFN_W_OUT_ROWS, d), F32),
            pltpu.SemaphoreType.DMA((2, 2)),
        ],
        compiler_params=_params("arbitrary"),
    )(*args)


def _shift_lanes_left(blocks, shift, n_out):
    lane = lax.broadcasted_iota(jnp.int32, blocks[0].shape, 1)
    rolled = [pltpu.roll(blk, LANES - shift, axis=1) for blk in blocks]
    return [jnp.where(lane < LANES - shift, rolled[k], rolled[k + 1]) for k in range(n_out)]


def _proj_kernel(x_ref, mod_ref, wraw_ref, wtail_ref, wsmall_ref, cw_ref, cb_ref, sbias_ref, alog_ref,
                 scw_ref, fbias_ref, tri_ref,
                 zx_ref, bc_ref, kvq_ref, ysc_ref, small_ref,
                 w_ref, xbc_scr, u_scr, carry_scr, *, tm):
    t = pl.program_id(1)
    halo = SUBLANES

    @pl.when((pl.program_id(0) == 0) & (t == 0))
    def _():
        raw_block = lambda m: wraw_ref[:, m * LANES:(m + 1) * LANES]
        w_ref[:, :COL_QKV] = wraw_ref[:, :COL_QKV].astype(BF16)
        for raw_start, col, width in ((RAW_QKV, COL_QKV, 3 * FOX_WIDTH), (RAW_SC, COL_SC, 3 * SCONV_WIDTH)):
            first, shift = divmod(raw_start, LANES)
            n_out = width // LANES
            blocks = [raw_block(m) if (m + 1) * LANES <= RAW_WIDTH else wtail_ref[...]
                      for m in range(first, first + n_out + 1)]
            for k, blk in enumerate(_shift_lanes_left(blocks, shift, n_out)):
                w_ref[:, col + k * LANES:col + (k + 1) * LANES] = blk.astype(BF16)
        w_ref[:, COL_SMALL:] = wsmall_ref[...]

    @pl.when(t == 0)
    def _():
        xbc_scr[0:halo, :] = jnp.zeros((halo, SSD_CONV_DIM), F32)
        u_scr[0:halo, :] = jnp.zeros((halo, SCONV_WIDTH), F32)
        carry_scr[...] = jnp.zeros_like(carry_scr)

    @pl.when(t > 0)
    def _():
        xbc_scr[0:halo, :] = xbc_scr[tm:tm + halo, :]
        u_scr[0:halo, :] = u_scr[tm:tm + halo, :]

    shift = mod_ref[3:4, :]
    scale = mod_ref[4:5, :]
    h = (x_ref[0] * (1.0 + scale) + shift).astype(BF16)

    def seg(start, width):
        return jnp.dot(h, w_ref[:, start:start + width], preferred_element_type=F32)

    xbc_scr[halo:halo + tm, :] = seg(COL_XBC, SSD_CONV_DIM)
    conv = cb_ref[...] + cw_ref[SSD_CONV - 1:SSD_CONV, :] * xbc_scr[halo:halo + tm, :]
    for k in range(SSD_CONV - 1):
        back = SSD_CONV - 1 - k
        conv = conv + cw_ref[k:k + 1, :] * xbc_scr[halo - back:halo - back + tm, :]
    xbc = _silu(conv)
    zx_ref[0, :, SSD_WIDTH:] = xbc[:, :SSD_WIDTH]
    bc_ref[0] = xbc[:, SSD_WIDTH:].astype(BF16)

    sc = seg(COL_SC, 3 * SCONV_WIDTH)
    u_scr[halo:halo + tm, :] = sc[:, SCONV_WIDTH:2 * SCONV_WIDTH] * sc[:, 2 * SCONV_WIDTH:]
    cu = scw_ref[SCONV_K - 1:SCONV_K, :] * u_scr[halo:halo + tm, :]
    for k in range(SCONV_K - 1):
        back = SCONV_K - 1 - k
        cu = cu + scw_ref[k:k + 1, :] * u_scr[halo - back:halo - back + tm, :]
    ysc_ref[0] = (sc[:, :SCONV_WIDTH] * cu).astype(BF16)

    lane = lax.broadcasted_iota(jnp.int32, (tm, LANES), 1)
    pos = lane % FOX_HEAD_DIM
    used = pos < (FOX_HEADS // HEADS_PER_VREG) * FOX_SLOT
    small_f = seg(COL_SMALL, 2 * LANES)
    log_f = jnp.where(used, -_softplus(-(small_f[:, LANES:] + fbias_ref[...])), 0.0)
    parts = jnp.concatenate(_split_bf16(log_f, 3), axis=1)
    offset = carry_scr[0:1, :]
    blocks = []
    for r in range(0, tm, PROJ_CUM):
        local3 = jnp.dot(tri_ref[...], parts[r:r + PROJ_CUM, :], preferred_element_type=F32)
        local = local3[:, :LANES] + local3[:, LANES:2 * LANES] + local3[:, 2 * LANES:]
        blocks.append(local + offset)
        offset = offset + local[PROJ_CUM - 1:PROJ_CUM, :]
    cum_f = jnp.concatenate(blocks, axis=0)
    carry_scr[...] = jnp.broadcast_to(offset, carry_scr.shape)
    hi, mid, lo = (part.astype(F32) for part in _split_bf16(cum_f * LOG2_E, FOX_PIECES))
    piece = jnp.where(pos % FOX_PIECES == 0, hi, jnp.where(pos % FOX_PIECES == 1, mid, lo))
    in_a = pos % FOX_SLOT < FOX_PIECES
    f_q = jnp.where(in_a, piece, 1.0)
    f_k = jnp.where(in_a, 1.0, -piece)

    qkv = seg(COL_QKV, 3 * FOX_WIDTH)
    q = qkv[:, :FOX_WIDTH] * (FOX_HEAD_DIM ** -0.5 * LOG2_E)
    k = qkv[:, FOX_WIDTH:2 * FOX_WIDTH]
    v = qkv[:, 2 * FOX_WIDTH:]
    for hd in range(FOX_HEADS):
        pair, hh = divmod(hd, HEADS_PER_VREG)
        ps = slice(pair * LANES, (pair + 1) * LANES)
        gs = slice(hd * LANES, (hd + 1) * LANES)
        mine = (lane // FOX_HEAD_DIM) == hh
        first = _forget_lane(hd)
        forget = (lane >= first) & (lane < first + FOX_SLOT)
        ones_col = jnp.where(lane == (1 - hh) * FOX_HEAD_DIM, 1.0, 0.0)
        kvq_ref[0, :, KVQ_Q + hd * LANES:KVQ_Q + (hd + 1) * LANES] = jnp.where(
            mine, q[:, ps], jnp.where(forget, f_q, 0.0)).astype(BF16)
        kvq_ref[0, :, KVQ_K + hd * LANES:KVQ_K + (hd + 1) * LANES] = jnp.where(
            mine, k[:, ps], jnp.where(forget, f_k, 0.0)).astype(BF16)
        kvq_ref[0, :, KVQ_V + hd * LANES:KVQ_V + (hd + 1) * LANES] = jnp.where(
            mine, v[:, ps], ones_col).astype(BF16)

    dt = _softplus(small_f[:, :LANES] + sbias_ref[...])
    a_neg = -jnp.exp(alog_ref[...])
    small_ref[0] = jnp.where(lane < LANE_A, dt, dt * a_neg)

    zx_ref[0, :, :SSD_WIDTH] = seg(COL_Z, SSD_WIDTH)


def _proj(x, mod_l, w_raw, w_tail, w_small, conv_w, conv_b, small_bias, alog_vec, sconv_w, f_bias, *, layer):
    b, seq, d = x.shape
    tm = min(PROJ_TM, seq)
    assert seq % tm == 0
    nt = seq // tm
    tok = lambda width: pl.BlockSpec((1, tm, width), lambda bi, t: (bi, t, 0))
    const2 = lambda shape: pl.BlockSpec(shape, lambda bi, t: (0, 0))
    out_shapes = (
        jax.ShapeDtypeStruct((b, seq, 2 * SSD_WIDTH), F32),
        jax.ShapeDtypeStruct((b, seq, 2 * SSD_BC), BF16),
        jax.ShapeDtypeStruct((b, seq, KVQ_WIDTH), BF16),
        jax.ShapeDtypeStruct((b, seq, SCONV_WIDTH), BF16),
        jax.ShapeDtypeStruct((b, seq, LANES), F32),
    )
    out_specs = (tok(2 * SSD_WIDTH), tok(2 * SSD_BC), tok(KVQ_WIDTH), tok(SCONV_WIDTH), tok(LANES))
    return pl.pallas_call(
        functools.partial(_proj_kernel, tm=tm),
        grid=(b, nt),
        in_specs=[
            tok(d),
            pl.BlockSpec((None, 3 * N_SUB, d), lambda bi, t: (bi, 0, 0)),
            pl.BlockSpec((None, d, RAW_WIDTH), lambda bi, t: (layer, 0, 0), pipeline_mode=pl.Buffered(1)),
            pl.BlockSpec((None, d, LANES), lambda bi, t: (layer, 0, 0), pipeline_mode=pl.Buffered(1)),
            pl.BlockSpec((None, d, 2 * LANES), lambda bi, t: (layer, 0, 0), pipeline_mode=pl.Buffered(1)),
            const2((SSD_CONV, SSD_CONV_DIM)),
            const2((1, SSD_CONV_DIM)),
            const2((1, LANES)),
            const2((1, LANES)),
            const2((SCONV_K, SCONV_WIDTH)),
            const2((1, LANES)),
            const2((PROJ_CUM, PROJ_CUM)),
        ],
        out_specs=out_specs,
        out_shape=out_shapes,
        scratch_shapes=[
            pltpu.VMEM((d, D_PROJ), BF16),
            pltpu.VMEM((tm + SUBLANES, SSD_CONV_DIM), F32),
            pltpu.VMEM((tm + SUBLANES, SCONV_WIDTH), F32),
            pltpu.VMEM((SUBLANES, LANES), F32),
        ],
        compiler_params=_params("arbitrary", "arbitrary"),
    )(x, mod_l, w_raw, w_tail, w_small, conv_w, conv_b, small_bias, alog_vec, sconv_w, f_bias,
      _lower_ones(PROJ_CUM))


def _ssd_kernel(zx_ref, bc_ref, small_ref, dskip_ref, ng_ref, tri_ref, spread_ref,
                o_ref, state_scr, *, t, chunks):
    @pl.when(pl.program_id(1) == 0)
    def _():
        state_scr[...] = jnp.zeros_like(state_scr)

    lane = lax.broadcasted_iota(jnp.int32, (t, LANES), 1)
    row_i = lax.broadcasted_iota(jnp.int32, (t, t), 0)
    col_i = lax.broadcasted_iota(jnp.int32, (t, t), 1)
    lower = row_i >= col_i
    head_of_lane = lax.broadcasted_iota(jnp.int32, (t, GROUP_WIDTH), 1) // SSD_HEAD_DIM

    for ci in range(chunks):
        rows = slice(ci * t, (ci + 1) * t)
        small = small_ref[0, rows, :]
        a_only = jnp.where((lane >= LANE_A) & (lane < LANE_A2 + SSD_HEADS), small, 0.0)
        cs3 = jnp.dot(tri_ref[...], jnp.concatenate(_split_bf16(a_only, 3), axis=1),
                      preferred_element_type=F32)
        cs = cs3[:, :LANES] + cs3[:, LANES:2 * LANES] + cs3[:, 2 * LANES:]
        cs_t = cs.T
        total = cs[t - 1:t, :]
        per_head = jnp.where(lane < LANE_A, small,
                             jnp.where(lane < LANE_A2, jnp.exp(cs), jnp.exp(total - cs)))
        wide = jnp.dot(jnp.concatenate(_split_bf16(per_head, SSD_SPREAD_PIECES), axis=1), spread_ref[...],
                       preferred_element_type=F32)
        dt_x = wide[:, :SSD_WIDTH]
        decay_in = wide[:, SSD_WIDTH:2 * SSD_WIDTH]
        decay_out = wide[:, 2 * SSD_WIDTH:]
        decay_chunk = decay_in[t - 1:t, :]

        xs = zx_ref[0, rows, SSD_WIDTH:]
        xdt = xs * dt_x
        ys = []
        for g in range(SSD_GROUPS):
            gs = slice(g * GROUP_WIDTH, (g + 1) * GROUP_WIDTH)
            ns = slice(g * SSD_STATE, (g + 1) * SSD_STATE)
            bg = bc_ref[0, rows, ns]
            cg = bc_ref[0, rows, slice(SSD_BC + ns.start, SSD_BC + ns.stop)]
            cb = lax.dot_general(cg, bg, (((1,), (1,)), ((), ())), preferred_element_type=F32)
            xg = xdt[:, gs]
            m_parts, x_parts = [], []
            for e in range(SSD_HPG):
                hd = g * SSD_HPG + e
                col = cs[:, LANE_A + hd:LANE_A + hd + 1]
                row = cs_t[LANE_A + hd:LANE_A + hd + 1, :]
                l_mat = jnp.exp(jnp.where(lower, col - row, -jnp.inf))
                m_parts.append((cb * l_mat).astype(BF16))
                x_parts.append(jnp.where(head_of_lane == e, xg, 0.0).astype(BF16))
            m_cat = jnp.concatenate(m_parts, axis=1)
            x_blk = jnp.concatenate(x_parts, axis=0)
            y_diag = jnp.dot(m_cat, x_blk, preferred_element_type=F32)

            state = state_scr[g]
            y_off = jnp.dot(cg, state.astype(BF16), preferred_element_type=F32) * decay_in[:, gs]
            xd = (xg * decay_out[:, gs]).astype(BF16)
            upd = lax.dot_general(bg, xd, (((0,), (0,)), ((), ())), preferred_element_type=F32)
            state_scr[g] = state * decay_chunk[:, gs] + upd
            ys.append(y_diag + y_off + dskip_ref[:, gs] * xs[:, gs])

        z = zx_ref[0, rows, :SSD_WIDTH]
        for g in range(SSD_GROUPS):
            gs = slice(g * GROUP_WIDTH, (g + 1) * GROUP_WIDTH)
            yg = ys[g] * _silu(z[:, gs])
            yg = yg * lax.rsqrt(jnp.mean(yg * yg, axis=-1, keepdims=True) + RMS_EPS)
            o_ref[0, rows, gs] = (yg * ng_ref[:, gs]).astype(BF16)


def _head_spread():
    src = jnp.arange(SSD_SPREAD_PIECES * LANES)[:, None] % LANES
    col = jnp.arange(3 * SSD_WIDTH)[None, :]
    first_lane = jnp.array([LANE_DT, LANE_A, LANE_A2])[col // SSD_WIDTH]
    return (src == first_lane + (col % SSD_WIDTH) // SSD_HEAD_DIM).astype(BF16)


def _ssd(zx, bc, small, d_skip_x, norm_g):
    b, seq, _ = zx.shape
    t = min(SSD_T, seq)
    rows = min(SSD_ROWS, seq)
    assert seq % rows == 0 and rows % t == 0
    tok = lambda width: pl.BlockSpec((1, rows, width), lambda bi, c: (bi, c, 0))
    const2 = lambda shape: pl.BlockSpec(shape, lambda bi, c: (0, 0))
    return pl.pallas_call(
        functools.partial(_ssd_kernel, t=t, chunks=rows // t),
        grid=(b, seq // rows),
        in_specs=[tok(2 * SSD_WIDTH), tok(2 * SSD_BC), tok(LANES),
                  const2((1, SSD_WIDTH)), const2((1, SSD_WIDTH)),
                  const2((t, t)), const2((SSD_SPREAD_PIECES * LANES, 3 * SSD_WIDTH))],
        out_specs=tok(SSD_WIDTH),
        out_shape=jax.ShapeDtypeStruct((b, seq, SSD_WIDTH), BF16),
        scratch_shapes=[pltpu.VMEM((SSD_GROUPS, SSD_STATE, GROUP_WIDTH), F32)],
        compiler_params=_params("arbitrary", "arbitrary"),
    )(zx, bc, small, d_skip_x, norm_g, _lower_ones(t), _head_spread())


def _fox_kernel(q_ref, kv_ref, o_ref, m_scr, acc_scr, *, t):
    qi = pl.program_id(1)
    lane_head = lax.broadcasted_iota(jnp.int32, (t, LANES), 1) // FOX_HEAD_DIM
    heads_per_vreg = LANES // FOX_HEAD_DIM

    def block(start, width, masked):
        ks = pl.ds(start, width)
        group = lambda hd, base=0: slice(base + hd * LANES, base + (hd + 1) * LANES)
        if masked:
            row_i = lax.broadcasted_iota(jnp.int32, (t, width), 0)
            col_i = lax.broadcasted_iota(jnp.int32, (t, width), 1)
            visible = row_i + (width - t) >= col_i

        def logits(hd):
            s = lax.dot_general(q_ref[0, :, group(hd)], kv_ref[0, ks, group(hd, KVQ_K)],
                                (((1,), (1,)), ((), ())), preferred_element_type=F32)
            return jnp.where(visible, s, -jnp.inf) if masked else s

        def update(hd, s):
            m_prev = m_scr[hd]
            m_new = jnp.maximum(m_prev, jnp.max(s, axis=-1, keepdims=True))
            p = jnp.exp2(s - jnp.concatenate([m_new] * (width // LANES), axis=1)).astype(BF16)
            acc_scr[hd] = jnp.exp2(m_prev - m_new) * acc_scr[hd] + jnp.dot(
                p, kv_ref[0, ks, group(hd, KVQ_V)], preferred_element_type=F32)
            m_scr[hd] = m_new

        s_next = logits(0)
        for hd in range(FOX_HEADS):
            s_cur = s_next
            if hd + 1 < FOX_HEADS:
                s_next = logits(hd + 1)
            update(hd, s_cur)

    m_scr[...] = jnp.full(m_scr.shape, -jnp.inf, F32)
    acc_scr[...] = jnp.zeros(acc_scr.shape, F32)

    wide = 2 * t
    assert kv_ref.shape[1] % wide == 0

    @pl.loop(0, qi // 2)
    def _(pj):
        block(pl.multiple_of(pj * wide, wide), wide, False)

    @pl.when(qi % 2 == 1)
    def _():
        block(pl.multiple_of((qi - 1) * t, t), wide, True)

    @pl.when(qi % 2 == 0)
    def _():
        block(pl.multiple_of(qi * t, t), t, True)
    for pair in range(FOX_HEADS // heads_per_vreg):
        out = jnp.zeros((t, LANES), F32)
        for hh in range(heads_per_vreg):
            acc = acc_scr[pair * heads_per_vreg + hh]
            ones_lane = (1 - hh) * FOX_HEAD_DIM
            row_sum = acc[:, ones_lane:ones_lane + 1]
            out = out + jnp.where(lane_head == hh, acc, 0.0) / row_sum
        o_ref[0, :, pair * LANES:(pair + 1) * LANES] = out.astype(BF16)


def _fox(kvq):
    b, seq, _ = kvq.shape
    t = min(FOX_T, seq)
    assert seq % t == 0 and KVQ_K == 0 and KVQ_V == FOX_GROUP and KVQ_Q == 2 * FOX_GROUP
    return pl.pallas_call(
        functools.partial(_fox_kernel, t=t),
        grid=(b, seq // t),
        in_specs=[
            pl.BlockSpec((1, t, FOX_GROUP), lambda bi, i: (bi, i, KVQ_Q // FOX_GROUP)),
            pl.BlockSpec((1, seq, 2 * FOX_GROUP), lambda bi, i: (bi, 0, 0)),
        ],
        out_specs=pl.BlockSpec((1, t, FOX_WIDTH), lambda bi, i: (bi, i, 0)),
        out_shape=jax.ShapeDtypeStruct((b, seq, FOX_WIDTH), BF16),
        scratch_shapes=[pltpu.VMEM((FOX_HEADS, t, LANES), F32), pltpu.VMEM((FOX_HEADS, t, LANES), F32)],
        compiler_params=_params("arbitrary", "arbitrary"),
    )(kvq, kvq)


def _lower_ones(n):
    return jnp.tril(jnp.ones((n, n), BF16))


def _spread_forget(f):
    lead = f.shape[:-1]
    order = sorted(range(FOX_HEADS), key=_forget_lane)
    blocks, lane = [], 0
    for hd in order:
        blocks.append(jnp.zeros(lead + (_forget_lane(hd) - lane,), f.dtype))
        blocks.append(jnp.broadcast_to(f[..., hd:hd + 1], lead + (FOX_SLOT,)))
        lane = _forget_lane(hd) + FOX_SLOT
    blocks.append(jnp.zeros(lead + (LANES - lane,), f.dtype))
    return jnp.concatenate(blocks, axis=-1)


def _mix_w_in_extras(w):
    tail_start = RAW_WIDTH // LANES * LANES
    tail = jnp.pad(w[..., tail_start:], ((0, 0), (0, 0), (0, LANES - (RAW_WIDTH - tail_start))))
    dt_cols = w[..., RAW_DT:RAW_QKV]
    pad = jnp.zeros(w.shape[:-1] + (LANES - 3 * SSD_HEADS,), w.dtype)
    small = jnp.concatenate([dt_cols, dt_cols, dt_cols, pad, _spread_forget(w[..., RAW_F:RAW_SC])], axis=-1)
    return tail, small.astype(BF16)


def _lane_vec(pieces):
    row = jnp.zeros((LANES,), F32)
    for off, vec in pieces:
        row = row.at[off:off + vec.shape[0]].set(vec.astype(F32))
    return row.reshape(1, LANES)


def kernel(x, c, ln_in_g, ln_in_b, ada_w, ada_b, ffn1_w_in, ffn1_w_out, mix_w_in, mix_w_out, ssd_conv_w, ssd_conv_b, ssd_dt_bias, ssd_a_log, ssd_d, ssd_norm_g, fox_f_bias, sconv_w, ffn2_w_in, ffn2_w_out, ln_g, ln_b):
    b, seq, d = x.shape
    depth = ada_w.shape[0]
    mod = _ada(c, ada_w, ada_b)
    mod = mod.reshape(depth, mod.shape[1], 3 * N_SUB, d)

    w1_in, w1_out, w2_in, w2_out = ffn1_w_in, ffn1_w_out, ffn2_w_in, ffn2_w_out
    wm_tail, wm_small = _mix_w_in_extras(mix_w_in)
    wm_out = mix_w_out.astype(BF16)

    x2d = x.reshape(b * seq, d)
    for l in range(depth):
        mod_l = mod[l]
        x2d = _ffn(x2d, mod_l, w1_in, w1_out, ln_g[l, 0], ln_b[l, 0], layer=l, sub=0, seq=seq,
                   pre=(ln_in_g, ln_in_b) if l == 0 else None)

        small_bias = _lane_vec([(lane, ssd_dt_bias[l]) for lane in (LANE_DT, LANE_A, LANE_A2)])
        alog_vec = _lane_vec([(LANE_A, ssd_a_log[l]), (LANE_A2, ssd_a_log[l])])
        zx, bc, kvq, y_sc, small = _proj(
            x2d.reshape(b, seq, d), mod_l, mix_w_in, wm_tail, wm_small,
            ssd_conv_w[l], ssd_conv_b[l].reshape(1, SSD_CONV_DIM), small_bias, alog_vec, sconv_w[l],
            _spread_forget(fox_f_bias[l]).reshape(1, LANES), layer=l)
        d_skip_x = jnp.repeat(ssd_d[l], SSD_HEAD_DIM).reshape(1, SSD_WIDTH)
        y_ssd = _ssd(zx, bc, small, d_skip_x, ssd_norm_g[l].reshape(1, SSD_WIDTH))
        y_fox = _fox(kvq)
        mixer_tail = (y_ssd.reshape(b * seq, SSD_WIDTH), y_fox.reshape(b * seq, FOX_WIDTH),
                      y_sc.reshape(b * seq, SCONV_WIDTH), wm_out, ln_g[l, 1], ln_b[l, 1])
        x2d = _ffn(x2d, mod_l, w2_in, w2_out, ln_g[l, 2], ln_b[l, 2], layer=l, sub=2, seq=seq,
                   mix=mixer_tail)
    return x2d.reshape(b, seq, d)
```

```python
import functools
import math

import jax
import jax.numpy as jnp
from jax import lax
from jax.experimental import pallas as pl
from jax.experimental.pallas import tpu as pltpu

F32 = jnp.float32
BF16 = jnp.bfloat16

D_MODEL = 1024
DEPTH = 2
N_SUB = 3
D_FF = 2816
SSD_WIDTH = 512
SSD_HEAD_DIM = 64
SSD_HEADS = 8
SSD_GROUPS = 2
SSD_HPG = SSD_HEADS // SSD_GROUPS
SSD_STATE = 128
SSD_CONV = 4
SSD_BC = SSD_GROUPS * SSD_STATE
SSD_CONV_DIM = SSD_WIDTH + 2 * SSD_BC
GROUP_WIDTH = SSD_WIDTH // SSD_GROUPS
FOX_WIDTH = 256
FOX_HEAD_DIM = 64
FOX_HEADS = 4
SCONV_WIDTH = 256
SCONV_K = 3
ALPHA = (2 * DEPTH) ** 0.25
LN_EPS = 1e-5
RMS_EPS = 1e-5
LOG2_E = math.log2(math.e)

LANES = 128
SUBLANES = 8
VMEM_LIMIT = 56 * 1024 * 1024

RAW_DT = SSD_WIDTH + SSD_CONV_DIM
RAW_QKV = RAW_DT + SSD_HEADS
RAW_F = RAW_QKV + 3 * FOX_WIDTH
RAW_SC = RAW_F + FOX_HEADS
RAW_WIDTH = RAW_SC + 3 * SCONV_WIDTH
COL_Z = 0
COL_XBC = COL_Z + SSD_WIDTH
COL_QKV = COL_XBC + SSD_CONV_DIM
COL_SC = COL_QKV + 3 * FOX_WIDTH
COL_SMALL = COL_SC + 3 * SCONV_WIDTH
COL_F = COL_SMALL + LANES
D_PROJ = COL_F + LANES
LANE_DT = 0
LANE_A = 8
LANE_A2 = 16
FOX_PIECES = 3
FOX_SLOT = 2 * FOX_PIECES
FOX_GROUP = FOX_HEADS * LANES
HEADS_PER_VREG = LANES // FOX_HEAD_DIM
KVQ_K = 0
KVQ_V = FOX_GROUP
KVQ_Q = 2 * FOX_GROUP
KVQ_WIDTH = 3 * FOX_GROUP


def _forget_lane(hd):
    pair, hh = divmod(hd, HEADS_PER_VREG)
    return (HEADS_PER_VREG - 1 - hh) * FOX_HEAD_DIM + pair * FOX_SLOT

FFN_TM = 1024
FFN_TM_MIX = 512
FFN_TF = 256
FFN_HALF = 256
FFN_W_IN_ROWS = 128
FFN_W_OUT_ROWS = 256
PROJ_TM = 512
PROJ_CUM = 128
SSD_T = 256
SSD_ROWS = 2048
SSD_SPREAD_PIECES = 2
FOX_T = 512
ADA_TN = 2304


def _layer_norm(x, g, b):
    mu = jnp.mean(x, axis=-1, keepdims=True)
    xc = x - mu
    var = jnp.mean(xc * xc, axis=-1, keepdims=True)
    return xc * lax.rsqrt(var + LN_EPS) * g + b


def _silu(x):
    return x / (1.0 + jnp.exp2(x * (-LOG2_E)))


def _softplus(x):
    return jnp.maximum(x, 0.0) + jnp.log1p(jnp.exp(-jnp.abs(x)))


def _split_bf16(x, pieces):
    out = []
    for _ in range(pieces):
        part = x.astype(BF16)
        out.append(part)
        x = x - part.astype(F32)
    return out


def _params(*semantics):
    return pltpu.CompilerParams(dimension_semantics=semantics, vmem_limit_bytes=VMEM_LIMIT)


def _ada_kernel(c_ref, w_ref, b_ref, o_ref, *, batch):
    act_hi, act_lo = _split_bf16(_silu(c_ref[...]), 2)
    row = lax.broadcasted_iota(jnp.int32, act_hi.shape, 0)
    lhs = jnp.where(row < batch, act_hi, act_lo)
    w_hi, w_lo = _split_bf16(w_ref[...], 2)
    y = jnp.dot(lhs, w_hi, preferred_element_type=F32) + jnp.dot(lhs, w_lo, preferred_element_type=F32)
    rows = y.shape[0]
    o_ref[...] = y + pltpu.roll(y, rows - batch, axis=0) + b_ref[...]


def _ada(c, ada_w, ada_b):
    depth, d, n = ada_w.shape
    batch = c.shape[0]
    rows = -(-2 * batch // SUBLANES) * SUBLANES
    c_pad = jnp.pad(jnp.concatenate([c, c], axis=0), ((0, rows - 2 * batch), (0, 0)))
    return pl.pallas_call(
        functools.partial(_ada_kernel, batch=batch),
        grid=(depth, n // ADA_TN),
        in_specs=[
            pl.BlockSpec((rows, d), lambda l, j: (0, 0)),
            pl.BlockSpec((None, d, ADA_TN), lambda l, j: (l, 0, j)),
            pl.BlockSpec((None, 1, ADA_TN), lambda l, j: (l, 0, j)),
        ],
        out_specs=pl.BlockSpec((None, rows, ADA_TN), lambda l, j: (l, 0, j)),
        out_shape=jax.ShapeDtypeStruct((depth, rows, n), F32),
        compiler_params=_params("arbitrary", "arbitrary"),
    )(c_pad, ada_w, ada_b.reshape(depth, 1, n))


def _ffn_kernel(*refs, layer, sub, pre_ln, mix):
    x_ref, mod_ref, wi_hbm, wo_hbm, lng_ref, lnb_ref = refs[:6]
    n_scratch = 6
    extra = list(refs[6:-1 - n_scratch])
    o_ref = refs[-1 - n_scratch]
    act_scr, wi_ref, wo_ref, stage_i, stage_o, sem = refs[-n_scratch:]
    if pre_ln:
        ing_ref, inb_ref = extra[:2]
        extra = extra[2:]
    if mix:
        yssd_ref, yfox_ref, ysc_ref, wm_ref, mlng_ref, mlnb_ref = extra
    shift = mod_ref[3 * sub:3 * sub + 1, :]
    scale = mod_ref[3 * sub + 1:3 * sub + 2, :]
    gain = mod_ref[3 * sub + 2:3 * sub + 3, :]
    halves = [slice(r, r + FFN_HALF) for r in range(0, x_ref.shape[0], FFN_HALF)]

    def entry(rows):
        x = x_ref[rows, :]
        if pre_ln:
            x = _layer_norm(x, ing_ref[...], inb_ref[...])
        if mix:
            y_mix = jnp.concatenate([yssd_ref[rows, :], yfox_ref[rows, :], ysc_ref[rows, :]], axis=1)
            y = jnp.dot(y_mix, wm_ref[...], preferred_element_type=F32)
            x = _layer_norm(ALPHA * x + mod_ref[5:6, :] * y, mlng_ref[...], mlnb_ref[...])
        return x

    @pl.when(pl.program_id(0) == 0)
    def _():
        def stream(src_hbm, dst_ref, stage, which, block_rows):
            n_blocks = src_hbm.shape[1] // block_rows
            copy = lambda b: pltpu.make_async_copy(
                src_hbm.at[layer, b * block_rows:(b + 1) * block_rows, :], stage.at[b % 2],
                sem.at[which, b % 2])
            copy(0).start()
            for b in range(n_blocks):
                if b + 1 < n_blocks:
                    copy(b + 1).start()
                copy(b).wait()
                dst_ref[b * block_rows:(b + 1) * block_rows, :] = stage[b % 2].astype(BF16)

        stream(wi_hbm, wi_ref, stage_i, 0, FFN_W_IN_ROWS)
        stream(wo_hbm, wo_ref, stage_o, 1, FFN_W_OUT_ROWS)

    xs = [entry(rows) for rows in halves]
    ys = []
    for rows, x in zip(halves, xs):
        h = (x * (1.0 + scale) + shift).astype(BF16)
        for c in range(D_FF // FFN_TF):
            cols = slice(c * FFN_TF, (c + 1) * FFN_TF)
            gate = jnp.dot(h, wi_ref[:, cols], preferred_element_type=F32)
            up = jnp.dot(h, wi_ref[:, D_FF + c * FFN_TF:D_FF + (c + 1) * FFN_TF],
                         preferred_element_type=F32)
            act_scr[rows, cols] = (_silu(gate) * up).astype(BF16)
        ys.append(jnp.dot(act_scr[rows, :], wo_ref[...], preferred_element_type=F32))
    for rows, x, y in zip(halves, xs, ys):
        o_ref[rows, :] = _layer_norm(ALPHA * x + (0.5 * gain) * y, lng_ref[...], lnb_ref[...])


def _ffn(x2d, mod_l, w_in, w_out, ln_g, ln_b, *, layer, sub, seq, pre=None, mix=None):
    n_tok, d = x2d.shape
    tm = min(FFN_TM if mix is None else FFN_TM_MIX, seq)
    assert seq % tm == 0 and n_tok % seq == 0 and D_FF % FFN_TF == 0 and tm % FFN_HALF == 0
    tiles_per_seq = seq // tm
    row = lambda v: v.reshape(1, d)
    resident = lambda shape: pl.BlockSpec(shape, lambda i: (0, 0), pipeline_mode=pl.Buffered(1))
    weight = lambda shape: pl.BlockSpec((None,) + shape, lambda i: (layer, 0, 0),
                                        pipeline_mode=pl.Buffered(1))
    in_specs = [
        pl.BlockSpec((tm, d), lambda i: (i, 0)),
        pl.BlockSpec((None, 3 * N_SUB, d), lambda i: (i // tiles_per_seq, 0, 0)),
        pl.BlockSpec(memory_space=pl.ANY),
        pl.BlockSpec(memory_space=pl.ANY),
        resident((1, d)),
        resident((1, d)),
    ]
    args = [x2d, mod_l, w_in, w_out, row(ln_g), row(ln_b)]
    if pre is not None:
        in_specs += [resident((1, d))] * 2
        args += [row(pre[0]), row(pre[1])]
    if mix is not None:
        y_ssd, y_fox, y_sc, w_mix, mix_g, mix_b = mix
        tok = lambda width: pl.BlockSpec((tm, width), lambda i: (i, 0))
        in_specs += [tok(SSD_WIDTH), tok(FOX_WIDTH), tok(SCONV_WIDTH), weight((d, d)),
                     resident((1, d)), resident((1, d))]
        args += [y_ssd, y_fox, y_sc, w_mix, row(mix_g), row(mix_b)]
    return pl.pallas_call(
        functools.partial(_ffn_kernel, layer=layer, sub=sub, pre_ln=pre is not None, mix=mix is not None),
        grid=(n_tok // tm,),
        in_specs=in_specs,
        out_specs=pl.BlockSpec((tm, d), lambda i: (i, 0)),
        out_shape=jax.ShapeDtypeStruct((n_tok, d), F32),
        scratch_shapes=[
            pltpu.VMEM((tm, D_FF), BF16),
            pltpu.VMEM((d, 2 * D_FF), BF16),
            pltpu.VMEM((D_FF, d), BF16),
            pltpu.VMEM((2, FFN_W_IN_ROWS, 2 * D_FF), F32),
            pltpu.VMEM((2, FFN_W_OUT_ROWS, d), F32),
            pltpu.SemaphoreType.DMA((2, 2)),
        ],
        compiler_params=_params("arbitrary"),
    )(*args)


def _shift_lanes_left(blocks, shift, n_out):
    lane = lax.broadcasted_iota(jnp.int32, blocks[0].shape, 1)
    rolled = [pltpu.roll(blk, LANES - shift, axis=1) for blk in blocks]
    return [jnp.where(lane < LANES - shift, rolled[k], rolled[k + 1]) for k in range(n_out)]


def _proj_kernel(x_ref, mod_ref, wraw_ref, wtail_ref, wsmall_ref, cw_ref, cb_ref, sbias_ref, alog_ref,
                 scw_ref, fbias_ref, tri_ref,
                 zx_ref, bc_ref, kvq_ref, ysc_ref, small_ref,
                 w_ref, xbc_scr, u_scr, carry_scr, *, tm):
    t = pl.program_id(1)
    halo = SUBLANES

    @pl.when((pl.program_id(0) == 0) & (t == 0))
    def _():
        raw_block = lambda m: wraw_ref[:, m * LANES:(m + 1) * LANES]
        w_ref[:, :COL_QKV] = wraw_ref[:, :COL_QKV].astype(BF16)
        for raw_start, col, width in ((RAW_QKV, COL_QKV, 3 * FOX_WIDTH), (RAW_SC, COL_SC, 3 * SCONV_WIDTH)):
            first, shift = divmod(raw_start, LANES)
            n_out = width // LANES
            blocks = [raw_block(m) if (m + 1) * LANES <= RAW_WIDTH else wtail_ref[...]
                      for m in range(first, first + n_out + 1)]
            for k, blk in enumerate(_shift_lanes_left(blocks, shift, n_out)):
                w_ref[:, col + k * LANES:col + (k + 1) * LANES] = blk.astype(BF16)
        w_ref[:, COL_SMALL:] = wsmall_ref[...]

    @pl.when(t == 0)
    def _():
        xbc_scr[0:halo, :] = jnp.zeros((halo, SSD_CONV_DIM), F32)
        u_scr[0:halo, :] = jnp.zeros((halo, SCONV_WIDTH), F32)
        carry_scr[...] = jnp.zeros_like(carry_scr)

    @pl.when(t > 0)
    def _():
        xbc_scr[0:halo, :] = xbc_scr[tm:tm + halo, :]
        u_scr[0:halo, :] = u_scr[tm:tm + halo, :]

    shift = mod_ref[3:4, :]
    scale = mod_ref[4:5, :]
    h = (x_ref[0] * (1.0 + scale) + shift).astype(BF16)

    def seg(start, width):
        return jnp.dot(h, w_ref[:, start:start + width], preferred_element_type=F32)

    xbc_scr[halo:halo + tm, :] = seg(COL_XBC, SSD_CONV_DIM)
    conv = cb_ref[...] + cw_ref[SSD_CONV - 1:SSD_CONV, :] * xbc_scr[halo:halo + tm, :]
    for k in range(SSD_CONV - 1):
        back = SSD_CONV - 1 - k
        conv = conv + cw_ref[k:k + 1, :] * xbc_scr[halo - back:halo - back + tm, :]
    xbc = _silu(conv)
    zx_ref[0, :, SSD_WIDTH:] = xbc[:, :SSD_WIDTH]
    bc_ref[0] = xbc[:, SSD_WIDTH:].astype(BF16)

    sc = seg(COL_SC, 3 * SCONV_WIDTH)
    u_scr[halo:halo + tm, :] = sc[:, SCONV_WIDTH:2 * SCONV_WIDTH] * sc[:, 2 * SCONV_WIDTH:]
    cu = scw_ref[SCONV_K - 1:SCONV_K, :] * u_scr[halo:halo + tm, :]
    for k in range(SCONV_K - 1):
        back = SCONV_K - 1 - k
        cu = cu + scw_ref[k:k + 1, :] * u_scr[halo - back:halo - back + tm, :]
    ysc_ref[0] = (sc[:, :SCONV_WIDTH] * cu).astype(BF16)

    lane = lax.broadcasted_iota(jnp.int32, (tm, LANES), 1)
    pos = lane % FOX_HEAD_DIM
    used = pos < (FOX_HEADS // HEADS_PER_VREG) * FOX_SLOT
    small_f = seg(COL_SMALL, 2 * LANES)
    log_f = jnp.where(used, -_softplus(-(small_f[:, LANES:] + fbias_ref[...])), 0.0)
    parts = jnp.concatenate(_split_bf16(log_f, 3), axis=1)
    offset = carry_scr[0:1, :]
    blocks = []
    for r in range(0, tm, PROJ_CUM):
        local3 = jnp.dot(tri_ref[...], parts[r:r + PROJ_CUM, :], preferred_element_type=F32)
        local = local3[:, :LANES] + local3[:, LANES:2 * LANES] + local3[:, 2 * LANES:]
        blocks.append(local + offset)
        offset = offset + local[PROJ_CUM - 1:PROJ_CUM, :]
    cum_f = jnp.concatenate(blocks, axis=0)
    carry_scr[...] = jnp.broadcast_to(offset, carry_scr.shape)
    hi, mid, lo = (part.astype(F32) for part in _split_bf16(cum_f * LOG2_E, FOX_PIECES))
    piece = jnp.where(pos % FOX_PIECES == 0, hi, jnp.where(pos % FOX_PIECES == 1, mid, lo))
    in_a = pos % FOX_SLOT < FOX_PIECES
    f_q = jnp.where(in_a, piece, 1.0)
    f_k = jnp.where(in_a, 1.0, -piece)

    qkv = seg(COL_QKV, 3 * FOX_WIDTH)
    q = qkv[:, :FOX_WIDTH] * (FOX_HEAD_DIM ** -0.5 * LOG2_E)
    k = qkv[:, FOX_WIDTH:2 * FOX_WIDTH]
    v = qkv[:, 2 * FOX_WIDTH:]
    for hd in range(FOX_HEADS):
        pair, hh = divmod(hd, HEADS_PER_VREG)
        ps = slice(pair * LANES, (pair + 1) * LANES)
        gs = slice(hd * LANES, (hd + 1) * LANES)
        mine = (lane // FOX_HEAD_DIM) == hh
        first = _forget_lane(hd)
        forget = (lane >= first) & (lane < first + FOX_SLOT)
        ones_col = jnp.where(lane == (1 - hh) * FOX_HEAD_DIM, 1.0, 0.0)
        kvq_ref[0, :, KVQ_Q + hd * LANES:KVQ_Q + (hd + 1) * LANES] = jnp.where(
            mine, q[:, ps], jnp.where(forget, f_q, 0.0)).astype(BF16)
        kvq_ref[0, :, KVQ_K + hd * LANES:KVQ_K + (hd + 1) * LANES] = jnp.where(
            mine, k[:, ps], jnp.where(forget, f_k, 0.0)).astype(BF16)
        kvq_ref[0, :, KVQ_V + hd * LANES:KVQ_V + (hd + 1) * LANES] = jnp.where(
            mine, v[:, ps], ones_col).astype(BF16)

    dt = _softplus(small_f[:, :LANES] + sbias_ref[...])
    a_neg = -jnp.exp(alog_ref[...])
    small_ref[0] = jnp.where(lane < LANE_A, dt, dt * a_neg)

    zx_ref[0, :, :SSD_WIDTH] = seg(COL_Z, SSD_WIDTH)


def _proj(x, mod_l, w_raw, w_tail, w_small, conv_w, conv_b, small_bias, alog_vec, sconv_w, f_bias, *, layer):
    b, seq, d = x.shape
    tm = min(PROJ_TM, seq)
    assert seq % tm == 0
    nt = seq // tm
    tok = lambda width: pl.BlockSpec((1, tm, width), lambda bi, t: (bi, t, 0))
    const2 = lambda shape: pl.BlockSpec(shape, lambda bi, t: (0, 0))
    out_shapes = (
        jax.ShapeDtypeStruct((b, seq, 2 * SSD_WIDTH), F32),
        jax.ShapeDtypeStruct((b, seq, 2 * SSD_BC), BF16),
        jax.ShapeDtypeStruct((b, seq, KVQ_WIDTH), BF16),
        jax.ShapeDtypeStruct((b, seq, SCONV_WIDTH), BF16),
        jax.ShapeDtypeStruct((b, seq, LANES), F32),
    )
    out_specs = (tok(2 * SSD_WIDTH), tok(2 * SSD_BC), tok(KVQ_WIDTH), tok(SCONV_WIDTH), tok(LANES))
    return pl.pallas_call(
        functools.partial(_proj_kernel, tm=tm),
        grid=(b, nt),
        in_specs=[
            tok(d),
            pl.BlockSpec((None, 3 * N_SUB, d), lambda bi, t: (bi, 0, 0)),
            pl.BlockSpec((None, d, RAW_WIDTH), lambda bi, t: (layer, 0, 0), pipeline_mode=pl.Buffered(1)),
            pl.BlockSpec((None, d, LANES), lambda bi, t: (layer, 0, 0), pipeline_mode=pl.Buffered(1)),
            pl.BlockSpec((None, d, 2 * LANES), lambda bi, t: (layer, 0, 0), pipeline_mode=pl.Buffered(1)),
            const2((SSD_CONV, SSD_CONV_DIM)),
            const2((1, SSD_CONV_DIM)),
            const2((1, LANES)),
            const2((1, LANES)),
            const2((SCONV_K, SCONV_WIDTH)),
            const2((1, LANES)),
            const2((PROJ_CUM, PROJ_CUM)),
        ],
        out_specs=out_specs,
        out_shape=out_shapes,
        scratch_shapes=[
            pltpu.VMEM((d, D_PROJ), BF16),
            pltpu.VMEM((tm + SUBLANES, SSD_CONV_DIM), F32),
            pltpu.VMEM((tm + SUBLANES, SCONV_WIDTH), F32),
            pltpu.VMEM((SUBLANES, LANES), F32),
        ],
        compiler_params=_params("arbitrary", "arbitrary"),
    )(x, mod_l, w_raw, w_tail, w_small, conv_w, conv_b, small_bias, alog_vec, sconv_w, f_bias,
      _lower_ones(PROJ_CUM))


def _ssd_kernel(zx_ref, bc_ref, small_ref, dskip_ref, ng_ref, tri_ref, spread_ref,
                o_ref, state_scr, *, t, chunks):
    @pl.when(pl.program_id(1) == 0)
    def _():
        state_scr[...] = jnp.zeros_like(state_scr)

    lane = lax.broadcasted_iota(jnp.int32, (t, LANES), 1)
    row_i = lax.broadcasted_iota(jnp.int32, (t, t), 0)
    col_i = lax.broadcasted_iota(jnp.int32, (t, t), 1)
    lower = row_i >= col_i
    head_of_lane = lax.broadcasted_iota(jnp.int32, (t, GROUP_WIDTH), 1) // SSD_HEAD_DIM

    for ci in range(chunks):
        rows = slice(ci * t, (ci + 1) * t)
        small = small_ref[0, rows, :]
        a_only = jnp.where((lane >= LANE_A) & (lane < LANE_A2 + SSD_HEADS), small, 0.0)
        cs3 = jnp.dot(tri_ref[...], jnp.concatenate(_split_bf16(a_only, 3), axis=1),
                      preferred_element_type=F32)
        cs = cs3[:, :LANES] + cs3[:, LANES:2 * LANES] + cs3[:, 2 * LANES:]
        cs_t = cs.T
        total = cs[t - 1:t, :]
        per_head = jnp.where(lane < LANE_A, small,
                             jnp.where(lane < LANE_A2, jnp.exp(cs), jnp.exp(total - cs)))
        wide = jnp.dot(jnp.concatenate(_split_bf16(per_head, SSD_SPREAD_PIECES), axis=1), spread_ref[...],
                       preferred_element_type=F32)
        dt_x = wide[:, :SSD_WIDTH]
        decay_in = wide[:, SSD_WIDTH:2 * SSD_WIDTH]
        decay_out = wide[:, 2 * SSD_WIDTH:]
        decay_chunk = decay_in[t - 1:t, :]

        xs = zx_ref[0, rows, SSD_WIDTH:]
        xdt = xs * dt_x
        ys = []
        for g in range(SSD_GROUPS):
            gs = slice(g * GROUP_WIDTH, (g + 1) * GROUP_WIDTH)
            ns = slice(g * SSD_STATE, (g + 1) * SSD_STATE)
            bg = bc_ref[0, rows, ns]
            cg = bc_ref[0, rows, slice(SSD_BC + ns.start, SSD_BC + ns.stop)]
            cb = lax.dot_general(cg, bg, (((1,), (1,)), ((), ())), preferred_element_type=F32)
            xg = xdt[:, gs]
            m_parts, x_parts = [], []
            for e in range(SSD_HPG):
                hd = g * SSD_HPG + e
                col = cs[:, LANE_A + hd:LANE_A + hd + 1]
                row = cs_t[LANE_A + hd:LANE_A + hd + 1, :]
                l_mat = jnp.exp(jnp.where(lower, col - row, -jnp.inf))
                m_parts.append((cb * l_mat).astype(BF16))
                x_parts.append(jnp.where(head_of_lane == e, xg, 0.0).astype(BF16))
            m_cat = jnp.concatenate(m_parts, axis=1)
            x_blk = jnp.concatenate(x_parts, axis=0)
            y_diag = jnp.dot(m_cat, x_blk, preferred_element_type=F32)

            state = state_scr[g]
            y_off = jnp.dot(cg, state.astype(BF16), preferred_element_type=F32) * decay_in[:, gs]
            xd = (xg * decay_out[:, gs]).astype(BF16)
            upd = lax.dot_general(bg, xd, (((0,), (0,)), ((), ())), preferred_element_type=F32)
            state_scr[g] = state * decay_chunk[:, gs] + upd
            ys.append(y_diag + y_off + dskip_ref[:, gs] * xs[:, gs])

        z = zx_ref[0, rows, :SSD_WIDTH]
        for g in range(SSD_GROUPS):
            gs = slice(g * GROUP_WIDTH, (g + 1) * GROUP_WIDTH)
            yg = ys[g] * _silu(z[:, gs])
            yg = yg * lax.rsqrt(jnp.mean(yg * yg, axis=-1, keepdims=True) + RMS_EPS)
            o_ref[0, rows, gs] = (yg * ng_ref[:, gs]).astype(BF16)


def _head_spread():
    src = jnp.arange(SSD_SPREAD_PIECES * LANES)[:, None] % LANES
    col = jnp.arange(3 * SSD_WIDTH)[None, :]
    first_lane = jnp.array([LANE_DT, LANE_A, LANE_A2])[col // SSD_WIDTH]
    return (src == first_lane + (col % SSD_WIDTH) // SSD_HEAD_DIM).astype(BF16)


def _ssd(zx, bc, small, d_skip_x, norm_g):
    b, seq, _ = zx.shape
    t = min(SSD_T, seq)
    rows = min(SSD_ROWS, seq)
    assert seq % rows == 0 and rows % t == 0
    tok = lambda width: pl.BlockSpec((1, rows, width), lambda bi, c: (bi, c, 0))
    const2 = lambda shape: pl.BlockSpec(shape, lambda bi, c: (0, 0))
    return pl.pallas_call(
        functools.partial(_ssd_kernel, t=t, chunks=rows // t),
        grid=(b, seq // rows),
        in_specs=[tok(2 * SSD_WIDTH), tok(2 * SSD_BC), tok(LANES),
                  const2((1, SSD_WIDTH)), const2((1, SSD_WIDTH)),
                  const2((t, t)), const2((SSD_SPREAD_PIECES * LANES, 3 * SSD_WIDTH))],
        out_specs=tok(SSD_WIDTH),
        out_shape=jax.ShapeDtypeStruct((b, seq, SSD_WIDTH), BF16),
        scratch_shapes=[pltpu.VMEM((SSD_GROUPS, SSD_STATE, GROUP_WIDTH), F32)],
        compiler_params=_params("arbitrary", "arbitrary"),
    )(zx, bc, small, d_skip_x, norm_g, _lower_ones(t), _head_spread())


def _fox_kernel(q_ref, kv_ref, o_ref, m_scr, acc_scr, *, t):
    qi = pl.program_id(1)
    lane_head = lax.broadcasted_iota(jnp.int32, (t, LANES), 1) // FOX_HEAD_DIM
    heads_per_vreg = LANES // FOX_HEAD_DIM

    def block(start, width, masked):
        ks = pl.ds(start, width)
        group = lambda hd, base=0: slice(base + hd * LANES, base + (hd + 1) * LANES)
        if masked:
            row_i = lax.broadcasted_iota(jnp.int32, (t, width), 0)
            col_i = lax.broadcasted_iota(jnp.int32, (t, width), 1)
            visible = row_i + (width - t) >= col_i

        def logits(hd):
            s = lax.dot_general(q_ref[0, :, group(hd)], kv_ref[0, ks, group(hd, KVQ_K)],
                                (((1,), (1,)), ((), ())), preferred_element_type=F32)
            return jnp.where(visible, s, -jnp.inf) if masked else s

        def update(hd, s):
            m_prev = m_scr[hd]
            m_new = jnp.maximum(m_prev, jnp.max(s, axis=-1, keepdims=True))
            p = jnp.exp2(s - jnp.concatenate([m_new] * (width // LANES), axis=1)).astype(BF16)
            acc_scr[hd] = jnp.exp2(m_prev - m_new) * acc_scr[hd] + jnp.dot(
                p, kv_ref[0, ks, group(hd, KVQ_V)], preferred_element_type=F32)
            m_scr[hd] = m_new

        s_next = logits(0)
        for hd in range(FOX_HEADS):
            s_cur = s_next
            if hd + 1 < FOX_HEADS:
                s_next = logits(hd + 1)
            update(hd, s_cur)

    m_scr[...] = jnp.full(m_scr.shape, -jnp.inf, F32)
    acc_scr[...] = jnp.zeros(acc_scr.shape, F32)

    wide = 2 * t
    assert kv_ref.shape[1] % wide == 0

    @pl.loop(0, qi // 2)
    def _(pj):
        block(pl.multiple_of(pj * wide, wide), wide, False)

    @pl.when(qi % 2 == 1)
    def _():
        block(pl.multiple_of((qi - 1) * t, t), wide, True)

    @pl.when(qi % 2 == 0)
    def _():
        block(pl.multiple_of(qi * t, t), t, True)
    for pair in range(FOX_HEADS // heads_per_vreg):
        out = jnp.zeros((t, LANES), F32)
        for hh in range(heads_per_vreg):
            acc = acc_scr[pair * heads_per_vreg + hh]
            ones_lane = (1 - hh) * FOX_HEAD_DIM
            row_sum = acc[:, ones_lane:ones_lane + 1]
            out = out + jnp.where(lane_head == hh, acc, 0.0) / row_sum
        o_ref[0, :, pair * LANES:(pair + 1) * LANES] = out.astype(BF16)


def _fox(kvq):
    b, seq, _ = kvq.shape
    t = min(FOX_T, seq)
    assert seq % t == 0 and KVQ_K == 0 and KVQ_V == FOX_GROUP and KVQ_Q == 2 * FOX_GROUP
    return pl.pallas_call(
        functools.partial(_fox_kernel, t=t),
        grid=(b, seq // t),
        in_specs=[
            pl.BlockSpec((1, t, FOX_GROUP), lambda bi, i: (bi, i, KVQ_Q // FOX_GROUP)),
            pl.BlockSpec((1, seq, 2 * FOX_GROUP), lambda bi, i: (bi, 0, 0)),
        ],
        out_specs=pl.BlockSpec((1, t, FOX_WIDTH), lambda bi, i: (bi, i, 0)),
        out_shape=jax.ShapeDtypeStruct((b, seq, FOX_WIDTH), BF16),
        scratch_shapes=[pltpu.VMEM((FOX_HEADS, t, LANES), F32), pltpu.VMEM((FOX_HEADS, t, LANES), F32)],
        compiler_params=_params("arbitrary", "arbitrary"),
    )(kvq, kvq)


def _lower_ones(n):
    return jnp.tril(jnp.ones((n, n), BF16))


def _spread_forget(f):
    lead = f.shape[:-1]
    order = sorted(range(FOX_HEADS), key=_forget_lane)
    blocks, lane = [], 0
    for hd in order:
        blocks.append(jnp.zeros(lead + (_forget_lane(hd) - lane,), f.dtype))
        blocks.append(jnp.broadcast_to(f[..., hd:hd + 1], lead + (FOX_SLOT,)))
        lane = _forget_lane(hd) + FOX_SLOT
    blocks.append(jnp.zeros(lead + (LANES - lane,), f.dtype))
    return jnp.concatenate(blocks, axis=-1)


def _mix_w_in_extras(w):
    tail_start = RAW_WIDTH // LANES * LANES
    tail = jnp.pad(w[..., tail_start:], ((0, 0), (0, 0), (0, LANES - (RAW_WIDTH - tail_start))))
    dt_cols = w[..., RAW_DT:RAW_QKV]
    pad = jnp.zeros(w.shape[:-1] + (LANES - 3 * SSD_HEADS,), w.dtype)
    small = jnp.concatenate([dt_cols, dt_cols, dt_cols, pad, _spread_forget(w[..., RAW_F:RAW_SC])], axis=-1)
    return tail, small.astype(BF16)


def _lane_vec(pieces):
    row = jnp.zeros((LANES,), F32)
    for off, vec in pieces:
        row = row.at[off:off + vec.shape[0]].set(vec.astype(F32))
    return row.reshape(1, LANES)


def kernel(x, c, ln_in_g, ln_in_b, ada_w, ada_b, ffn1_w_in, ffn1_w_out, mix_w_in, mix_w_out, ssd_conv_w, ssd_conv_b, ssd_dt_bias, ssd_a_log, ssd_d, ssd_norm_g, fox_f_bias, sconv_w, ffn2_w_in, ffn2_w_out, ln_g, ln_b):
    b, seq, d = x.shape
    depth = ada_w.shape[0]
    mod = _ada(c, ada_w, ada_b)
    mod = mod.reshape(depth, mod.shape[1], 3 * N_SUB, d)

    w1_in, w1_out, w2_in, w2_out = ffn1_w_in, ffn1_w_out, ffn2_w_in, ffn2_w_out
    wm_tail, wm_small = _mix_w_in_extras(mix_w_in)
    wm_out = mix_w_out.astype(BF16)

    x2d = x.reshape(b * seq, d)
    for l in range(depth):
        mod_l = mod[l]
        x2d = _ffn(x2d, mod_l, w1_in, w1_out, ln_g[l, 0], ln_b[l, 0], layer=l, sub=0, seq=seq,
                   pre=(ln_in_g, ln_in_b) if l == 0 else None)

        small_bias = _lane_vec([(lane, ssd_dt_bias[l]) for lane in (LANE_DT, LANE_A, LANE_A2)])
        alog_vec = _lane_vec([(LANE_A, ssd_a_log[l]), (LANE_A2, ssd_a_log[l])])
        zx, bc, kvq, y_sc, small = _proj(
            x2d.reshape(b, seq, d), mod_l, mix_w_in, wm_tail, wm_small,
            ssd_conv_w[l], ssd_conv_b[l].reshape(1, SSD_CONV_DIM), small_bias, alog_vec, sconv_w[l],
            _spread_forget(fox_f_bias[l]).reshape(1, LANES), layer=l)
        d_skip_x = jnp.repeat(ssd_d[l], SSD_HEAD_DIM).reshape(1, SSD_WIDTH)
        y_ssd = _ssd(zx, bc, small, d_skip_x, ssd_norm_g[l].reshape(1, SSD_WIDTH))
        y_fox = _fox(kvq)
        mixer_tail = (y_ssd.reshape(b * seq, SSD_WIDTH), y_fox.reshape(b * seq, FOX_WIDTH),
                      y_sc.reshape(b * seq, SCONV_WIDTH), wm_out, ln_g[l, 1], ln_b[l, 1])
        x2d = _ffn(x2d, mod_l, w2_in, w2_out, ln_g[l, 2], ln_b[l, 2], layer=l, sub=2, seq=seq,
                   mix=mixer_tail)
    return x2d.reshape(b, seq, d)
```

```python
import functools
import math

import jax
import jax.numpy as jnp
from jax import lax
from jax.experimental import pallas as pl
from jax.experimental.pallas import tpu as pltpu

F32 = jnp.float32
BF16 = jnp.bfloat16

D_MODEL = 1024
DEPTH = 2
N_SUB = 3
D_FF = 2816
SSD_WIDTH = 512
SSD_HEAD_DIM = 64
SSD_HEADS = 8
SSD_GROUPS = 2
SSD_HPG = SSD_HEADS // SSD_GROUPS
SSD_STATE = 128
SSD_CONV = 4
SSD_BC = SSD_GROUPS * SSD_STATE
SSD_CONV_DIM = SSD_WIDTH + 2 * SSD_BC
GROUP_WIDTH = SSD_WIDTH // SSD_GROUPS
FOX_WIDTH = 256
FOX_HEAD_DIM = 64
FOX_HEADS = 4
SCONV_WIDTH = 256
SCONV_K = 3
ALPHA = (2 * DEPTH) ** 0.25
LN_EPS = 1e-5
RMS_EPS = 1e-5
LOG2_E = math.log2(math.e)

LANES = 128
SUBLANES = 8
VMEM_LIMIT = 56 * 1024 * 1024

RAW_DT = SSD_WIDTH + SSD_CONV_DIM
RAW_QKV = RAW_DT + SSD_HEADS
RAW_F = RAW_QKV + 3 * FOX_WIDTH
RAW_SC = RAW_F + FOX_HEADS
RAW_WIDTH = RAW_SC + 3 * SCONV_WIDTH
COL_Z = 0
COL_XBC = COL_Z + SSD_WIDTH
COL_QKV = COL_XBC + SSD_CONV_DIM
COL_SC = COL_QKV + 3 * FOX_WIDTH
COL_SMALL = COL_SC + 3 * SCONV_WIDTH
COL_F = COL_SMALL + LANES
D_PROJ = COL_F + LANES
LANE_DT = 0
LANE_A = 8
LANE_A2 = 16
FOX_PIECES = 3
FOX_SLOT = 2 * FOX_PIECES
FOX_GROUP = FOX_HEADS * LANES
HEADS_PER_VREG = LANES // FOX_HEAD_DIM
KVQ_K = 0
KVQ_V = FOX_GROUP
KVQ_Q = 2 * FOX_GROUP
KVQ_WIDTH = 3 * FOX_GROUP


def _forget_lane(hd):
    pair, hh = divmod(hd, HEADS_PER_VREG)
    return (HEADS_PER_VREG - 1 - hh) * FOX_HEAD_DIM + pair * FOX_SLOT

FFN_TM = 1024
FFN_TM_MIX = 512
FFN_TF = 256
FFN_HALF = 256
FFN_W_IN_ROWS = 128
FFN_W_OUT_ROWS = 704
PROJ_TM = 512
PROJ_CUM = 128
SSD_T = 256
SSD_ROWS = 2048
SSD_SPREAD_PIECES = 2
FOX_T = 512
ADA_TN = 2304


def _layer_norm(x, g, b):
    mu = jnp.mean(x, axis=-1, keepdims=True)
    xc = x - mu
    var = jnp.mean(xc * xc, axis=-1, keepdims=True)
    return xc * lax.rsqrt(var + LN_EPS) * g + b


def _silu(x):
    return x / (1.0 + jnp.exp2(x * (-LOG2_E)))


def _softplus(x):
    return jnp.maximum(x, 0.0) + jnp.log1p(jnp.exp(-jnp.abs(x)))


def _split_bf16(x, pieces):
    out = []
    for _ in range(pieces):
        part = x.astype(BF16)
        out.append(part)
        x = x - part.astype(F32)
    return out


def _params(*semantics):
    return pltpu.CompilerParams(dimension_semantics=semantics, vmem_limit_bytes=VMEM_LIMIT)


def _ada_kernel(c_ref, w_ref, b_ref, o_ref, *, batch):
    act_hi, act_lo = _split_bf16(_silu(c_ref[...]), 2)
    row = lax.broadcasted_iota(jnp.int32, act_hi.shape, 0)
    lhs = jnp.where(row < batch, act_hi, act_lo)
    w_hi, w_lo = _split_bf16(w_ref[...], 2)
    y = jnp.dot(lhs, w_hi, preferred_element_type=F32) + jnp.dot(lhs, w_lo, preferred_element_type=F32)
    rows = y.shape[0]
    o_ref[...] = y + pltpu.roll(y, rows - batch, axis=0) + b_ref[...]


def _ada(c, ada_w, ada_b):
    depth, d, n = ada_w.shape
    batch = c.shape[0]
    rows = -(-2 * batch // SUBLANES) * SUBLANES
    c_pad = jnp.pad(jnp.concatenate([c, c], axis=0), ((0, rows - 2 * batch), (0, 0)))
    return pl.pallas_call(
        functools.partial(_ada_kernel, batch=batch),
        grid=(depth, n // ADA_TN),
        in_specs=[
            pl.BlockSpec((rows, d), lambda l, j: (0, 0)),
            pl.BlockSpec((None, d, ADA_TN), lambda l, j: (l, 0, j)),
            pl.BlockSpec((None, 1, ADA_TN), lambda l, j: (l, 0, j)),
        ],
        out_specs=pl.BlockSpec((None, rows, ADA_TN), lambda l, j: (l, 0, j)),
        out_shape=jax.ShapeDtypeStruct((depth, rows, n), F32),
        compiler_params=_params("arbitrary", "arbitrary"),
    )(c_pad, ada_w, ada_b.reshape(depth, 1, n))


def _ffn_kernel(*refs, layer, sub, pre_ln, mix):
    x_ref, mod_ref, wi_hbm, wo_hbm, lng_ref, lnb_ref = refs[:6]
    n_scratch = 6
    extra = list(refs[6:-1 - n_scratch])
    o_ref = refs[-1 - n_scratch]
    act_scr, wi_ref, wo_ref, stage_i, stage_o, sem = refs[-n_scratch:]
    if pre_ln:
        ing_ref, inb_ref = extra[:2]
        extra = extra[2:]
    if mix:
        yssd_ref, yfox_ref, ysc_ref, wm_ref, mlng_ref, mlnb_ref = extra
    shift = mod_ref[3 * sub:3 * sub + 1, :]
    scale = mod_ref[3 * sub + 1:3 * sub + 2, :]
    gain = mod_ref[3 * sub + 2:3 * sub + 3, :]
    halves = [slice(r, r + FFN_HALF) for r in range(0, x_ref.shape[0], FFN_HALF)]

    def entry(rows):
        x = x_ref[rows, :]
        if pre_ln:
            x = _layer_norm(x, ing_ref[...], inb_ref[...])
        if mix:
            y_mix = jnp.concatenate([yssd_ref[rows, :], yfox_ref[rows, :], ysc_ref[rows, :]], axis=1)
            y = jnp.dot(y_mix, wm_ref[...], preferred_element_type=F32)
            x = _layer_norm(ALPHA * x + mod_ref[5:6, :] * y, mlng_ref[...], mlnb_ref[...])
        return x

    @pl.when(pl.program_id(0) == 0)
    def _():
        def stream(src_hbm, dst_ref, stage, which, block_rows):
            n_blocks = src_hbm.shape[1] // block_rows
            copy = lambda b: pltpu.make_async_copy(
                src_hbm.at[layer, b * block_rows:(b + 1) * block_rows, :], stage.at[b % 2],
                sem.at[which, b % 2])
            copy(0).start()
            for b in range(n_blocks):
                if b + 1 < n_blocks:
                    copy(b + 1).start()
                copy(b).wait()
                dst_ref[b * block_rows:(b + 1) * block_rows, :] = stage[b % 2].astype(BF16)

        stream(wi_hbm, wi_ref, stage_i, 0, FFN_W_IN_ROWS)
        stream(wo_hbm, wo_ref, stage_o, 1, FFN_W_OUT_ROWS)

    xs = [entry(rows) for rows in halves]
    ys = []
    for rows, x in zip(halves, xs):
        h = (x * (1.0 + scale) + shift).astype(BF16)
        for c in range(D_FF // FFN_TF):
            cols = slice(c * FFN_TF, (c + 1) * FFN_TF)
            gate = jnp.dot(h, wi_ref[:, cols], preferred_element_type=F32)
            up = jnp.dot(h, wi_ref[:, D_FF + c * FFN_TF:D_FF + (c + 1) * FFN_TF],
                         preferred_element_type=F32)
            act_scr[rows, cols] = (_silu(gate) * up).astype(BF16)
        ys.append(jnp.dot(act_scr[rows, :], wo_ref[...], preferred_element_type=F32))
    for rows, x, y in zip(halves, xs, ys):
        o_ref[rows, :] = _layer_norm(ALPHA * x + (0.5 * gain) * y, lng_ref[...], lnb_ref[...])


def _ffn(x2d, mod_l, w_in, w_out, ln_g, ln_b, *, layer, sub, seq, pre=None, mix=None):
    n_tok, d = x2d.shape
    tm = min(FFN_TM if mix is None else FFN_TM_MIX, seq)
    assert seq % tm == 0 and n_tok % seq == 0 and D_FF % FFN_TF == 0 and tm % FFN_HALF == 0
    tiles_per_seq = seq // tm
    row = lambda v: v.reshape(1, d)
    resident = lambda shape: pl.BlockSpec(shape, lambda i: (0, 0), pipeline_mode=pl.Buffered(1))
    weight = lambda shape: pl.BlockSpec((None,) + shape, lambda i: (layer, 0, 0),
                                        pipeline_mode=pl.Buffered(1))
    in_specs = [
        pl.BlockSpec((tm, d), lambda i: (i, 0)),
        pl.BlockSpec((None, 3 * N_SUB, d), lambda i: (i // tiles_per_seq, 0, 0)),
        pl.BlockSpec(memory_space=pl.ANY),
        pl.BlockSpec(memory_space=pl.ANY),
        resident((1, d)),
        resident((1, d)),
    ]
    args = [x2d, mod_l, w_in, w_out, row(ln_g), row(ln_b)]
    if pre is not None:
        in_specs += [resident((1, d))] * 2
        args += [row(pre[0]), row(pre[1])]
    if mix is not None:
        y_ssd, y_fox, y_sc, w_mix, mix_g, mix_b = mix
        tok = lambda width: pl.BlockSpec((tm, width), lambda i: (i, 0))
        in_specs += [tok(SSD_WIDTH), tok(FOX_WIDTH), tok(SCONV_WIDTH), weight((d, d)),
                     resident((1, d)), resident((1, d))]
        args += [y_ssd, y_fox, y_sc, w_mix, row(mix_g), row(mix_b)]
    return pl.pallas_call(
        functools.partial(_ffn_kernel, layer=layer, sub=sub, pre_ln=pre is not None, mix=mix is not None),
        grid=(n_tok // tm,),
        in_specs=in_specs,
        out_specs=pl.BlockSpec((tm, d), lambda i: (i, 0)),
        out_shape=jax.ShapeDtypeStruct((n_tok, d), F32),
        scratch_shapes=[
            pltpu.VMEM((tm, D_FF), BF16),
            pltpu.VMEM((d, 2 * D_FF), BF16),
            pltpu.VMEM((D_FF, d), BF16),
            pltpu.VMEM((2, FFN_W_IN_ROWS, 2 * D_FF), F32),
            pltpu.VMEM((2, FFN_W_OUT_ROWS, d), F32),
            pltpu.SemaphoreType.DMA((2, 2)),
        ],
        compiler_params=_params("arbitrary"),
    )(*args)


def _shift_lanes_left(blocks, shift, n_out):
    lane = lax.broadcasted_iota(jnp.int32, blocks[0].shape, 1)
    rolled = [pltpu.roll(blk, LANES - shift, axis=1) for blk in blocks]
    return [jnp.where(lane < LANES - shift, rolled[k], rolled[k + 1]) for k in range(n_out)]


def _proj_kernel(x_ref, mod_ref, wraw_ref, wtail_ref, wsmall_ref, cw_ref, cb_ref, sbias_ref, alog_ref,
                 scw_ref, fbias_ref, tri_ref,
                 zx_ref, bc_ref, kvq_ref, ysc_ref, small_ref,
                 w_ref, xbc_scr, u_scr, carry_scr, *, tm):
    t = pl.program_id(1)
    halo = SUBLANES

    @pl.when((pl.program_id(0) == 0) & (t == 0))
    def _():
        raw_block = lambda m: wraw_ref[:, m * LANES:(m + 1) * LANES]
        w_ref[:, :COL_QKV] = wraw_ref[:, :COL_QKV].astype(BF16)
        for raw_start, col, width in ((RAW_QKV, COL_QKV, 3 * FOX_WIDTH), (RAW_SC, COL_SC, 3 * SCONV_WIDTH)):
            first, shift = divmod(raw_start, LANES)
            n_out = width // LANES
            blocks = [raw_block(m) if (m + 1) * LANES <= RAW_WIDTH else wtail_ref[...]
                      for m in range(first, first + n_out + 1)]
            for k, blk in enumerate(_shift_lanes_left(blocks, shift, n_out)):
                w_ref[:, col + k * LANES:col + (k + 1) * LANES] = blk.astype(BF16)
        w_ref[:, COL_SMALL:] = wsmall_ref[...]

    @pl.when(t == 0)
    def _():
        xbc_scr[0:halo, :] = jnp.zeros((halo, SSD_CONV_DIM), F32)
        u_scr[0:halo, :] = jnp.zeros((halo, SCONV_WIDTH), F32)
        carry_scr[...] = jnp.zeros_like(carry_scr)

    @pl.when(t > 0)
    def _():
        xbc_scr[0:halo, :] = xbc_scr[tm:tm + halo, :]
        u_scr[0:halo, :] = u_scr[tm:tm + halo, :]

    shift = mod_ref[3:4, :]
    scale = mod_ref[4:5, :]
    h = (x_ref[0] * (1.0 + scale) + shift).astype(BF16)

    def seg(start, width):
        return jnp.dot(h, w_ref[:, start:start + width], preferred_element_type=F32)

    xbc_scr[halo:halo + tm, :] = seg(COL_XBC, SSD_CONV_DIM)
    conv = cb_ref[...] + cw_ref[SSD_CONV - 1:SSD_CONV, :] * xbc_scr[halo:halo + tm, :]
    for k in range(SSD_CONV - 1):
        back = SSD_CONV - 1 - k
        conv = conv + cw_ref[k:k + 1, :] * xbc_scr[halo - back:halo - back + tm, :]
    xbc = _silu(conv)
    zx_ref[0, :, SSD_WIDTH:] = xbc[:, :SSD_WIDTH]
    bc_ref[0] = xbc[:, SSD_WIDTH:].astype(BF16)

    sc = seg(COL_SC, 3 * SCONV_WIDTH)
    u_scr[halo:halo + tm, :] = sc[:, SCONV_WIDTH:2 * SCONV_WIDTH] * sc[:, 2 * SCONV_WIDTH:]
    cu = scw_ref[SCONV_K - 1:SCONV_K, :] * u_scr[halo:halo + tm, :]
    for k in range(SCONV_K - 1):
        back = SCONV_K - 1 - k
        cu = cu + scw_ref[k:k + 1, :] * u_scr[halo - back:halo - back + tm, :]
    ysc_ref[0] = (sc[:, :SCONV_WIDTH] * cu).astype(BF16)

    lane = lax.broadcasted_iota(jnp.int32, (tm, LANES), 1)
    pos = lane % FOX_HEAD_DIM
    used = pos < (FOX_HEADS // HEADS_PER_VREG) * FOX_SLOT
    small_f = seg(COL_SMALL, 2 * LANES)
    log_f = jnp.where(used, -_softplus(-(small_f[:, LANES:] + fbias_ref[...])), 0.0)
    parts = jnp.concatenate(_split_bf16(log_f, 3), axis=1)
    offset = carry_scr[0:1, :]
    blocks = []
    for r in range(0, tm, PROJ_CUM):
        local3 = jnp.dot(tri_ref[...], parts[r:r + PROJ_CUM, :], preferred_element_type=F32)
        local = local3[:, :LANES] + local3[:, LANES:2 * LANES] + local3[:, 2 * LANES:]
        blocks.append(local + offset)
        offset = offset + local[PROJ_CUM - 1:PROJ_CUM, :]
    cum_f = jnp.concatenate(blocks, axis=0)
    carry_scr[...] = jnp.broadcast_to(offset, carry_scr.shape)
    hi, mid, lo = (part.astype(F32) for part in _split_bf16(cum_f * LOG2_E, FOX_PIECES))
    piece = jnp.where(pos % FOX_PIECES == 0, hi, jnp.where(pos % FOX_PIECES == 1, mid, lo))
    in_a = pos % FOX_SLOT < FOX_PIECES
    f_q = jnp.where(in_a, piece, 1.0)
    f_k = jnp.where(in_a, 1.0, -piece)

    qkv = seg(COL_QKV, 3 * FOX_WIDTH)
    q = qkv[:, :FOX_WIDTH] * (FOX_HEAD_DIM ** -0.5 * LOG2_E)
    k = qkv[:, FOX_WIDTH:2 * FOX_WIDTH]
    v = qkv[:, 2 * FOX_WIDTH:]
    for hd in range(FOX_HEADS):
        pair, hh = divmod(hd, HEADS_PER_VREG)
        ps = slice(pair * LANES, (pair + 1) * LANES)
        gs = slice(hd * LANES, (hd + 1) * LANES)
        mine = (lane // FOX_HEAD_DIM) == hh
        first = _forget_lane(hd)
        forget = (lane >= first) & (lane < first + FOX_SLOT)
        ones_col = jnp.where(lane == (1 - hh) * FOX_HEAD_DIM, 1.0, 0.0)
        kvq_ref[0, :, KVQ_Q + hd * LANES:KVQ_Q + (hd + 1) * LANES] = jnp.where(
            mine, q[:, ps], jnp.where(forget, f_q, 0.0)).astype(BF16)
        kvq_ref[0, :, KVQ_K + hd * LANES:KVQ_K + (hd + 1) * LANES] = jnp.where(
            mine, k[:, ps], jnp.where(forget, f_k, 0.0)).astype(BF16)
        kvq_ref[0, :, KVQ_V + hd * LANES:KVQ_V + (hd + 1) * LANES] = jnp.where(
            mine, v[:, ps], ones_col).astype(BF16)

    dt = _softplus(small_f[:, :LANES] + sbias_ref[...])
    a_neg = -jnp.exp(alog_ref[...])
    small_ref[0] = jnp.where(lane < LANE_A, dt, dt * a_neg)

    zx_ref[0, :, :SSD_WIDTH] = seg(COL_Z, SSD_WIDTH)


def _proj(x, mod_l, w_raw, w_tail, w_small, conv_w, conv_b, small_bias, alog_vec, sconv_w, f_bias, *, layer):
    b, seq, d = x.shape
    tm = min(PROJ_TM, seq)
    assert seq % tm == 0
    nt = seq // tm
    tok = lambda width: pl.BlockSpec((1, tm, width), lambda bi, t: (bi, t, 0))
    const2 = lambda shape: pl.BlockSpec(shape, lambda bi, t: (0, 0))
    out_shapes = (
        jax.ShapeDtypeStruct((b, seq, 2 * SSD_WIDTH), F32),
        jax.ShapeDtypeStruct((b, seq, 2 * SSD_BC), BF16),
        jax.ShapeDtypeStruct((b, seq, KVQ_WIDTH), BF16),
        jax.ShapeDtypeStruct((b, seq, SCONV_WIDTH), BF16),
        jax.ShapeDtypeStruct((b, seq, LANES), F32),
    )
    out_specs = (tok(2 * SSD_WIDTH), tok(2 * SSD_BC), tok(KVQ_WIDTH), tok(SCONV_WIDTH), tok(LANES))
    return pl.pallas_call(
        functools.partial(_proj_kernel, tm=tm),
        grid=(b, nt),
        in_specs=[
            tok(d),
            pl.BlockSpec((None, 3 * N_SUB, d), lambda bi, t: (bi, 0, 0)),
            pl.BlockSpec((None, d, RAW_WIDTH), lambda bi, t: (layer, 0, 0), pipeline_mode=pl.Buffered(1)),
            pl.BlockSpec((None, d, LANES), lambda bi, t: (layer, 0, 0), pipeline_mode=pl.Buffered(1)),
            pl.BlockSpec((None, d, 2 * LANES), lambda bi, t: (layer, 0, 0), pipeline_mode=pl.Buffered(1)),
            const2((SSD_CONV, SSD_CONV_DIM)),
            const2((1, SSD_CONV_DIM)),
            const2((1, LANES)),
            const2((1, LANES)),
            const2((SCONV_K, SCONV_WIDTH)),
            const2((1, LANES)),
            const2((PROJ_CUM, PROJ_CUM)),
        ],
        out_specs=out_specs,
        out_shape=out_shapes,
        scratch_shapes=[
            pltpu.VMEM((d, D_PROJ), BF16),
            pltpu.VMEM((tm + SUBLANES, SSD_CONV_DIM), F32),
            pltpu.VMEM((tm + SUBLANES, SCONV_WIDTH), F32),
            pltpu.VMEM((SUBLANES, LANES), F32),
        ],
        compiler_params=_params("arbitrary", "arbitrary"),
    )(x, mod_l, w_raw, w_tail, w_small, conv_w, conv_b, small_bias, alog_vec, sconv_w, f_bias,
      _lower_ones(PROJ_CUM))


def _ssd_kernel(zx_ref, bc_ref, small_ref, dskip_ref, ng_ref, tri_ref, spread_ref,
                o_ref, state_scr, *, t, chunks):
    @pl.when(pl.program_id(1) == 0)
    def _():
        state_scr[...] = jnp.zeros_like(state_scr)

    lane = lax.broadcasted_iota(jnp.int32, (t, LANES), 1)
    row_i = lax.broadcasted_iota(jnp.int32, (t, t), 0)
    col_i = lax.broadcasted_iota(jnp.int32, (t, t), 1)
    lower = row_i >= col_i
    head_of_lane = lax.broadcasted_iota(jnp.int32, (t, GROUP_WIDTH), 1) // SSD_HEAD_DIM

    for ci in range(chunks):
        rows = slice(ci * t, (ci + 1) * t)
        small = small_ref[0, rows, :]
        a_only = jnp.where((lane >= LANE_A) & (lane < LANE_A2 + SSD_HEADS), small, 0.0)
        cs3 = jnp.dot(tri_ref[...], jnp.concatenate(_split_bf16(a_only, 3), axis=1),
                      preferred_element_type=F32)
        cs = cs3[:, :LANES] + cs3[:, LANES:2 * LANES] + cs3[:, 2 * LANES:]
        cs_t = cs.T
        total = cs[t - 1:t, :]
        per_head = jnp.where(lane < LANE_A, small,
                             jnp.where(lane < LANE_A2, jnp.exp(cs), jnp.exp(total - cs)))
        wide = jnp.dot(jnp.concatenate(_split_bf16(per_head, SSD_SPREAD_PIECES), axis=1), spread_ref[...],
                       preferred_element_type=F32)
        dt_x = wide[:, :SSD_WIDTH]
        decay_in = wide[:, SSD_WIDTH:2 * SSD_WIDTH]
        decay_out = wide[:, 2 * SSD_WIDTH:]
        decay_chunk = decay_in[t - 1:t, :]

        xs = zx_ref[0, rows, SSD_WIDTH:]
        xdt = xs * dt_x
        ys = []
        for g in range(SSD_GROUPS):
            gs = slice(g * GROUP_WIDTH, (g + 1) * GROUP_WIDTH)
            ns = slice(g * SSD_STATE, (g + 1) * SSD_STATE)
            bg = bc_ref[0, rows, ns]
            cg = bc_ref[0, rows, slice(SSD_BC + ns.start, SSD_BC + ns.stop)]
            cb = lax.dot_general(cg, bg, (((1,), (1,)), ((), ())), preferred_element_type=F32)
            xg = xdt[:, gs]
            m_parts, x_parts = [], []
            for e in range(SSD_HPG):
                hd = g * SSD_HPG + e
                col = cs[:, LANE_A + hd:LANE_A + hd + 1]
                row = cs_t[LANE_A + hd:LANE_A + hd + 1, :]
                l_mat = jnp.exp(jnp.where(lower, col - row, -jnp.inf))
                m_parts.append((cb * l_mat).astype(BF16))
                x_parts.append(jnp.where(head_of_lane == e, xg, 0.0).astype(BF16))
            m_cat = jnp.concatenate(m_parts, axis=1)
            x_blk = jnp.concatenate(x_parts, axis=0)
            y_diag = jnp.dot(m_cat, x_blk, preferred_element_type=F32)

            state = state_scr[g]
            y_off = jnp.dot(cg, state.astype(BF16), preferred_element_type=F32) * decay_in[:, gs]
            xd = (xg * decay_out[:, gs]).astype(BF16)
            upd = lax.dot_general(bg, xd, (((0,), (0,)), ((), ())), preferred_element_type=F32)
            state_scr[g] = state * decay_chunk[:, gs] + upd
            ys.append(y_diag + y_off + dskip_ref[:, gs] * xs[:, gs])

        z = zx_ref[0, rows, :SSD_WIDTH]
        for g in range(SSD_GROUPS):
            gs = slice(g * GROUP_WIDTH, (g + 1) * GROUP_WIDTH)
            yg = ys[g] * _silu(z[:, gs])
            yg = yg * lax.rsqrt(jnp.mean(yg * yg, axis=-1, keepdims=True) + RMS_EPS)
            o_ref[0, rows, gs] = (yg * ng_ref[:, gs]).astype(BF16)


def _head_spread():
    src = jnp.arange(SSD_SPREAD_PIECES * LANES)[:, None] % LANES
    col = jnp.arange(3 * SSD_WIDTH)[None, :]
    first_lane = jnp.array([LANE_DT, LANE_A, LANE_A2])[col // SSD_WIDTH]
    return (src == first_lane + (col % SSD_WIDTH) // SSD_HEAD_DIM).astype(BF16)


def _ssd(zx, bc, small, d_skip_x, norm_g):
    b, seq, _ = zx.shape
    t = min(SSD_T, seq)
    rows = min(SSD_ROWS, seq)
    assert seq % rows == 0 and rows % t == 0
    tok = lambda width: pl.BlockSpec((1, rows, width), lambda bi, c: (bi, c, 0))
    const2 = lambda shape: pl.BlockSpec(shape, lambda bi, c: (0, 0))
    return pl.pallas_call(
        functools.partial(_ssd_kernel, t=t, chunks=rows // t),
        grid=(b, seq // rows),
        in_specs=[tok(2 * SSD_WIDTH), tok(2 * SSD_BC), tok(LANES),
                  const2((1, SSD_WIDTH)), const2((1, SSD_WIDTH)),
                  const2((t, t)), const2((SSD_SPREAD_PIECES * LANES, 3 * SSD_WIDTH))],
        out_specs=tok(SSD_WIDTH),
        out_shape=jax.ShapeDtypeStruct((b, seq, SSD_WIDTH), BF16),
        scratch_shapes=[pltpu.VMEM((SSD_GROUPS, SSD_STATE, GROUP_WIDTH), F32)],
        compiler_params=_params("arbitrary", "arbitrary"),
    )(zx, bc, small, d_skip_x, norm_g, _lower_ones(t), _head_spread())


def _fox_kernel(q_ref, kv_ref, o_ref, m_scr, acc_scr, *, t):
    qi = pl.program_id(1)
    lane_head = lax.broadcasted_iota(jnp.int32, (t, LANES), 1) // FOX_HEAD_DIM
    heads_per_vreg = LANES // FOX_HEAD_DIM

    def block(start, width, masked):
        ks = pl.ds(start, width)
        group = lambda hd, base=0: slice(base + hd * LANES, base + (hd + 1) * LANES)
        if masked:
            row_i = lax.broadcasted_iota(jnp.int32, (t, width), 0)
            col_i = lax.broadcasted_iota(jnp.int32, (t, width), 1)
            visible = row_i + (width - t) >= col_i

        def logits(hd):
            s = lax.dot_general(q_ref[0, :, group(hd)], kv_ref[0, ks, group(hd, KVQ_K)],
                                (((1,), (1,)), ((), ())), preferred_element_type=F32)
            return jnp.where(visible, s, -jnp.inf) if masked else s

        def update(hd, s):
            m_prev = m_scr[hd]
            m_new = jnp.maximum(m_prev, jnp.max(s, axis=-1, keepdims=True))
            p = jnp.exp2(s - jnp.concatenate([m_new] * (width // LANES), axis=1)).astype(BF16)
            acc_scr[hd] = jnp.exp2(m_prev - m_new) * acc_scr[hd] + jnp.dot(
                p, kv_ref[0, ks, group(hd, KVQ_V)], preferred_element_type=F32)
            m_scr[hd] = m_new

        s_next = logits(0)
        for hd in range(FOX_HEADS):
            s_cur = s_next
            if hd + 1 < FOX_HEADS:
                s_next = logits(hd + 1)
            update(hd, s_cur)

    m_scr[...] = jnp.full(m_scr.shape, -jnp.inf, F32)
    acc_scr[...] = jnp.zeros(acc_scr.shape, F32)

    wide = 2 * t
    assert kv_ref.shape[1] % wide == 0

    @pl.loop(0, qi // 2)
    def _(pj):
        block(pl.multiple_of(pj * wide, wide), wide, False)

    @pl.when(qi % 2 == 1)
    def _():
        block(pl.multiple_of((qi - 1) * t, t), wide, True)

    @pl.when(qi % 2 == 0)
    def _():
        block(pl.multiple_of(qi * t, t), t, True)
    for pair in range(FOX_HEADS // heads_per_vreg):
        out = jnp.zeros((t, LANES), F32)
        for hh in range(heads_per_vreg):
            acc = acc_scr[pair * heads_per_vreg + hh]
            ones_lane = (1 - hh) * FOX_HEAD_DIM
            row_sum = acc[:, ones_lane:ones_lane + 1]
            out = out + jnp.where(lane_head == hh, acc, 0.0) / row_sum
        o_ref[0, :, pair * LANES:(pair + 1) * LANES] = out.astype(BF16)


def _fox(kvq):
    b, seq, _ = kvq.shape
    t = min(FOX_T, seq)
    assert seq % t == 0 and KVQ_K == 0 and KVQ_V == FOX_GROUP and KVQ_Q == 2 * FOX_GROUP
    return pl.pallas_call(
        functools.partial(_fox_kernel, t=t),
        grid=(b, seq // t),
        in_specs=[
            pl.BlockSpec((1, t, FOX_GROUP), lambda bi, i: (bi, i, KVQ_Q // FOX_GROUP)),
            pl.BlockSpec((1, seq, 2 * FOX_GROUP), lambda bi, i: (bi, 0, 0)),
        ],
        out_specs=pl.BlockSpec((1, t, FOX_WIDTH), lambda bi, i: (bi, i, 0)),
        out_shape=jax.ShapeDtypeStruct((b, seq, FOX_WIDTH), BF16),
        scratch_shapes=[pltpu.VMEM((FOX_HEADS, t, LANES), F32), pltpu.VMEM((FOX_HEADS, t, LANES), F32)],
        compiler_params=_params("arbitrary", "arbitrary"),
    )(kvq, kvq)


def _lower_ones(n):
    return jnp.tril(jnp.ones((n, n), BF16))


def _spread_forget(f):
    lead = f.shape[:-1]
    order = sorted(range(FOX_HEADS), key=_forget_lane)
    blocks, lane = [], 0
    for hd in order:
        blocks.append(jnp.zeros(lead + (_forget_lane(hd) - lane,), f.dtype))
        blocks.append(jnp.broadcast_to(f[..., hd:hd + 1], lead + (FOX_SLOT,)))
        lane = _forget_lane(hd) + FOX_SLOT
    blocks.append(jnp.zeros(lead + (LANES - lane,), f.dtype))
    return jnp.concatenate(blocks, axis=-1)


def _mix_w_in_extras(w):
    tail_start = RAW_WIDTH // LANES * LANES
    tail = jnp.pad(w[..., tail_start:], ((0, 0), (0, 0), (0, LANES - (RAW_WIDTH - tail_start))))
    dt_cols = w[..., RAW_DT:RAW_QKV]
    pad = jnp.zeros(w.shape[:-1] + (LANES - 3 * SSD_HEADS,), w.dtype)
    small = jnp.concatenate([dt_cols, dt_cols, dt_cols, pad, _spread_forget(w[..., RAW_F:RAW_SC])], axis=-1)
    return tail, small.astype(BF16)


def _lane_vec(pieces):
    row = jnp.zeros((LANES,), F32)
    for off, vec in pieces:
        row = row.at[off:off + vec.shape[0]].set(vec.astype(F32))
    return row.reshape(1, LANES)


def kernel(x, c, ln_in_g, ln_in_b, ada_w, ada_b, ffn1_w_in, ffn1_w_out, mix_w_in, mix_w_out, ssd_conv_w, ssd_conv_b, ssd_dt_bias, ssd_a_log, ssd_d, ssd_norm_g, fox_f_bias, sconv_w, ffn2_w_in, ffn2_w_out, ln_g, ln_b):
    b, seq, d = x.shape
    depth = ada_w.shape[0]
    mod = _ada(c, ada_w, ada_b)
    mod = mod.reshape(depth, mod.shape[1], 3 * N_SUB, d)

    w1_in, w1_out, w2_in, w2_out = ffn1_w_in, ffn1_w_out, ffn2_w_in, ffn2_w_out
    wm_tail, wm_small = _mix_w_in_extras(mix_w_in)
    wm_out = mix_w_out.astype(BF16)

    x2d = x.reshape(b * seq, d)
    for l in range(depth):
        mod_l = mod[l]
        x2d = _ffn(x2d, mod_l, w1_in, w1_out, ln_g[l, 0], ln_b[l, 0], layer=l, sub=0, seq=seq,
                   pre=(ln_in_g, ln_in_b) if l == 0 else None)

        small_bias = _lane_vec([(lane, ssd_dt_bias[l]) for lane in (LANE_DT, LANE_A, LANE_A2)])
        alog_vec = _lane_vec([(LANE_A, ssd_a_log[l]), (LANE_A2, ssd_a_log[l])])
        zx, bc, kvq, y_sc, small = _proj(
            x2d.reshape(b, seq, d), mod_l, mix_w_in, wm_tail, wm_small,
            ssd_conv_w[l], ssd_conv_b[l].reshape(1, SSD_CONV_DIM), small_bias, alog_vec, sconv_w[l],
            _spread_forget(fox_f_bias[l]).reshape(1, LANES), layer=l)
        d_skip_x = jnp.repeat(ssd_d[l], SSD_HEAD_DIM).reshape(1, SSD_WIDTH)
        y_ssd = _ssd(zx, bc, small, d_skip_x, ssd_norm_g[l].reshape(1, SSD_WIDTH))
        y_fox = _fox(kvq)
        mixer_tail = (y_ssd.reshape(b * seq, SSD_WIDTH), y_fox.reshape(b * seq, FOX_WIDTH),
                      y_sc.reshape(b * seq, SCONV_WIDTH), wm_out, ln_g[l, 1], ln_b[l, 1])
        x2d = _ffn(x2d, mod_l, w2_in, w2_out, ln_g[l, 2], ln_b[l, 2], layer=l, sub=2, seq=seq,
                   mix=mixer_tail)
    return x2d.reshape(b, seq, d)
```
